```python
import math
import jax
import jax.numpy as jnp
from jax import lax
import numpy as np

D_MODEL = 2048
BATCH = 4
SEQ = 4096
DEPTH = 4

GRID_W = 64
CTX_LEN = 256
EPS = 1e-6
N_MOD = 6

ATTN_HEADS = 8
ATTN_KV_HEADS = 2
ATTN_REP = ATTN_HEADS // ATTN_KV_HEADS
HEAD_DIM = 128
ATTN_WIDTH = ATTN_HEADS * HEAD_DIM
KV_WIDTH = ATTN_KV_HEADS * HEAD_DIM
ROPE_AXIS_DIM = HEAD_DIM // 2
ROPE_FREQS = ROPE_AXIS_DIM // 2
ROPE_THETA = 10000.0
Q_BLOCK = 128
ATTN_SCALE = HEAD_DIM ** -0.5

CONF_WIDTH = 1024
CONF_KERNEL = 31

SSM_HEADS = 16
SSM_HEAD_DIM = 64
SSM_WIDTH = SSM_HEADS * SSM_HEAD_DIM
SSM_GROUPS = 2
SSM_HEADS_PER_GROUP = SSM_HEADS // SSM_GROUPS
SSM_STATE = 128
SSM_XBC = SSM_WIDTH + 2 * SSM_GROUPS * SSM_STATE
SSM_CONV = 5
SSM_CHUNK = 128

D_FF = 5632
FFN_CONV = 3

N_BRANCHES = 3
Q_END = ATTN_WIDTH
K_END = Q_END + KV_WIDTH
V_END = K_END + KV_WIDTH
CONF_END = V_END + 2 * CONF_WIDTH
Z_END = CONF_END + SSM_WIDTH
XBC_END = Z_END + SSM_XBC
DT_END = XBC_END + 2 * SSM_HEADS
IN_WIDTH = DT_END + N_BRANCHES * D_MODEL
IN_SPLITS = (Q_END, K_END, V_END, CONF_END, Z_END, XBC_END, DT_END)

kernel_name = "hybrid_gated_attn_conformer_ssd_dit"


def rmsnorm(x, w):
    x32 = x.astype(jnp.float32)
    y = x32 * lax.rsqrt(jnp.mean(jnp.square(x32), axis=-1, keepdims=True) + EPS)
    return (y * w.astype(jnp.float32)).astype(x.dtype)


def layernorm(x, w, b):
    x32 = x.astype(jnp.float32)
    mu = jnp.mean(x32, axis=-1, keepdims=True)
    var = jnp.mean(jnp.square(x32 - mu), axis=-1, keepdims=True)
    y = (x32 - mu) * lax.rsqrt(var + EPS) * w.astype(jnp.float32) + b.astype(jnp.float32)
    return y.astype(x.dtype)


def modulate(h, shift, scale):
    return h * (1 + scale) + shift


def depthwise_conv(x, w, b):
    k, ch = w.shape
    pad = (k - 1) // 2
    y = lax.conv_general_dilated(x, w[:, None, :], window_strides=(1,), padding=[(pad, pad)],
                                 dimension_numbers=('NWC', 'WIO', 'NWC'), feature_group_count=ch)
    return y + b


def axial_rope_tables(n_tokens):
    rows = n_tokens // GRID_W
    row = jnp.repeat(jnp.arange(rows), GRID_W).astype(jnp.float32)
    col = jnp.tile(jnp.arange(GRID_W), rows).astype(jnp.float32)
    inv = ROPE_THETA ** (-jnp.arange(0, ROPE_AXIS_DIM, 2, dtype=jnp.float32) / ROPE_AXIS_DIM)
    ang = jnp.stack([row[:, None] * inv, col[:, None] * inv], axis=1)
    return jnp.cos(ang), jnp.sin(ang)


def apply_rope(x, cos, sin):
    xr = x.reshape(*x.shape[:-1], 2, 2, ROPE_FREQS)
    x1, x2 = xr[..., 0, :], xr[..., 1, :]
    c, s = cos[None, :, None], sin[None, :, None]
    out = jnp.stack([x1 * c - x2 * s, x2 * c + x1 * s], axis=-2)
    return out.reshape(x.shape).astype(x.dtype)


def attend_blocks(q, k, v):
    b, lq = q.shape[:2]
    nb = lq // Q_BLOCK
    qb = q.reshape(b, nb, Q_BLOCK, ATTN_KV_HEADS, ATTN_REP, HEAD_DIM).swapaxes(0, 1)

    def one_block(qi):
        s = jnp.einsum('bqgrd,bkgd->bgrqk', qi, k, preferred_element_type=jnp.float32) * ATTN_SCALE
        p = jax.nn.softmax(s, axis=-1)
        return jnp.einsum('bgrqk,bkgd->bqgrd', p.astype(v.dtype), v)

    o = lax.map(one_block, qb)
    return o.swapaxes(0, 1).reshape(b, lq, ATTN_WIDTH)


def conformer_conv(u, conv_w, conv_b, ln_w, ln_b, w_o):
    a, g = jnp.split(u, 2, axis=-1)
    v = depthwise_conv(a * jax.nn.sigmoid(g), conv_w, conv_b)
    v = jax.nn.silu(layernorm(v, ln_w, ln_b))
    return v @ w_o


def ssd_inputs(xbc, dt_raw, conv_w, conv_b, dt_bias):
    b, n = xbc.shape[:2]
    xbc = jax.nn.silu(depthwise_conv(xbc, conv_w, conv_b))
    xs, bm, cm = jnp.split(xbc, [SSM_WIDTH, SSM_WIDTH + SSM_GROUPS * SSM_STATE], axis=-1)
    xs = xs.reshape(b, n, SSM_HEADS, SSM_HEAD_DIM)
    bm = bm.reshape(b, n, SSM_GROUPS, SSM_STATE)
    cm = cm.reshape(b, n, SSM_GROUPS, SSM_STATE)
    dt = jax.nn.softplus(dt_raw.astype(jnp.float32).reshape(b, n, 2, SSM_HEADS) + dt_bias.astype(jnp.float32))
    return xs, bm, cm, dt


def ssd_scan(x, dt, a_coef, bm, cm, h0, with_output):
    b, n = x.shape[:2]
    nc = n // SSM_CHUNK
    shp = (b, nc, SSM_CHUNK, SSM_GROUPS, SSM_HEADS_PER_GROUP)
    xc = x.reshape(*shp, SSM_HEAD_DIM)
    bc = bm.reshape(b, nc, SSM_CHUNK, SSM_GROUPS, SSM_STATE)
    cc = cm.reshape(b, nc, SSM_CHUNK, SSM_GROUPS, SSM_STATE)
    dtc = dt.reshape(shp)
    acum = jnp.cumsum(dtc * a_coef.reshape(SSM_GROUPS, SSM_HEADS_PER_GROUP), axis=2)
    xdt = xc * dtc[..., None]
    decay_to_end = jnp.exp(acum[:, :, -1:] - acum)
    states = jnp.einsum('bcjgn,bcjgrp->bcgrpn', bc, xdt * decay_to_end[..., None])
    chunk_decay = jnp.exp(acum[:, :, -1])

    def step(h, inp):
        s, dcy = inp
        return h * dcy[..., None, None] + s, h

    h_final, h_in = lax.scan(step, h0, (states.swapaxes(0, 1), chunk_decay.swapaxes(0, 1)))
    if not with_output:
        return None, h_final
    h_in = h_in.swapaxes(0, 1)
    seg = acum[:, :, :, None] - acum[:, :, None, :]
    lower = jnp.tril(jnp.ones((SSM_CHUNK, SSM_CHUNK), dtype=bool))[:, :, None, None]
    decay = jnp.exp(jnp.where(lower, seg, -jnp.inf))
    cb = jnp.einsum('bcign,bcjgn->bcijg', cc, bc)
    y_intra = jnp.einsum('bcijgr,bcjgrp->bcigrp', cb[..., None] * decay, xdt)
    y_inter = jnp.einsum('bcign,bcgrpn->bcigrp', cc, h_in) * jnp.exp(acum)[..., None]
    y = (y_intra + y_inter).reshape(b, n, SSM_HEADS, SSM_HEAD_DIM)
    return y, h_final


def ssd_mixer(xbc_lat, dt_lat, z_lat, xbc_ctx, dt_ctx, z_ctx, conv_w, conv_b, a_log, dt_bias, d_skip,
              norm_w, w_o, need_ctx_out):
    xs_l, b_l, c_l, dt_l = ssd_inputs(xbc_lat, dt_lat, conv_w, conv_b, dt_bias)
    xs_c, b_c, c_c, dt_c = ssd_inputs(xbc_ctx, dt_ctx, conv_w, conv_b, dt_bias)
    a_coef = -jnp.exp(a_log.astype(jnp.float32))
    bsz = xs_l.shape[0]
    h0 = jnp.zeros((bsz, SSM_GROUPS, SSM_HEADS_PER_GROUP, SSM_HEAD_DIM, SSM_STATE), jnp.float32)

    def rev(t):
        return jnp.flip(t, axis=1)

    y_cf, h_cf = ssd_scan(xs_c, dt_c[:, :, 0], a_coef[0], b_c, c_c, h0, need_ctx_out)
    y_lf, _ = ssd_scan(xs_l, dt_l[:, :, 0], a_coef[0], b_l, c_l, h_cf, True)
    y_cb, h_cb = ssd_scan(rev(xs_c), rev(dt_c[:, :, 1]), a_coef[1], rev(b_c), rev(c_c), h0, need_ctx_out)
    y_lb, _ = ssd_scan(rev(xs_l), rev(dt_l[:, :, 1]), a_coef[1], rev(b_l), rev(c_l), h_cb, True)
    d_sum = (d_skip[0] + d_skip[1])[:, None]

    def gated_out(y, xs, z):
        y = (y + d_sum * xs).astype(z.dtype).reshape(*z.shape[:2], SSM_WIDTH)
        return rmsnorm(y * jax.nn.silu(z), norm_w) @ w_o

    out_lat = gated_out(y_lf + rev(y_lb), xs_l, z_lat)
    out_ctx = gated_out(y_cf + rev(y_cb), xs_c, z_ctx) if need_ctx_out else None
    return out_lat, out_ctx


def token_mixer(h_lat, h_ctx, w_in, q_norm_w, k_norm_w, w_attn_o, conf_conv_w, conf_conv_b, conf_ln_w,
                conf_ln_b, w_conf_o, ssm_conv_w, ssm_conv_b, ssm_a_log, ssm_dt_bias, ssm_d, ssm_norm_w,
                w_ssm_o, w_out, rope_cos, rope_sin, need_ctx_out):
    bsz, n_lat = h_lat.shape[:2]
    n_ctx = h_ctx.shape[1]
    q_l, k_l, v_l, conf_l, z_l, xbc_l, dt_l, gate_l = jnp.split(h_lat @ w_in, IN_SPLITS, axis=-1)
    q_c, k_c, v_c, conf_c, z_c, xbc_c, dt_c, gate_c = jnp.split(h_ctx @ w_in, IN_SPLITS, axis=-1)

    q_l = apply_rope(rmsnorm(q_l.reshape(bsz, n_lat, ATTN_HEADS, HEAD_DIM), q_norm_w), rope_cos, rope_sin)
    k_l = apply_rope(rmsnorm(k_l.reshape(bsz, n_lat, ATTN_KV_HEADS, HEAD_DIM), k_norm_w), rope_cos, rope_sin)
    v_l = v_l.reshape(bsz, n_lat, ATTN_KV_HEADS, HEAD_DIM)
    q_c = rmsnorm(q_c.reshape(bsz, n_ctx, ATTN_HEADS, HEAD_DIM), q_norm_w)
    k_c = rmsnorm(k_c.reshape(bsz, n_ctx, ATTN_KV_HEADS, HEAD_DIM), k_norm_w)
    v_c = v_c.reshape(bsz, n_ctx, ATTN_KV_HEADS, HEAD_DIM)
    k_all = jnp.concatenate([k_c, k_l], axis=1)
    v_all = jnp.concatenate([v_c, v_l], axis=1)
    q_l = q_l.reshape(bsz, n_lat, ATTN_KV_HEADS, ATTN_REP, HEAD_DIM)
    ya_l = attend_blocks(q_l, k_all, v_all) @ w_attn_o

    yb_l = conformer_conv(conf_l, conf_conv_w, conf_conv_b, conf_ln_w, conf_ln_b, w_conf_o)

    yc_l, yc_c = ssd_mixer(xbc_l, dt_l, z_l, xbc_c, dt_c, z_c, ssm_conv_w, ssm_conv_b, ssm_a_log, ssm_dt_bias,
                           ssm_d, ssm_norm_w, w_ssm_o, need_ctx_out)

    def merge(ya, yb, yc, gates):
        g = jax.nn.sigmoid(gates.reshape(*gates.shape[:2], N_BRANCHES, D_MODEL))
        return (g[:, :, 0] * ya + g[:, :, 1] * yb + g[:, :, 2] * yc) @ w_out

    out_lat = merge(ya_l, yb_l, yc_l, gate_l)
    if not need_ctx_out:
        return out_lat, None
    q_c = q_c.reshape(bsz, n_ctx, ATTN_KV_HEADS, ATTN_REP, HEAD_DIM)
    ya_c = attend_blocks(q_c, k_c, v_c) @ w_attn_o
    yb_c = conformer_conv(conf_c, conf_conv_w, conf_conv_b, conf_ln_w, conf_ln_b, w_conf_o)
    out_ctx = merge(ya_c, yb_c, yc_c, gate_c)
    return out_lat, out_ctx


def conv_ffn(h, w_up, conv_w, conv_b, w_down):
    u = depthwise_conv(h @ w_up, conv_w, conv_b)
    g, v = jnp.split(u, 2, axis=-1)
    return (jax.nn.silu(g) * v) @ w_down


def setup_inputs(seed: int = 0) -> dict:
    key = jax.random.key(seed)
    ks = iter(jax.random.split(key, 48))

    def nrm(shape, scale):
        return jax.random.normal(next(ks), shape, jnp.float32) * scale

    def gain(shape):
        return 1.0 + nrm(shape, 0.02)

    L = DEPTH
    dt0 = jnp.exp(jax.random.uniform(next(ks), (L, 2, SSM_HEADS), jnp.float32,
                                     minval=math.log(1e-3), maxval=math.log(1e-1)))
    dt_bias = dt0 + jnp.log(-jnp.expm1(-dt0))
    a_log = jnp.log(jax.random.uniform(next(ks), (L, 2, SSM_HEADS), jnp.float32, minval=1.0, maxval=16.0))
    return {
        'x': nrm((BATCH, SEQ, D_MODEL), 1.0),
        'c': nrm((BATCH, D_MODEL), 1.0),
        'ctx': nrm((BATCH, CTX_LEN, D_MODEL), 1.0),
        'c_ctx': nrm((D_MODEL,), 1.0),
        'w_mod': nrm((L, D_MODEL, N_MOD * D_MODEL), 0.5 * D_MODEL ** -0.5),
        'b_mod': nrm((L, N_MOD * D_MODEL), 0.01),
        'norm_mix_w': gain((L, D_MODEL)),
        'norm_ffn_w': gain((L, D_MODEL)),
        'w_in': nrm((L, D_MODEL, IN_WIDTH), D_MODEL ** -0.5),
        'q_norm_w': gain((L, HEAD_DIM)),
        'k_norm_w': gain((L, HEAD_DIM)),
        'w_attn_o': nrm((L, ATTN_WIDTH, D_MODEL), ATTN_WIDTH ** -0.5),
        'conf_conv_w': nrm((L, CONF_KERNEL, CONF_WIDTH), CONF_KERNEL ** -0.5),
        'conf_conv_b': nrm((L, CONF_WIDTH), 0.01),
        'conf_ln_w': gain((L, CONF_WIDTH)),
        'conf_ln_b': nrm((L, CONF_WIDTH), 0.01),
        'w_conf_o': nrm((L, CONF_WIDTH, D_MODEL), CONF_WIDTH ** -0.5),
        'ssm_conv_w': nrm((L, SSM_CONV, SSM_XBC), SSM_CONV ** -0.5),
        'ssm_conv_b': nrm((L, SSM_XBC), 0.01),
        'ssm_a_log': a_log,
        'ssm_dt_bias': dt_bias,
        'ssm_d': gain((L, 2, SSM_HEADS)),
        'ssm_norm_w': gain((L, SSM_WIDTH)),
        'w_ssm_o': nrm((L, SSM_WIDTH, D_MODEL), SSM_WIDTH ** -0.5),
        'w_out': nrm((L, D_MODEL, D_MODEL), D_MODEL ** -0.5),
        'ffn_w_up': nrm((L, D_MODEL, 2 * D_FF), D_MODEL ** -0.5),
        'ffn_conv_w': nrm((L, FFN_CONV, 2 * D_FF), FFN_CONV ** -0.5),
        'ffn_conv_b': nrm((L, 2 * D_FF), 0.01),
        'ffn_w_down': nrm((L, D_FF, D_MODEL), D_FF ** -0.5),
        'final_norm_w': gain((D_MODEL,)),
    }


def reference(x, c, ctx, c_ctx, w_mod, b_mod, norm_mix_w, norm_ffn_w, w_in, q_norm_w, k_norm_w, w_attn_o,
              conf_conv_w, conf_conv_b, conf_ln_w, conf_ln_b, w_conf_o, ssm_conv_w, ssm_conv_b, ssm_a_log,
              ssm_dt_bias, ssm_d, ssm_norm_w, w_ssm_o, w_out, ffn_w_up, ffn_conv_w, ffn_conv_b, ffn_w_down,
              final_norm_w):
    bsz = x.shape[0]
    rope_cos, rope_sin = axial_rope_tables(x.shape[1])
    for l in range(DEPTH):
        need_ctx = l < DEPTH - 1
        m_lat = (jax.nn.silu(c) @ w_mod[l] + b_mod[l]).reshape(bsz, 1, N_MOD, D_MODEL)
        m_ctx = (jax.nn.silu(c_ctx) @ w_mod[l] + b_mod[l]).reshape(1, 1, N_MOD, D_MODEL)
        h_lat = modulate(rmsnorm(x, norm_mix_w[l]), m_lat[:, :, 0], m_lat[:, :, 1])
        h_ctx = modulate(rmsnorm(ctx, norm_mix_w[l]), m_ctx[:, :, 0], m_ctx[:, :, 1])
        y_lat, y_ctx = token_mixer(h_lat, h_ctx, w_in[l], q_norm_w[l], k_norm_w[l], w_attn_o[l], conf_conv_w[l],
                                   conf_conv_b[l], conf_ln_w[l], conf_ln_b[l], w_conf_o[l], ssm_conv_w[l],
                                   ssm_conv_b[l], ssm_a_log[l], ssm_dt_bias[l], ssm_d[l], ssm_norm_w[l],
                                   w_ssm_o[l], w_out[l], rope_cos, rope_sin, need_ctx)
        x = x + m_lat[:, :, 2] * y_lat
        h_lat = modulate(rmsnorm(x, norm_ffn_w[l]), m_lat[:, :, 3], m_lat[:, :, 4])
        x = x + m_lat[:, :, 5] * conv_ffn(h_lat, ffn_w_up[l], ffn_conv_w[l], ffn_conv_b[l], ffn_w_down[l])
        if need_ctx:
            ctx = ctx + m_ctx[:, :, 2] * y_ctx
            h_ctx = modulate(rmsnorm(ctx, norm_ffn_w[l]), m_ctx[:, :, 3], m_ctx[:, :, 4])
            ctx = ctx + m_ctx[:, :, 5] * conv_ffn(h_ctx, ffn_w_up[l], ffn_conv_w[l], ffn_conv_b[l], ffn_w_down[l])
    return rmsnorm(x, final_norm_w)
```

```python
import functools
import math

import jax
import jax.numpy as jnp
from jax import lax
from jax.experimental import pallas as pl
from jax.experimental.pallas import tpu as pltpu

F32 = jnp.float32
BF16 = jnp.bfloat16

EPS = 1e-6
N_MOD = 6
GRID_W = 64
HEAD_DIM = 128
ATTN_HEADS = 8
ATTN_KV_HEADS = 2
ATTN_REP = ATTN_HEADS // ATTN_KV_HEADS
ATTN_WIDTH = ATTN_HEADS * HEAD_DIM
KV_WIDTH = ATTN_KV_HEADS * HEAD_DIM
ROPE_THETA = 10000.0
ROPE_AXIS_DIM = HEAD_DIM // 2
ROPE_FREQS = ROPE_AXIS_DIM // 2
CONF_WIDTH = 1024
CONF_KERNEL = 31
SSM_HEADS = 16
SSM_HEAD_DIM = 64
SSM_WIDTH = SSM_HEADS * SSM_HEAD_DIM
SSM_GROUPS = 2
SSM_STATE = 128
SSM_GROUP_WIDTH = SSM_WIDTH // SSM_GROUPS
SSM_BC = 2 * SSM_GROUPS * SSM_STATE
SSM_CONV = 5
SSM_CHUNK = 128
FFN_CONV = 3
N_BRANCHES = 3

LANES = 128
HALO = 16
VMEM_LIMIT = 56 * 1024 * 1024
LOG2E = 1.4426950408889634
NEG_BIG = -1e30

COL_Q = 0
COL_A = 1024
COL_G = 2048
COL_Z = 3072
COL_XS = 4096
COL_KV = 5120
COL_BC = 5632
COL_DT = 6144
MIX_WIDTH = 6272


def _cparams(sem):
    return pltpu.CompilerParams(dimension_semantics=sem, vmem_limit_bytes=VMEM_LIMIT)


def _sigmoid(x):
    return 1.0 / (1.0 + jnp.exp(-x))


def _silu(x):
    return x * _sigmoid(x)


def _rms_mod(x, nw, shift, scale):
    var = jnp.mean(x * x, axis=-1, keepdims=True)
    y = x * lax.rsqrt(var + EPS) * nw
    return y * (1.0 + scale) + shift


def _mod_kernel(c_ref, w_ref, b_ref, o_ref):
    s = _silu(c_ref[...]).astype(BF16)
    o_ref[0] = jnp.dot(s, w_ref[0].astype(BF16), preferred_element_type=F32) + b_ref[0]


def _mod_vectors(cc, w_mod, b_mod):
    depth, d, nm = w_mod.shape
    tn = 1024
    return pl.pallas_call(
        _mod_kernel,
        grid=(depth, nm // tn),
        in_specs=[
            pl.BlockSpec((8, d), lambda l, j: (0, 0)),
            pl.BlockSpec((1, d, tn), lambda l, j: (l, 0, j)),
            pl.BlockSpec((1, 1, tn), lambda l, j: (l, 0, j)),
        ],
        out_specs=pl.BlockSpec((1, 8, tn), lambda l, j: (l, 0, j)),
        out_shape=jax.ShapeDtypeStruct((depth, 8, nm), F32),
        compiler_params=_cparams(("parallel", "parallel")),
        name="mod_vectors",
    )(cc, w_mod, b_mod.reshape(depth, 1, nm))


def _inproj_kernel(x_ref, nw_ref, sh_ref, sc_ref, w_ref, o_ref, h_ref):
    @pl.when(pl.program_id(2) == 0)
    def _():
        h_ref[...] = _rms_mod(x_ref[0], nw_ref[...], sh_ref[0], sc_ref[0]).astype(BF16)

    o_ref[0] = jnp.dot(h_ref[...], w_ref[...], preferred_element_type=F32)


def _inproj(x, nw, shift, scale, w, tm):
    b, t, d = x.shape
    n = w.shape[1]
    tn = 896
    return pl.pallas_call(
        _inproj_kernel,
        grid=(b, t // tm, n // tn),
        in_specs=[
            pl.BlockSpec((1, tm, d), lambda bb, i, j: (bb, i, 0)),
            pl.BlockSpec((1, d), lambda bb, i, j: (0, 0)),
            pl.BlockSpec((1, 1, d), lambda bb, i, j: (bb, 0, 0)),
            pl.BlockSpec((1, 1, d), lambda bb, i, j: (bb, 0, 0)),
            pl.BlockSpec((d, tn), lambda bb, i, j: (0, j)),
        ],
        out_specs=pl.BlockSpec((1, tm, tn), lambda bb, i, j: (bb, i, j)),
        out_shape=jax.ShapeDtypeStruct((b, t, n), F32),
        scratch_shapes=[pltpu.VMEM((tm, d), BF16)],
        compiler_params=_cparams(("parallel", "parallel", "arbitrary")),
        name="mixer_inproj",
    )(x, nw, shift, scale, w)


def _head_norm(x, w):
    var = jnp.mean(x * x, axis=-1, keepdims=True)
    return x * lax.rsqrt(var + EPS) * w


def _rope(x, c, s):
    lane = lax.broadcasted_iota(jnp.int32, x.shape, 1)
    first = (lane % ROPE_AXIS_DIM) < ROPE_FREQS
    partner = jnp.where(first, pltpu.roll(x, HEAD_DIM - ROPE_FREQS, 1), pltpu.roll(x, ROPE_FREQS, 1))
    return x * c + partner * s


def _qkprep_kernel(*refs, use_rope):
    if use_rope:
        q_ref, kv_ref, qw_ref, kw_ref, c_ref, s_ref, qo_ref, ko_ref, vo_ref = refs
    else:
        q_ref, kv_ref, qw_ref, kw_ref, qo_ref, ko_ref, vo_ref = refs
    q = q_ref[0]
    kv = kv_ref[0]
    for h in range(ATTN_HEADS):
        y = _head_norm(q[:, h * HEAD_DIM:(h + 1) * HEAD_DIM], qw_ref[...])
        if use_rope:
            y = _rope(y, c_ref[...], s_ref[...])
        qo_ref[0, :, h * HEAD_DIM:(h + 1) * HEAD_DIM] = y.astype(BF16)
    for h in range(ATTN_KV_HEADS):
        y = _head_norm(kv[:, h * HEAD_DIM:(h + 1) * HEAD_DIM], kw_ref[...])
        if use_rope:
            y = _rope(y, c_ref[...], s_ref[...])
        ko_ref[0, :, h * HEAD_DIM:(h + 1) * HEAD_DIM] = y.astype(BF16)
    vo_ref[0] = kv[:, KV_WIDTH:].astype(BF16)


def _qkprep(proj, qw, kw, rope, tm):
    b, t, _ = proj.shape
    use_rope = rope is not None
    in_specs = [
        pl.BlockSpec((1, tm, ATTN_WIDTH), lambda bb, i: (bb, i, COL_Q // ATTN_WIDTH)),
        pl.BlockSpec((1, tm, 2 * KV_WIDTH), lambda bb, i: (bb, i, COL_KV // (2 * KV_WIDTH))),
        pl.BlockSpec((1, HEAD_DIM), lambda bb, i: (0, 0)),
        pl.BlockSpec((1, HEAD_DIM), lambda bb, i: (0, 0)),
    ]
    args = [proj, proj, qw, kw]
    if use_rope:
        in_specs += [pl.BlockSpec((tm, HEAD_DIM), lambda bb, i: (i, 0))] * 2
        args += list(rope)
    return pl.pallas_call(
        functools.partial(_qkprep_kernel, use_rope=use_rope),
        grid=(b, t // tm),
        in_specs=in_specs,
        out_specs=[
            pl.BlockSpec((1, tm, ATTN_WIDTH), lambda bb, i: (bb, i, 0)),
            pl.BlockSpec((1, tm, KV_WIDTH), lambda bb, i: (bb, i, 0)),
            pl.BlockSpec((1, tm, KV_WIDTH), lambda bb, i: (bb, i, 0)),
        ],
        out_shape=[
            jax.ShapeDtypeStruct((b, t, ATTN_WIDTH), BF16),
            jax.ShapeDtypeStruct((b, t, KV_WIDTH), BF16),
            jax.ShapeDtypeStruct((b, t, KV_WIDTH), BF16),
        ],
        compiler_params=_cparams(("parallel", "parallel")),
        name="qk_prep",
    )(*args)


def _attn_kernel(q_ref, k_ref, v_ref, o_ref, *, tq, tk, nk):
    q = q_ref[0]
    qs = jnp.concatenate([q[:, r * HEAD_DIM:(r + 1) * HEAD_DIM] for r in range(ATTN_REP)], axis=0)
    rows = ATTN_REP * tq

    def body(j, carry):
        m, l, acc = carry
        start = pl.multiple_of(j * tk, tk)
        kj = k_ref[0, pl.ds(start, tk), :]
        vj = v_ref[0, pl.ds(start, tk), :]
        s = lax.dot_general(qs, kj, (((1,), (1,)), ((), ())), preferred_element_type=F32)
        s = s * (HEAD_DIM ** -0.5 * LOG2E)
        m_new = jnp.maximum(m, jnp.max(s, axis=-1, keepdims=True))
        alpha = jnp.exp2(m - m_new)
        p = jnp.exp2(s - m_new)
        l = alpha * l + jnp.sum(p, axis=-1, keepdims=True)
        acc = alpha * acc + jnp.dot(p.astype(BF16), vj, preferred_element_type=F32)
        return m_new, l, acc

    init = (jnp.full((rows, 1), NEG_BIG, F32), jnp.zeros((rows, 1), F32), jnp.zeros((rows, HEAD_DIM), F32))
    _, l, acc = lax.fori_loop(0, nk, body, init)
    o = acc / l
    o_ref[0] = jnp.concatenate([o[r * tq:(r + 1) * tq] for r in range(ATTN_REP)], axis=1).astype(BF16)


def _attention(q, k, v, tq, tk):
    b, t, _ = q.shape
    tkv = k.shape[1]
    gw = ATTN_REP * HEAD_DIM
    return pl.pallas_call(
        functools.partial(_attn_kernel, tq=tq, tk=tk, nk=tkv // tk),
        grid=(b, ATTN_KV_HEADS, t // tq),
        in_specs=[
            pl.BlockSpec((1, tq, gw), lambda bb, g, i: (bb, i, g)),
            pl.BlockSpec((1, tkv, HEAD_DIM), lambda bb, g, i: (bb, 0, g)),
            pl.BlockSpec((1, tkv, HEAD_DIM), lambda bb, g, i: (bb, 0, g)),
        ],
        out_specs=pl.BlockSpec((1, tq, gw), lambda bb, g, i: (bb, i, g)),
        out_shape=jax.ShapeDtypeStruct((b, t, ATTN_WIDTH), BF16),
        compiler_params=_cparams(("parallel", "parallel", "arbitrary")),
        name="attention",
    )(q, k, v)


def _halo_specs(tm, t, width, col_block):
    per = tm // HALO
    last = t // HALO - 1
    return [
        pl.BlockSpec((1, tm, width), lambda bb, i: (bb, i, col_block)),
        pl.BlockSpec((1, HALO, width), lambda bb, i: (bb, jnp.maximum(i * per - 1, 0), col_block)),
        pl.BlockSpec((1, HALO, width), lambda bb, i: (bb, jnp.minimum((i + 1) * per, last), col_block)),
    ]


def _conv_rows(ext_ref, w_ref, taps, first_row, tm, width, out_fn, rows_per=64):
    for c0 in range(0, width, LANES):
        for r0 in range(0, tm, rows_per):
            acc = None
            for k in range(taps):
                lo = first_row + r0 + k
                term = ext_ref[lo:lo + rows_per, c0:c0 + LANES] * w_ref[k:k + 1, c0:c0 + LANES]
                acc = term if acc is None else acc + term
            out_fn(r0, rows_per, c0, acc)


def _conf_kernel(a_ref, ap_ref, an_ref, g_ref, gp_ref, gn_ref, cw_ref, cb_ref, lw_ref, lb_ref,
                 o_ref, u_ref, v_ref, *, tm):
    i = pl.program_id(1)
    nt = pl.num_programs(1)
    u_ref[HALO:HALO + tm] = a_ref[0] * _sigmoid(g_ref[0])
    u_ref[0:HALO] = jnp.where(i > 0, ap_ref[0] * _sigmoid(gp_ref[0]), 0.0)
    u_ref[HALO + tm:] = jnp.where(i < nt - 1, an_ref[0] * _sigmoid(gn_ref[0]), 0.0)

    def put(r0, nr, c0, acc):
        v_ref[r0:r0 + nr, c0:c0 + LANES] = acc + cb_ref[:, c0:c0 + LANES]

    pad = (CONF_KERNEL - 1) // 2
    _conv_rows(u_ref, cw_ref, CONF_KERNEL, HALO - pad, tm, CONF_WIDTH, put)
    v = v_ref[...]
    mu = jnp.mean(v, axis=-1, keepdims=True)
    d = v - mu
    var = jnp.mean(d * d, axis=-1, keepdims=True)
    y = d * lax.rsqrt(var + EPS) * lw_ref[...] + lb_ref[...]
    o_ref[0] = _silu(y).astype(BF16)


def _conformer(proj, cw, cb, lw, lb, tm):
    b, t, _ = proj.shape
    vec = pl.BlockSpec((1, CONF_WIDTH), lambda bb, i: (0, 0))
    return pl.pallas_call(
        functools.partial(_conf_kernel, tm=tm),
        grid=(b, t // tm),
        in_specs=_halo_specs(tm, t, CONF_WIDTH, COL_A // CONF_WIDTH)
        + _halo_specs(tm, t, CONF_WIDTH, COL_G // CONF_WIDTH)
        + [pl.BlockSpec((CONF_KERNEL, CONF_WIDTH), lambda bb, i: (0, 0)), vec, vec, vec],
        out_specs=pl.BlockSpec((1, tm, CONF_WIDTH), lambda bb, i: (bb, i, 0)),
        out_shape=jax.ShapeDtypeStruct((b, t, CONF_WIDTH), BF16),
        scratch_shapes=[pltpu.VMEM((tm + 2 * HALO, CONF_WIDTH), F32), pltpu.VMEM((tm, CONF_WIDTH), F32)],
        compiler_params=_cparams(("parallel", "parallel")),
        name="conformer",
    )(proj, proj, proj, proj, proj, proj, cw, cb, lw, lb)


def _ssdprep_kernel(x_ref, xp_ref, xn_ref, bc_ref, bcp_ref, bcn_ref, dt_ref, cwx_ref, cbx_ref,
                    cwb_ref, cbb_ref, dtb_ref, xo_ref, bco_ref, bmt_ref, dto_ref, xe_ref, be_ref, bv_ref, *, tm):
    i = pl.program_id(1)
    nt = pl.num_programs(1)
    xe_ref[HALO:HALO + tm] = x_ref[0]
    xe_ref[0:HALO] = jnp.where(i > 0, xp_ref[0], 0.0)
    xe_ref[HALO + tm:] = jnp.where(i < nt - 1, xn_ref[0], 0.0)
    be_ref[HALO:HALO + tm] = bc_ref[0]
    be_ref[0:HALO] = jnp.where(i > 0, bcp_ref[0], 0.0)
    be_ref[HALO + tm:] = jnp.where(i < nt - 1, bcn_ref[0], 0.0)
    pad = (SSM_CONV - 1) // 2

    def put_x(r0, nr, c0, acc):
        xo_ref[0, r0:r0 + nr, c0:c0 + LANES] = _silu(acc + cbx_ref[:, c0:c0 + LANES])

    def put_b(r0, nr, c0, acc):
        bv_ref[r0:r0 + nr, c0:c0 + LANES] = _silu(acc + cbb_ref[:, c0:c0 + LANES])

    _conv_rows(xe_ref, cwx_ref, SSM_CONV, HALO - pad, tm, SSM_WIDTH, put_x)
    _conv_rows(be_ref, cwb_ref, SSM_CONV, HALO - pad, tm, SSM_BC, put_b)
    bv = bv_ref[...]
    bco_ref[0] = bv.astype(BF16)
    bmt_ref[0] = bv[:, :SSM_GROUPS * SSM_STATE].T.astype(BF16)
    x = dt_ref[0] + dtb_ref[...]
    dto_ref[0] = jnp.maximum(x, 0.0) + jnp.log(1.0 + jnp.exp(-jnp.abs(x)))


def _ssdprep(proj, cwx, cbx, cwb, cbb, dtb, tm):
    b, t, _ = proj.shape
    gs = SSM_GROUPS * SSM_STATE
    return pl.pallas_call(
        functools.partial(_ssdprep_kernel, tm=tm),
        grid=(b, t // tm),
        in_specs=_halo_specs(tm, t, SSM_WIDTH, COL_XS // SSM_WIDTH)
        + _halo_specs(tm, t, SSM_BC, COL_BC // SSM_BC)
        + [
            pl.BlockSpec((1, tm, LANES), lambda bb, i: (bb, i, COL_DT // LANES)),
            pl.BlockSpec((SSM_CONV, SSM_WIDTH), lambda bb, i: (0, 0)),
            pl.BlockSpec((1, SSM_WIDTH), lambda bb, i: (0, 0)),
            pl.BlockSpec((SSM_CONV, SSM_BC), lambda bb, i: (0, 0)),
            pl.BlockSpec((1, SSM_BC), lambda bb, i: (0, 0)),
            pl.BlockSpec((1, LANES), lambda bb, i: (0, 0)),
        ],
        out_specs=[
            pl.BlockSpec((1, tm, SSM_WIDTH), lambda bb, i: (bb, i, 0)),
            pl.BlockSpec((1, tm, SSM_BC), lambda bb, i: (bb, i, 0)),
            pl.BlockSpec((1, gs, tm), lambda bb, i: (bb, 0, i)),
            pl.BlockSpec((1, tm, LANES), lambda bb, i: (bb, i, 0)),
        ],
        out_shape=[
            jax.ShapeDtypeStruct((b, t, SSM_WIDTH), F32),
            jax.ShapeDtypeStruct((b, t, SSM_BC), BF16),
            jax.ShapeDtypeStruct((b, gs, t), BF16),
            jax.ShapeDtypeStruct((b, t, LANES), F32),
        ],
        scratch_shapes=[
            pltpu.VMEM((tm + 2 * HALO, SSM_WIDTH), F32),
            pltpu.VMEM((tm + 2 * HALO, SSM_BC), F32),
            pltpu.VMEM((tm, SSM_BC), F32),
        ],
        compiler_params=_cparams(("parallel", "parallel")),
        name="ssd_prep",
    )(proj, proj, proj, proj, proj, proj, proj, cwx, cbx, cwb, cbb, dtb)


def _split_hi_lo(v):
    hi = v.astype(BF16)
    lo = (v - hi.astype(F32)).astype(BF16)
    return jnp.concatenate([hi, lo], axis=1)


def _ssd_kernel(*refs, direction, with_y, finish):
    it = iter(refs)
    xs_ref, bc_ref, bmt_ref, dt_ref, a_ref, e_ref, h0_ref = (next(it) for _ in range(7))
    if finish:
        yf_ref, z_ref, dsum_ref, nw_ref = (next(it) for _ in range(4))
    if with_y:
        y_ref = next(it)
    hfin_ref = next(it)
    st_ref = next(it)
    q = SSM_CHUNK
    c = pl.program_id(1)

    @pl.when(c == 0)
    def _():
        st_ref[...] = h0_ref[0]

    ii = lax.broadcasted_iota(jnp.int32, (q, q), 0)
    jj = lax.broadcasted_iota(jnp.int32, (q, q), 1)
    if direction == 0:
        valid, valid_t, end_row = jj <= ii, ii <= jj, q - 1
    else:
        valid, valid_t, end_row = jj >= ii, ii >= jj, 0
    dt = dt_ref[0]
    da = dt * a_ref[...]
    tri = jnp.where(valid, 1.0, 0.0)
    tri_t = jnp.where(valid_t, 1.0, 0.0)
    acum_c = jnp.dot(tri, da, precision=lax.Precision.HIGHEST, preferred_element_type=F32)
    acum_r = jnp.dot(da.T, tri_t, precision=lax.Precision.HIGHEST, preferred_element_type=F32)
    tot = acum_c[end_row:end_row + 1, :]
    to_end = jnp.exp(tot - acum_c)
    ea = jnp.exp(acum_c)

    e = e_ref[...]

    def expand(v):
        return jnp.dot(_split_hi_lo(v), e, preferred_element_type=F32)

    ea_e = expand(ea)
    xs = xs_ref[0]
    xw = (xs * expand(dt * to_end)).astype(BF16)
    cdec_e = ea_e[end_row:end_row + 1, :]
    bc = bc_ref[0]
    bmt = bmt_ref[0]
    if with_y:
        xdt = (xs * expand(dt)).astype(BF16)
        lane = lax.broadcasted_iota(jnp.int32, xdt.shape, 1)
        left = (lane % LANES) < SSM_HEAD_DIM
        zero = jnp.zeros_like(xdt)
        xdt_l = jnp.where(left, xdt, zero)
        xdt_r = jnp.where(left, zero, xdt)

    gw = SSM_GROUP_WIDTH
    hpg = SSM_HEADS // SSM_GROUPS
    for g in range(SSM_GROUPS):
        bm_t = bmt[g * SSM_STATE:(g + 1) * SSM_STATE, :]
        st = st_ref[g]
        if with_y:
            cm = bc[:, (SSM_GROUPS + g) * SSM_STATE:(SSM_GROUPS + g + 1) * SSM_STATE]
            bm = bc[:, g * SSM_STATE:(g + 1) * SSM_STATE]
            cb = lax.dot_general(cm, bm, (((1,), (1,)), ((), ())), preferred_element_type=F32)
            y_inter = jnp.dot(cm, st.astype(BF16), preferred_element_type=F32) * ea_e[:, g * gw:(g + 1) * gw]
            parts = []
            for k in range(hpg // 2):
                ms = []
                for r in (2 * k, 2 * k + 1):
                    hl = SSM_HEADS * direction + g * hpg + r
                    seg = acum_c[:, hl:hl + 1] - acum_r[hl:hl + 1, :]
                    ms.append((cb * jnp.where(valid, jnp.exp(seg), 0.0)).astype(BF16))
                c0 = g * gw + k * LANES
                rhs = jnp.concatenate([xdt_l[:, c0:c0 + LANES], xdt_r[:, c0:c0 + LANES]], axis=0)
                parts.append(jnp.dot(jnp.concatenate(ms, axis=1), rhs, preferred_element_type=F32))
            y_g = y_inter + jnp.concatenate(parts, axis=1)
            if finish:
                sl = slice(g * gw, (g + 1) * gw)
                y_ref[:, sl] = yf_ref[0, :, sl] + y_g + dsum_ref[:, sl] * xs[:, sl]
            else:
                y_ref[0, :, g * gw:(g + 1) * gw] = y_g
        st_ref[g] = st * cdec_e[:, g * gw:(g + 1) * gw] + jnp.dot(
            bm_t, xw[:, g * gw:(g + 1) * gw], preferred_element_type=F32)

    if finish:
        out_ref = next(it)
        gz = y_ref[...] * _silu(z_ref[0])
        var = jnp.mean(gz * gz, axis=-1, keepdims=True)
        out_ref[0] = (gz * lax.rsqrt(var + EPS) * nw_ref[...]).astype(BF16)

    @pl.when(c == pl.num_programs(1) - 1)
    def _():
        hfin_ref[0] = st_ref[...]


def _ssd_scan(xs, bc, bmt, dt, a_row, e_mat, h0, direction, with_y, fin=None):
    b, t, _ = xs.shape
    nc = t // SSM_CHUNK
    q = SSM_CHUNK
    gs = SSM_GROUPS * SSM_STATE
    finish = fin is not None
    if direction == 0:
        tmap = lambda bb, c: (bb, c, 0)
        tmap_t = lambda bb, c: (bb, 0, c)
        zmap = lambda bb, c: (bb, c, COL_Z // SSM_WIDTH)
    else:
        tmap = lambda bb, c: (bb, nc - 1 - c, 0)
        tmap_t = lambda bb, c: (bb, 0, nc - 1 - c)
        zmap = lambda bb, c: (bb, nc - 1 - c, COL_Z // SSM_WIDTH)
    st_spec = pl.BlockSpec((1, SSM_GROUPS, SSM_STATE, SSM_GROUP_WIDTH), lambda bb, c: (bb, 0, 0, 0))
    in_specs = [
        pl.BlockSpec((1, q, SSM_WIDTH), tmap),
        pl.BlockSpec((1, q, SSM_BC), tmap),
        pl.BlockSpec((1, gs, q), tmap_t),
        pl.BlockSpec((1, q, LANES), tmap),
        pl.BlockSpec((1, LANES), lambda bb, c: (0, 0)),
        pl.BlockSpec((2 * LANES, SSM_WIDTH), lambda bb, c: (0, 0)),
        st_spec,
    ]
    args = [xs, bc, bmt, dt, a_row, e_mat, h0]
    out_specs, out_shape, scratch = [], [], []
    if finish:
        y_fwd, proj, dsum_e, nw = fin
        in_specs += [
            pl.BlockSpec((1, q, SSM_WIDTH), tmap),
            pl.BlockSpec((1, q, SSM_WIDTH), zmap),
            pl.BlockSpec((1, SSM_WIDTH), lambda bb, c: (0, 0)),
            pl.BlockSpec((1, SSM_WIDTH), lambda bb, c: (0, 0)),
        ]
        args += [y_fwd, proj, dsum_e, nw]
    elif with_y:
        out_specs.append(pl.BlockSpec((1, q, SSM_WIDTH), tmap))
        out_shape.append(jax.ShapeDtypeStruct((b, t, SSM_WIDTH), F32))
    out_specs.append(st_spec)
    out_shape.append(jax.ShapeDtypeStruct((b, SSM_GROUPS, SSM_STATE, SSM_GROUP_WIDTH), F32))
    scratch.append(pltpu.VMEM((SSM_GROUPS, SSM_STATE, SSM_GROUP_WIDTH), F32))
    if finish:
        out_specs.append(pl.BlockSpec((1, q, SSM_WIDTH), tmap))
        out_shape.append(jax.ShapeDtypeStruct((b, t, SSM_WIDTH), BF16))
    return pl.pallas_call(
        functools.partial(_ssd_kernel_ordered, direction=direction, with_y=with_y, finish=finish),
        grid=(b, nc),
        in_specs=in_specs,
        out_specs=out_specs,
        out_shape=out_shape,
        scratch_shapes=scratch + ([pltpu.VMEM((q, SSM_WIDTH), F32)] if finish else []),
        compiler_params=_cparams(("parallel", "arbitrary")),
        name="ssd_scan",
    )(*args)


def _ssd_kernel_ordered(*refs, direction, with_y, finish):
    n_in = 11 if finish else 7
    ins = list(refs[:n_in])
    rest = list(refs[n_in:])
    if finish:
        hfin, out, st, ybuf = rest
        ordered = ins + [ybuf, hfin, st, out]
    elif with_y:
        y, hfin, st = rest
        ordered = ins + [y, hfin, st]
    else:
        hfin, st = rest
        ordered = ins + [hfin, st]
    _ssd_kernel(*ordered, direction=direction, with_y=with_y, finish=finish)


def _merge_kernel(x_ref, nw_ref, sh_ref, sc_ref, gt_ref, a_ref, b_ref, c_ref, wg0_ref, wg1_ref, wg2_ref,
                  wa_ref, wb_ref, wc_ref, wo_ref, o_ref, h_ref, acc_ref):
    j = pl.program_id(2)

    @pl.when(j == 0)
    def _():
        h_ref[...] = _rms_mod(x_ref[0], nw_ref[...], sh_ref[0], sc_ref[0]).astype(BF16)
        acc_ref[...] = jnp.zeros_like(acc_ref)

    h = h_ref[...]
    s = None
    for act_ref, wp_ref, wg_ref in ((a_ref, wa_ref, wg0_ref), (b_ref, wb_ref, wg1_ref), (c_ref, wc_ref, wg2_ref)):
        y = jnp.dot(act_ref[0], wp_ref[...], preferred_element_type=F32)
        gate = _sigmoid(jnp.dot(h, wg_ref[...], preferred_element_type=F32))
        s = gate * y if s is None else s + gate * y
    acc_ref[...] += jnp.dot(s.astype(BF16), wo_ref[...], preferred_element_type=F32)

    @pl.when(j == pl.num_programs(2) - 1)
    def _():
        o_ref[0] = x_ref[0] + gt_ref[0] * acc_ref[...]


def _merge(x, nw, shift, scale, gate, act_a, act_b, act_c, wg, wa, wb, wc, wo, tm):
    b, t, d = x.shape
    tn = 256
    nj = d // tn
    vec = pl.BlockSpec((1, 1, d), lambda bb, i, j: (bb, 0, 0))
    act =pl.BlockSpec((1, tm, ATTN_WIDTH), lambda bb, i, j: (bb, i, 0))
    wproj = pl.BlockSpec((ATTN_WIDTH, tn), lambda bb, i, j: (0, j))
    return pl.pallas_call(
        _merge_kernel,
        grid=(b, t // tm, nj),
        in_specs=[
            pl.BlockSpec((1, tm, d), lambda bb, i, j: (bb, i, 0)),
            pl.BlockSpec((1, d), lambda bb, i, j: (0, 0)),
            vec, vec, vec, act, act, act,
            pl.BlockSpec((d, tn), lambda bb, i, j: (0, j)),
            pl.BlockSpec((d, tn), lambda bb, i, j: (0, nj + j)),
            pl.BlockSpec((d, tn), lambda bb, i, j: (0, 2 * nj + j)),
            wproj, wproj, wproj,
            pl.BlockSpec((tn, d), lambda bb, i, j: (j, 0)),
        ],
        out_specs=pl.BlockSpec((1, tm, d), lambda bb, i, j: (bb, i, 0)),
        out_shape=jax.ShapeDtypeStruct((b, t, d), F32),
        scratch_shapes=[pltpu.VMEM((tm, d), BF16), pltpu.VMEM((tm, d), F32)],
        compiler_params=_cparams(("parallel", "parallel", "arbitrary")),
        name="branch_merge",
    )(x, nw, shift, scale, gate, act_a, act_b, act_c, wg, wg, wg, wa, wb, wc, wo)


def _ffn_kernel(x_ref, xp_ref, xn_ref, nw_ref, sh_ref, sc_ref, gt_ref, wug_ref, wuv_ref, cwg_ref, cwv_ref,
                cbg_ref, cbv_ref, wd_ref, o_ref, h_ref, acc_ref, *, tm):
    i = pl.program_id(1)
    j = pl.program_id(2)

    @pl.when(j == 0)
    def _():
        nw, sh, sc = nw_ref[...], sh_ref[0], sc_ref[0]
        h_ref[HALO:HALO + tm] = _rms_mod(x_ref[0], nw, sh, sc).astype(BF16)
        hp = _rms_mod(xp_ref[0], nw, sh, sc)
        h_ref[0:HALO] = jnp.where(i > 0, hp, 0.0).astype(BF16)
        hn = _rms_mod(xn_ref[0], nw, sh, sc)
        h_ref[HALO + tm:] = jnp.where(i < pl.num_programs(1) - 1, hn, 0.0).astype(BF16)
        acc_ref[...] = jnp.zeros_like(acc_ref)

    h = h_ref[...]
    rows = tm + 2 * HALO

    def conv(p, cw_ref, cb_ref):
        prev = pltpu.roll(p, 1, 0)[HALO:HALO + tm]
        nxt = pltpu.roll(p, rows - 1, 0)[HALO:HALO + tm]
        return prev * cw_ref[0:1, :] + p[HALO:HALO + tm] * cw_ref[1:2, :] + nxt * cw_ref[2:3, :] + cb_ref[...]

    ug = conv(jnp.dot(h, wug_ref[...], preferred_element_type=F32), cwg_ref, cbg_ref)
    uv = conv(jnp.dot(h, wuv_ref[...], preferred_element_type=F32), cwv_ref, cbv_ref)
    s = (_silu(ug) * uv).astype(BF16)
    acc_ref[...] += jnp.dot(s, wd_ref[...], preferred_element_type=F32)

    @pl.when(j == pl.num_programs(2) - 1)
    def _():
        o_ref[0] = x_ref[0] + gt_ref[0] * acc_ref[...]


def _ffn(x, nw, shift, scale, gate, w_up, conv_w, conv_b, w_down, tm):
    b, t, d = x.shape
    dff = w_down.shape[0]
    tf = 512
    nj = dff // tf
    per = tm // HALO
    last = t // HALO - 1
    vec = pl.BlockSpec((1, 1, d), lambda bb, i, j: (bb, 0, 0))
    return pl.pallas_call(
        functools.partial(_ffn_kernel, tm=tm),
        grid=(b, t // tm, nj),
        in_specs=[
            pl.BlockSpec((1, tm, d), lambda bb, i, j: (bb, i, 0)),
            pl.BlockSpec((1, HALO, d), lambda bb, i, j: (bb, jnp.maximum(i * per - 1, 0), 0)),
            pl.BlockSpec((1, HALO, d), lambda bb, i, j: (bb, jnp.minimum((i + 1) * per, last), 0)),
            pl.BlockSpec((1, d), lambda bb, i, j: (0, 0)),
            vec, vec, vec,
            pl.BlockSpec((d, tf), lambda bb, i, j: (0, j)),
            pl.BlockSpec((d, tf), lambda bb, i, j: (0, nj + j)),
            pl.BlockSpec((FFN_CONV, tf), lambda bb, i, j: (0, j)),
            pl.BlockSpec((FFN_CONV, tf), lambda bb, i, j: (0, nj + j)),
            pl.BlockSpec((1, tf), lambda bb, i, j: (0, j)),
            pl.BlockSpec((1, tf), lambda bb, i, j: (0, nj + j)),
            pl.BlockSpec((tf, d), lambda bb, i, j: (j, 0)),
        ],
        out_specs=pl.BlockSpec((1, tm, d), lambda bb, i, j: (bb, i, 0)),
        out_shape=jax.ShapeDtypeStruct((b, t, d), F32),
        scratch_shapes=[pltpu.VMEM((tm + 2 * HALO, d), BF16), pltpu.VMEM((tm, d), F32)],
        compiler_params=_cparams(("parallel", "parallel", "arbitrary")),
        name="conv_ffn",
    )(x, x, x, nw, shift, scale, gate, w_up, w_up, conv_w, conv_w, conv_b, conv_b, w_down)


def _final_norm_kernel(x_ref, w_ref, o_ref):
    x = x_ref[0]
    var = jnp.mean(x * x, axis=-1, keepdims=True)
    o_ref[0] = x * lax.rsqrt(var + EPS) * w_ref[...]


def _final_norm(x, w, tm):
    b, t, d = x.shape
    return pl.pallas_call(
        _final_norm_kernel,
        grid=(b, t // tm),
        in_specs=[pl.BlockSpec((1, tm, d), lambda bb, i: (bb, i, 0)), pl.BlockSpec((1, d), lambda bb, i: (0, 0))],
        out_specs=pl.BlockSpec((1, tm, d), lambda bb, i: (bb, i, 0)),
        out_shape=jax.ShapeDtypeStruct((b, t, d), F32),
        compiler_params=_cparams(("parallel", "parallel")),
        name="final_norm",
    )(x, w)


def _rope_tables(n_tokens):
    rows = n_tokens // GRID_W
    row = jnp.repeat(jnp.arange(rows), GRID_W).astype(F32)
    col = jnp.tile(jnp.arange(GRID_W), rows).astype(F32)
    inv = ROPE_THETA ** (-jnp.arange(0, ROPE_AXIS_DIM, 2, dtype=F32) / ROPE_AXIS_DIM)
    ar, ac = row[:, None] * inv, col[:, None] * inv
    cos = jnp.concatenate([jnp.cos(ar), jnp.cos(ar), jnp.cos(ac), jnp.cos(ac)], axis=1)
    sin = jnp.concatenate([-jnp.sin(ar), jnp.sin(ar), -jnp.sin(ac), jnp.sin(ac)], axis=1)
    return cos, sin


def _expansion_matrix(direction):
    head_of_col = jnp.arange(SSM_WIDTH) // SSM_HEAD_DIM
    row = jnp.arange(LANES)[:, None]
    e = (row == head_of_col[None, :] + SSM_HEADS * direction).astype(BF16)
    return jnp.concatenate([e, e], axis=0)


def _pad_lanes(v, offset):
    return jnp.zeros((1, LANES), F32).at[0, offset:offset + v.shape[0]].set(v.astype(F32))


def kernel(x, c, ctx, c_ctx, w_mod, b_mod, norm_mix_w, norm_ffn_w, w_in, q_norm_w, k_norm_w, w_attn_o, conf_conv_w, conf_conv_b, conf_ln_w, conf_ln_b, w_conf_o, ssm_conv_w, ssm_conv_b, ssm_a_log, ssm_dt_bias, ssm_d, ssm_norm_w, w_ssm_o, w_out, ffn_w_up, ffn_conv_w, ffn_conv_b, ffn_w_down, final_norm_w):
    bsz, n_lat, d = x.shape
    n_ctx = ctx.shape[1]
    depth = w_mod.shape[0]
    assert bsz <= 7 and n_lat % 512 == 0 and n_ctx % 256 == 0

    q_end = ATTN_WIDTH
    v_end = q_end + 2 * KV_WIDTH
    conf_end = v_end + 2 * CONF_WIDTH
    z_end = conf_end + SSM_WIDTH
    xbc_end = z_end + SSM_WIDTH + SSM_BC
    dt_end = xbc_end + 2 * SSM_HEADS

    cc = jnp.zeros((8, d), F32).at[:bsz].set(c).at[bsz].set(c_ctx)
    mods = _mod_vectors(cc, w_mod, b_mod).reshape(depth, 8, N_MOD, d)
    rope = _rope_tables(n_lat)
    e_mats = [_expansion_matrix(0), _expansion_matrix(1)]
    zero_state = jnp.zeros((bsz, SSM_GROUPS, SSM_STATE, SSM_GROUP_WIDTH), F32)

    for l in range(depth):
        need_ctx = l < depth - 1
        wi = w_in[l]
        w_mix = jnp.concatenate([
            wi[:, :q_end], wi[:, v_end:conf_end], wi[:, conf_end:z_end], wi[:, z_end:z_end + SSM_WIDTH],
            wi[:, q_end:v_end], wi[:, z_end + SSM_WIDTH:xbc_end], wi[:, xbc_end:dt_end],
            jnp.zeros((d, MIX_WIDTH - dt_end), F32)], axis=1).astype(BF16)
        w_gate = wi[:, dt_end:].astype(BF16)
        wa, wb, wc, wo = (w[l].astype(BF16) for w in (w_attn_o, w_conf_o, w_ssm_o, w_out))
        w_up, w_down = ffn_w_up[l].astype(BF16), ffn_w_down[l].astype(BF16)
        row = lambda v: v.reshape(1, -1)
        nmix, nffn = row(norm_mix_w[l]), row(norm_ffn_w[l])
        cwx, cwb = ssm_conv_w[l][:, :SSM_WIDTH], ssm_conv_w[l][:, SSM_WIDTH:]
        cbx, cbb = row(ssm_conv_b[l][:SSM_WIDTH]), row(ssm_conv_b[l][SSM_WIDTH:])
        a_coef = -jnp.exp(ssm_a_log[l].astype(F32))
        a_rows = [_pad_lanes(a_coef[0], 0), _pad_lanes(a_coef[1], SSM_HEADS)]
        dtb = _pad_lanes(ssm_dt_bias[l].reshape(-1), 0)
        dsum_e = row(jnp.repeat(ssm_d[l][0] + ssm_d[l][1], SSM_HEAD_DIM))
        m_lat = [mods[l, :bsz, k].reshape(bsz, 1, d) for k in range(N_MOD)]
        m_ctx = [jnp.broadcast_to(mods[l, bsz, k].reshape(1, 1, d), (bsz, 1, d)) for k in range(N_MOD)]

        proj_l = _inproj(x, nmix, m_lat[0], m_lat[1], w_mix, 1024 if n_lat % 1024 == 0 else 512)
        proj_c = _inproj(ctx, nmix, m_ctx[0], m_ctx[1], w_mix, 256)

        q_l, k_l, v_l = _qkprep(proj_l, row(q_norm_w[l]), row(k_norm_w[l]), rope, 256)
        q_c, k_c, v_c = _qkprep(proj_c, row(q_norm_w[l]), row(k_norm_w[l]), None, 256)
        act_a_l = _attention(q_l, jnp.concatenate([k_c, k_l], axis=1), jnp.concatenate([v_c, v_l], axis=1), 128, 256)
        conf_args = (conf_conv_w[l], row(conf_conv_b[l]), row(conf_ln_w[l]), row(conf_ln_b[l]))
        act_b_l = _conformer(proj_l, *conf_args, 256)

        prep_l = _ssdprep(proj_l, cwx, cbx, cwb, cbb, dtb, 256)
        prep_c = _ssdprep(proj_c, cwx, cbx, cwb, cbb, dtb, 256)
        fin_args = (dsum_e, row(ssm_norm_w[l]))
        if need_ctx:
            y_cf, h_cf = _ssd_scan(*prep_c, a_rows[0], e_mats[0], zero_state, 0, True)
            h_cb, act_c_c = _ssd_scan(*prep_c, a_rows[1], e_mats[1], zero_state, 1, True, (y_cf, proj_c) + fin_args)
        else:
            (h_cf,) = _ssd_scan(*prep_c, a_rows[0], e_mats[0], zero_state, 0, False)
            (h_cb,) = _ssd_scan(*prep_c, a_rows[1], e_mats[1], zero_state, 1, False)
        y_lf, _ = _ssd_scan(*prep_l, a_rows[0], e_mats[0], h_cf, 0, True)
        _, act_c_l = _ssd_scan(*prep_l, a_rows[1], e_mats[1], h_cb, 1, True, (y_lf, proj_l) + fin_args)

        x = _merge(x, nmix, m_lat[0], m_lat[1], m_lat[2], act_a_l, act_b_l, act_c_l, w_gate, wa, wb, wc, wo, 512)
        ffn_args = (w_up, ffn_conv_w[l], row(ffn_conv_b[l]), w_down)
        x = _ffn(x, nffn, m_lat[3], m_lat[4], m_lat[5], *ffn_args, 512)
        if need_ctx:
            act_a_c = _attention(q_c, k_c, v_c, 128, 256)
            act_b_c = _conformer(proj_c, *conf_args, 256)
            ctx = _merge(ctx, nmix, m_ctx[0], m_ctx[1], m_ctx[2], act_a_c, act_b_c, act_c_c, w_gate, wa, wb, wc, wo, 256)
            ctx = _ffn(ctx, nffn, m_ctx[3], m_ctx[4], m_ctx[5], *ffn_args, 256)
    return _final_norm(x, final_norm_w.reshape(1, d), 512)
```

```python
import functools
import math

import jax
import jax.numpy as jnp
from jax import lax
from jax.experimental import pallas as pl
from jax.experimental.pallas import tpu as pltpu

F32 = jnp.float32
BF16 = jnp.bfloat16

EPS = 1e-6
N_MOD = 6
GRID_W = 64
HEAD_DIM = 128
ATTN_HEADS = 8
ATTN_KV_HEADS = 2
ATTN_REP = ATTN_HEADS // ATTN_KV_HEADS
ATTN_WIDTH = ATTN_HEADS * HEAD_DIM
KV_WIDTH = ATTN_KV_HEADS * HEAD_DIM
ROPE_THETA = 10000.0
ROPE_AXIS_DIM = HEAD_DIM // 2
ROPE_FREQS = ROPE_AXIS_DIM // 2
CONF_WIDTH = 1024
CONF_KERNEL = 31
SSM_HEADS = 16
SSM_HEAD_DIM = 64
SSM_WIDTH = SSM_HEADS * SSM_HEAD_DIM
SSM_GROUPS = 2
SSM_STATE = 128
SSM_GROUP_WIDTH = SSM_WIDTH // SSM_GROUPS
SSM_BC = 2 * SSM_GROUPS * SSM_STATE
SSM_CONV = 5
SSM_CHUNK = 128
FFN_CONV = 3
N_BRANCHES = 3

LANES = 128
HALO = 16
VMEM_LIMIT = 56 * 1024 * 1024
LOG2E = 1.4426950408889634
NEG_BIG = -1e30

COL_Q = 0
COL_A = 1024
COL_G = 2048
COL_Z = 3072
COL_XS = 4096
COL_KV = 5120
COL_BC = 5632
COL_DT = 6144
MIX_WIDTH = 6272


def _cparams(sem):
    return pltpu.CompilerParams(dimension_semantics=sem, vmem_limit_bytes=VMEM_LIMIT)


def _sigmoid(x):
    return 1.0 / (1.0 + jnp.exp(-x))


def _silu(x):
    return x * _sigmoid(x)


def _rms_mod(x, nw, shift, scale):
    var = jnp.mean(x * x, axis=-1, keepdims=True)
    y = x * lax.rsqrt(var + EPS) * nw
    return y * (1.0 + scale) + shift


def _mod_kernel(c_ref, w_ref, b_ref, o_ref):
    s = _silu(c_ref[...]).astype(BF16)
    o_ref[0] = jnp.dot(s, w_ref[0].astype(BF16), preferred_element_type=F32) + b_ref[0]


def _mod_vectors(cc, w_mod, b_mod):
    depth, d, nm = w_mod.shape
    tn = 1024
    return pl.pallas_call(
        _mod_kernel,
        grid=(depth, nm // tn),
        in_specs=[
            pl.BlockSpec((8, d), lambda l, j: (0, 0)),
            pl.BlockSpec((1, d, tn), lambda l, j: (l, 0, j)),
            pl.BlockSpec((1, 1, tn), lambda l, j: (l, 0, j)),
        ],
        out_specs=pl.BlockSpec((1, 8, tn), lambda l, j: (l, 0, j)),
        out_shape=jax.ShapeDtypeStruct((depth, 8, nm), F32),
        compiler_params=_cparams(("parallel", "parallel")),
        name="mod_vectors",
    )(cc, w_mod, b_mod.reshape(depth, 1, nm))


def _inproj_kernel(x_ref, nw_ref, sh_ref, sc_ref, w_ref, o_ref, h_ref):
    @pl.when(pl.program_id(2) == 0)
    def _():
        h_ref[...] = _rms_mod(x_ref[0], nw_ref[...], sh_ref[0], sc_ref[0]).astype(BF16)

    o_ref[0] = jnp.dot(h_ref[...], w_ref[...], preferred_element_type=F32)


def _inproj(x, nw, shift, scale, w, tm):
    b, t, d = x.shape
    n = w.shape[1]
    tn = 896
    return pl.pallas_call(
        _inproj_kernel,
        grid=(b, t // tm, n // tn),
        in_specs=[
            pl.BlockSpec((1, tm, d), lambda bb, i, j: (bb, i, 0)),
            pl.BlockSpec((1, d), lambda bb, i, j: (0, 0)),
            pl.BlockSpec((1, 1, d), lambda bb, i, j: (bb, 0, 0)),
            pl.BlockSpec((1, 1, d), lambda bb, i, j: (bb, 0, 0)),
            pl.BlockSpec((d, tn), lambda bb, i, j: (0, j)),
        ],
        out_specs=pl.BlockSpec((1, tm, tn), lambda bb, i, j: (bb, i, j)),
        out_shape=jax.ShapeDtypeStruct((b, t, n), F32),
        scratch_shapes=[pltpu.VMEM((tm, d), BF16)],
        compiler_params=_cparams(("parallel", "parallel", "arbitrary")),
        name="mixer_inproj",
    )(x, nw, shift, scale, w)


def _head_norm(x, w):
    var = jnp.mean(x * x, axis=-1, keepdims=True)
    return x * lax.rsqrt(var + EPS) * w


def _rope(x, c, s):
    lane = lax.broadcasted_iota(jnp.int32, x.shape, 1)
    first = (lane % ROPE_AXIS_DIM) < ROPE_FREQS
    partner = jnp.where(first, pltpu.roll(x, HEAD_DIM - ROPE_FREQS, 1), pltpu.roll(x, ROPE_FREQS, 1))
    return x * c + partner * s


def _qkprep_kernel(*refs, use_rope):
    if use_rope:
        q_ref, kv_ref, qw_ref, kw_ref, c_ref, s_ref, qo_ref, ko_ref, vo_ref = refs
    else:
        q_ref, kv_ref, qw_ref, kw_ref, qo_ref, ko_ref, vo_ref = refs
    q = q_ref[0]
    kv = kv_ref[0]
    for h in range(ATTN_HEADS):
        y = _head_norm(q[:, h * HEAD_DIM:(h + 1) * HEAD_DIM], qw_ref[...])
        if use_rope:
            y = _rope(y, c_ref[...], s_ref[...])
        qo_ref[0, :, h * HEAD_DIM:(h + 1) * HEAD_DIM] = y.astype(BF16)
    for h in range(ATTN_KV_HEADS):
        y = _head_norm(kv[:, h * HEAD_DIM:(h + 1) * HEAD_DIM], kw_ref[...])
        if use_rope:
            y = _rope(y, c_ref[...], s_ref[...])
        ko_ref[0, :, h * HEAD_DIM:(h + 1) * HEAD_DIM] = y.astype(BF16)
    vo_ref[0] = kv[:, KV_WIDTH:].T.astype(BF16)


def _qkprep(proj, qw, kw, rope, tm):
    b, t, _ = proj.shape
    use_rope = rope is not None
    in_specs = [
        pl.BlockSpec((1, tm, ATTN_WIDTH), lambda bb, i: (bb, i, COL_Q // ATTN_WIDTH)),
        pl.BlockSpec((1, tm, 2 * KV_WIDTH), lambda bb, i: (bb, i, COL_KV // (2 * KV_WIDTH))),
        pl.BlockSpec((1, HEAD_DIM), lambda bb, i: (0, 0)),
        pl.BlockSpec((1, HEAD_DIM), lambda bb, i: (0, 0)),
    ]
    args = [proj, proj, qw, kw]
    if use_rope:
        in_specs += [pl.BlockSpec((tm, HEAD_DIM), lambda bb, i: (i, 0))] * 2
        args += list(rope)
    return pl.pallas_call(
        functools.partial(_qkprep_kernel, use_rope=use_rope),
        grid=(b, t // tm),
        in_specs=in_specs,
        out_specs=[
            pl.BlockSpec((1, tm, ATTN_WIDTH), lambda bb, i: (bb, i, 0)),
            pl.BlockSpec((1, tm, KV_WIDTH), lambda bb, i: (bb, i, 0)),
            pl.BlockSpec((1, KV_WIDTH, tm), lambda bb, i: (bb, 0, i)),
        ],
        out_shape=[
            jax.ShapeDtypeStruct((b, t, ATTN_WIDTH), BF16),
            jax.ShapeDtypeStruct((b, t, KV_WIDTH), BF16),
            jax.ShapeDtypeStruct((b, KV_WIDTH, t), BF16),
        ],
        compiler_params=_cparams(("parallel", "parallel")),
        name="qk_prep",
    )(*args)


def _attn_kernel(q_ref, k_ref, vt_ref, o_ref, s0_ref, s1_ref, acc_ref, *, tq, tk, nk):
    q = q_ref[0]
    qs = jnp.concatenate([q[:, r * HEAD_DIM:(r + 1) * HEAD_DIM] for r in range(ATTN_REP)], axis=0)
    nq = ATTN_REP * tq
    scale = HEAD_DIM ** -0.5 * LOG2E

    def scores(j):
        kj = k_ref[0, pl.ds(pl.multiple_of(j * tk, tk), tk), :]
        return lax.dot_general(kj, qs, (((1,), (1,)), ((), ())), preferred_element_type=F32)

    def update(j, s_ref, m, l):
        s = s_ref[...]
        m_new = jnp.maximum(m, jnp.max(s, axis=0, keepdims=True))
        alpha = jnp.exp2((m - m_new) * scale)
        p = jnp.exp2((s - m_new) * scale)
        l = alpha * l + jnp.sum(p, axis=0, keepdims=True)
        vt = vt_ref[0, :, pl.ds(pl.multiple_of(j * tk, tk), tk)]
        acc_ref[...] = alpha * acc_ref[...] + jnp.dot(vt, p.astype(BF16), preferred_element_type=F32)
        return m_new, l

    s0_ref[...] = scores(0)
    acc_ref[...] = jnp.zeros_like(acc_ref)

    def body(i, carry):
        m, l = carry
        s1_ref[...] = scores(2 * i + 1)
        m, l = update(2 * i, s0_ref, m, l)
        s0_ref[...] = scores(2 * i + 2)
        return update(2 * i + 1, s1_ref, m, l)

    init = (jnp.full((1, nq), NEG_BIG, F32), jnp.zeros((1, nq), F32))
    m, l = lax.fori_loop(0, (nk - 1) // 2, body, init)
    _, l = update(nk - 1, s0_ref, m, l)
    o = acc_ref[...] / l
    for r in range(ATTN_REP):
        o_ref[0, :, r * HEAD_DIM:(r + 1) * HEAD_DIM] = o[:, r * tq:(r + 1) * tq].T.astype(BF16)


def _attention(q, k, vt, tq, tk):
    b, t, _ = q.shape
    tkv = k.shape[1]
    nk = tkv // tk
    assert nk % 2 == 1
    gw = ATTN_REP * HEAD_DIM
    nq = ATTN_REP * tq
    return pl.pallas_call(
        functools.partial(_attn_kernel, tq=tq, tk=tk, nk=nk),
        grid=(b, ATTN_KV_HEADS, t // tq),
        in_specs=[
            pl.BlockSpec((1, tq, gw), lambda bb, g, i: (bb, i, g)),
            pl.BlockSpec((1, tkv, HEAD_DIM), lambda bb, g, i: (bb, 0, g)),
            pl.BlockSpec((1, HEAD_DIM, tkv), lambda bb, g, i: (bb, g, 0)),
        ],
        out_specs=pl.BlockSpec((1, tq, gw), lambda bb, g, i: (bb, i, g)),
        out_shape=jax.ShapeDtypeStruct((b, t, ATTN_WIDTH), BF16),
        scratch_shapes=[pltpu.VMEM((tk, nq), F32), pltpu.VMEM((tk, nq), F32), pltpu.VMEM((HEAD_DIM, nq), F32)],
        compiler_params=_cparams(("parallel", "parallel", "arbitrary")),
        name="attention",
    )(q, k, vt)


def _halo_specs(tm, t, width, col_block):
    per = tm // HALO
    last = t // HALO - 1
    return [
        pl.BlockSpec((1, tm, width), lambda bb, i: (bb, i, col_block)),
        pl.BlockSpec((1, HALO, width), lambda bb, i: (bb, jnp.maximum(i * per - 1, 0), col_block)),
        pl.BlockSpec((1, HALO, width), lambda bb, i: (bb, jnp.minimum((i + 1) * per, last), col_block)),
    ]


SUBLANES = 8


def _conv_rows(ext_ref, w_ref, taps, first_row, tm, width, out_fn, shift_ref=None, rows_per=64):
    if shift_ref is not None:
        n_shift = shift_ref.shape[1]
        for c0 in range(0, width, LANES):
            for r in range(1, SUBLANES):
                shift_ref[r - 1, :, c0:c0 + LANES] = ext_ref[r:r + n_shift, c0:c0 + LANES]
    for c0 in range(0, width, LANES):
        for r0 in range(0, tm, rows_per):
            acc = None
            for k in range(taps):
                lo = first_row + r0 + k
                res = lo % SUBLANES
                if shift_ref is None or res == 0:
                    rows = ext_ref[lo:lo + rows_per, c0:c0 + LANES]
                else:
                    rows = shift_ref[res - 1, lo - res:lo - res + rows_per, c0:c0 + LANES]
                term = rows * w_ref[k:k + 1, c0:c0 + LANES]
                acc = term if acc is None else acc + term
            out_fn(r0, rows_per, c0, acc)


def _conf_kernel(a_ref, ap_ref, an_ref, g_ref, gp_ref, gn_ref, cw_ref, cb_ref, lw_ref, lb_ref,
                 o_ref, u_ref, v_ref, us_ref, *, tm):
    i = pl.program_id(1)
    nt = pl.num_programs(1)
    u_ref[HALO:HALO + tm] = a_ref[0] * _sigmoid(g_ref[0])
    u_ref[0:HALO] = jnp.where(i > 0, ap_ref[0] * _sigmoid(gp_ref[0]), 0.0)
    u_ref[HALO + tm:] = jnp.where(i < nt - 1, an_ref[0] * _sigmoid(gn_ref[0]), 0.0)

    def put(r0, nr, c0, acc):
        v_ref[r0:r0 + nr, c0:c0 + LANES] = acc + cb_ref[:, c0:c0 + LANES]

    pad = (CONF_KERNEL - 1) // 2
    _conv_rows(u_ref, cw_ref, CONF_KERNEL, HALO - pad, tm, CONF_WIDTH, put, shift_ref=us_ref)
    v = v_ref[...]
    mu = jnp.mean(v, axis=-1, keepdims=True)
    d = v - mu
    var = jnp.mean(d * d, axis=-1, keepdims=True)
    y = d * lax.rsqrt(var + EPS) * lw_ref[...] + lb_ref[...]
    o_ref[0] = _silu(y).astype(BF16)


def _conformer(proj, cw, cb, lw, lb, tm):
    b, t, _ = proj.shape
    vec = pl.BlockSpec((1, CONF_WIDTH), lambda bb, i: (0, 0))
    return pl.pallas_call(
        functools.partial(_conf_kernel, tm=tm),
        grid=(b, t // tm),
        in_specs=_halo_specs(tm, t, CONF_WIDTH, COL_A // CONF_WIDTH)
        + _halo_specs(tm, t, CONF_WIDTH, COL_G // CONF_WIDTH)
        + [pl.BlockSpec((CONF_KERNEL, CONF_WIDTH), lambda bb, i: (0, 0)), vec, vec, vec],
        out_specs=pl.BlockSpec((1, tm, CONF_WIDTH), lambda bb, i: (bb, i, 0)),
        out_shape=jax.ShapeDtypeStruct((b, t, CONF_WIDTH), BF16),
        scratch_shapes=[
            pltpu.VMEM((tm + 2 * HALO, CONF_WIDTH), F32),
            pltpu.VMEM((tm, CONF_WIDTH), F32),
            pltpu.VMEM((SUBLANES - 1, tm + 2 * HALO - SUBLANES, CONF_WIDTH), F32),
        ],
        compiler_params=_cparams(("parallel", "parallel")),
        name="conformer",
    )(proj, proj, proj, proj, proj, proj, cw, cb, lw, lb)


def _ssdprep_kernel(x_ref, xp_ref, xn_ref, bc_ref, bcp_ref, bcn_ref, dt_ref, cwx_ref, cbx_ref,
                    cwb_ref, cbb_ref, dtb_ref, xo_ref, bco_ref, bmt_ref, dto_ref, xe_ref, be_ref, bv_ref, *, tm):
    i = pl.program_id(1)
    nt = pl.num_programs(1)
    xe_ref[HALO:HALO + tm] = x_ref[0]
    xe_ref[0:HALO] = jnp.where(i > 0, xp_ref[0], 0.0)
    xe_ref[HALO + tm:] = jnp.where(i < nt - 1, xn_ref[0], 0.0)
    be_ref[HALO:HALO + tm] = bc_ref[0]
    be_ref[0:HALO] = jnp.where(i > 0, bcp_ref[0], 0.0)
    be_ref[HALO + tm:] = jnp.where(i < nt - 1, bcn_ref[0], 0.0)
    pad = (SSM_CONV - 1) // 2

    def put_x(r0, nr, c0, acc):
        xo_ref[0, r0:r0 + nr, c0:c0 + LANES] = _silu(acc + cbx_ref[:, c0:c0 + LANES])

    def put_b(r0, nr, c0, acc):
        bv_ref[r0:r0 + nr, c0:c0 + LANES] = _silu(acc + cbb_ref[:, c0:c0 + LANES])

    _conv_rows(xe_ref, cwx_ref, SSM_CONV, HALO - pad, tm, SSM_WIDTH, put_x)
    _conv_rows(be_ref, cwb_ref, SSM_CONV, HALO - pad, tm, SSM_BC, put_b)
    bv = bv_ref[...]
    bco_ref[0] = bv.astype(BF16)
    bmt_ref[0] = bv[:, :SSM_GROUPS * SSM_STATE].T.astype(BF16)
    x = dt_ref[0] + dtb_ref[...]
    dto_ref[0] = jnp.maximum(x, 0.0) + jnp.log(1.0 + jnp.exp(-jnp.abs(x)))


def _ssdprep(proj, cwx, cbx, cwb, cbb, dtb, tm):
    b, t, _ = proj.shape
    gs = SSM_GROUPS * SSM_STATE
    return pl.pallas_call(
        functools.partial(_ssdprep_kernel, tm=tm),
        grid=(b, t // tm),
        in_specs=_halo_specs(tm, t, SSM_WIDTH, COL_XS // SSM_WIDTH)
        + _halo_specs(tm, t, SSM_BC, COL_BC // SSM_BC)
        + [
            pl.BlockSpec((1, tm, LANES), lambda bb, i: (bb, i, COL_DT // LANES)),
            pl.BlockSpec((SSM_CONV, SSM_WIDTH), lambda bb, i: (0, 0)),
            pl.BlockSpec((1, SSM_WIDTH), lambda bb, i: (0, 0)),
            pl.BlockSpec((SSM_CONV, SSM_BC), lambda bb, i: (0, 0)),
            pl.BlockSpec((1, SSM_BC), lambda bb, i: (0, 0)),
            pl.BlockSpec((1, LANES), lambda bb, i: (0, 0)),
        ],
        out_specs=[
            pl.BlockSpec((1, tm, SSM_WIDTH), lambda bb, i: (bb, i, 0)),
            pl.BlockSpec((1, tm, SSM_BC), lambda bb, i: (bb, i, 0)),
            pl.BlockSpec((1, gs, tm), lambda bb, i: (bb, 0, i)),
            pl.BlockSpec((1, tm, LANES), lambda bb, i: (bb, i, 0)),
        ],
        out_shape=[
            jax.ShapeDtypeStruct((b, t, SSM_WIDTH), F32),
            jax.ShapeDtypeStruct((b, t, SSM_BC), BF16),
            jax.ShapeDtypeStruct((b, gs, t), BF16),
            jax.ShapeDtypeStruct((b, t, LANES), F32),
        ],
        scratch_shapes=[
            pltpu.VMEM((tm + 2 * HALO, SSM_WIDTH), F32),
            pltpu.VMEM((tm + 2 * HALO, SSM_BC), F32),
            pltpu.VMEM((tm, SSM_BC), F32),
        ],
        compiler_params=_cparams(("parallel", "parallel")),
        name="ssd_prep",
    )(proj, proj, proj, proj, proj, proj, proj, cwx, cbx, cwb, cbb, dtb)


def _split_hi_lo(v):
    hi = v.astype(BF16)
    lo = (v - hi.astype(F32)).astype(BF16)
    return jnp.concatenate([hi, lo], axis=1)


def _ssd_kernel(*refs, direction, with_y, finish):
    it = iter(refs)
    xs_ref, bc_ref, bmt_ref, dt_ref, a_ref, e_ref, h0_ref = (next(it) for _ in range(7))
    if finish:
        yf_ref, z_ref, dsum_ref, nw_ref = (next(it) for _ in range(4))
    if with_y:
        y_ref = next(it)
    hfin_ref = next(it)
    st_ref = next(it)
    q = SSM_CHUNK
    c = pl.program_id(1)

    @pl.when(c == 0)
    def _():
        st_ref[...] = h0_ref[0]

    ii = lax.broadcasted_iota(jnp.int32, (q, q), 0)
    jj = lax.broadcasted_iota(jnp.int32, (q, q), 1)
    if direction == 0:
        valid, valid_t, end_row = jj <= ii, ii <= jj, q - 1
    else:
        valid, valid_t, end_row = jj >= ii, ii >= jj, 0
    dt = dt_ref[0]
    da = dt * a_ref[...]
    tri = jnp.where(valid, 1.0, 0.0)
    tri_t = jnp.where(valid_t, 1.0, 0.0)
    acum_c = jnp.dot(tri, da, precision=lax.Precision.HIGHEST, preferred_element_type=F32)
    acum_r = jnp.dot(da.T, tri_t, precision=lax.Precision.HIGHEST, preferred_element_type=F32)
    tot = acum_c[end_row:end_row + 1, :]
    to_end = jnp.exp(tot - acum_c)
    ea = jnp.exp(acum_c)

    e = e_ref[...]

    def expand(v):
        return jnp.dot(_split_hi_lo(v), e, preferred_element_type=F32)

    ea_e = expand(ea)
    xs = xs_ref[0]
    xw = (xs * expand(dt * to_end)).astype(BF16)
    cdec_e = ea_e[end_row:end_row + 1, :]
    bc = bc_ref[0]
    bmt = bmt_ref[0]
    if with_y:
        xdt = (xs * expand(dt)).astype(BF16)
        lane = lax.broadcasted_iota(jnp.int32, xdt.shape, 1)
        left = (lane % LANES) < SSM_HEAD_DIM
        zero = jnp.zeros_like(xdt)
        xdt_l = jnp.where(left, xdt, zero)
        xdt_r = jnp.where(left, zero, xdt)

    gw = SSM_GROUP_WIDTH
    hpg = SSM_HEADS // SSM_GROUPS
    for g in range(SSM_GROUPS):
        bm_t = bmt[g * SSM_STATE:(g + 1) * SSM_STATE, :]
        st = st_ref[g]
        if with_y:
            cm = bc[:, (SSM_GROUPS + g) * SSM_STATE:(SSM_GROUPS + g + 1) * SSM_STATE]
            bm = bc[:, g * SSM_STATE:(g + 1) * SSM_STATE]
            cb = lax.dot_general(cm, bm, (((1,), (1,)), ((), ())), preferred_element_type=F32)
            y_inter = jnp.dot(cm, st.astype(BF16), preferred_element_type=F32) * ea_e[:, g * gw:(g + 1) * gw]
            parts = []
            for k in range(hpg // 2):
                ms = []
                for r in (2 * k, 2 * k + 1):
                    hl = SSM_HEADS * direction + g * hpg + r
                    seg = acum_c[:, hl:hl + 1] - acum_r[hl:hl + 1, :]
                    ms.append((cb * jnp.where(valid, jnp.exp(seg), 0.0)).astype(BF16))
                c0 = g * gw + k * LANES
                rhs = jnp.concatenate([xdt_l[:, c0:c0 + LANES], xdt_r[:, c0:c0 + LANES]], axis=0)
                parts.append(jnp.dot(jnp.concatenate(ms, axis=1), rhs, preferred_element_type=F32))
            y_g = y_inter + jnp.concatenate(parts, axis=1)
            if finish:
                sl = slice(g * gw, (g + 1) * gw)
                y_ref[:, sl] = yf_ref[0, :, sl] + y_g + dsum_ref[:, sl] * xs[:, sl]
            else:
                y_ref[0, :, g * gw:(g + 1) * gw] = y_g
        st_ref[g] = st * cdec_e[:, g * gw:(g + 1) * gw] + jnp.dot(
            bm_t, xw[:, g * gw:(g + 1) * gw], preferred_element_type=F32)

    if finish:
        out_ref = next(it)
        gz = y_ref[...] * _silu(z_ref[0])
        var = jnp.mean(gz * gz, axis=-1, keepdims=True)
        out_ref[0] = (gz * lax.rsqrt(var + EPS) * nw_ref[...]).astype(BF16)

    @pl.when(c == pl.num_programs(1) - 1)
    def _():
        hfin_ref[0] = st_ref[...]


def _ssd_scan(xs, bc, bmt, dt, a_row, e_mat, h0, direction, with_y, fin=None):
    b, t, _ = xs.shape
    nc = t // SSM_CHUNK
    q = SSM_CHUNK
    gs = SSM_GROUPS * SSM_STATE
    finish = fin is not None
    if direction == 0:
        tmap = lambda bb, c: (bb, c, 0)
        tmap_t = lambda bb, c: (bb, 0, c)
        zmap = lambda bb, c: (bb, c, COL_Z // SSM_WIDTH)
    else:
        tmap = lambda bb, c: (bb, nc - 1 - c, 0)
        tmap_t = lambda bb, c: (bb, 0, nc - 1 - c)
        zmap = lambda bb, c: (bb, nc - 1 - c, COL_Z // SSM_WIDTH)
    st_spec = pl.BlockSpec((1, SSM_GROUPS, SSM_STATE, SSM_GROUP_WIDTH), lambda bb, c: (bb, 0, 0, 0))
    in_specs = [
        pl.BlockSpec((1, q, SSM_WIDTH), tmap),
        pl.BlockSpec((1, q, SSM_BC), tmap),
        pl.BlockSpec((1, gs, q), tmap_t),
        pl.BlockSpec((1, q, LANES), tmap),
        pl.BlockSpec((1, LANES), lambda bb, c: (0, 0)),
        pl.BlockSpec((2 * LANES, SSM_WIDTH), lambda bb, c: (0, 0)),
        st_spec,
    ]
    args = [xs, bc, bmt, dt, a_row, e_mat, h0]
    out_specs, out_shape, scratch = [], [], []
    if finish:
        y_fwd, proj, dsum_e, nw = fin
        in_specs += [
            pl.BlockSpec((1, q, SSM_WIDTH), tmap),
            pl.BlockSpec((1, q, SSM_WIDTH), zmap),
            pl.BlockSpec((1, SSM_WIDTH), lambda bb, c: (0, 0)),
            pl.BlockSpec((1, SSM_WIDTH), lambda bb, c: (0, 0)),
        ]
        args += [y_fwd, proj, dsum_e, nw]
    elif with_y:
        out_specs.append(pl.BlockSpec((1, q, SSM_WIDTH), tmap))
        out_shape.append(jax.ShapeDtypeStruct((b, t, SSM_WIDTH), F32))
    out_specs.append(st_spec)
    out_shape.append(jax.ShapeDtypeStruct((b, SSM_GROUPS, SSM_STATE, SSM_GROUP_WIDTH), F32))
    scratch.append(pltpu.VMEM((SSM_GROUPS, SSM_STATE, SSM_GROUP_WIDTH), F32))
    if finish:
        out_specs.append(pl.BlockSpec((1, q, SSM_WIDTH), tmap))
        out_shape.append(jax.ShapeDtypeStruct((b, t, SSM_WIDTH), BF16))
    return pl.pallas_call(
        functools.partial(_ssd_kernel_ordered, direction=direction, with_y=with_y, finish=finish),
        grid=(b, nc),
        in_specs=in_specs,
        out_specs=out_specs,
        out_shape=out_shape,
        scratch_shapes=scratch + ([pltpu.VMEM((q, SSM_WIDTH), F32)] if finish else []),
        compiler_params=_cparams(("parallel", "arbitrary")),
        name="ssd_scan",
    )(*args)


def _ssd_kernel_ordered(*refs, direction, with_y, finish):
    n_in = 11 if finish else 7
    ins = list(refs[:n_in])
    rest = list(refs[n_in:])
    if finish:
        hfin, out, st, ybuf = rest
        ordered = ins + [ybuf, hfin, st, out]
    elif with_y:
        y, hfin, st = rest
        ordered = ins + [y, hfin, st]
    else:
        hfin, st = rest
        ordered = ins + [hfin, st]
    _ssd_kernel(*ordered, direction=direction, with_y=with_y, finish=finish)


def _merge_kernel(x_ref, nw_ref, sh_ref, sc_ref, gt_ref, a_ref, b_ref, c_ref, wg0_ref, wg1_ref, wg2_ref,
                  wa_ref, wb_ref, wc_ref, wo_ref, o_ref, h_ref):
    j = pl.program_id(2)

    @pl.when(j == 0)
    def _():
        h_ref[...] = _rms_mod(x_ref[0], nw_ref[...], sh_ref[0], sc_ref[0]).astype(BF16)
        o_ref[...] = jnp.zeros_like(o_ref)

    h = h_ref[...]
    s = None
    for act_ref, wp_ref, wg_ref in ((a_ref, wa_ref, wg0_ref), (b_ref, wb_ref, wg1_ref), (c_ref, wc_ref, wg2_ref)):
        y = jnp.dot(act_ref[0], wp_ref[...], preferred_element_type=F32)
        gate = _sigmoid(jnp.dot(h, wg_ref[...], preferred_element_type=F32))
        s = gate * y if s is None else s + gate * y
    o_ref[0] += jnp.dot(s.astype(BF16), wo_ref[...], preferred_element_type=F32)

    @pl.when(j == pl.num_programs(2) - 1)
    def _():
        o_ref[0] = x_ref[0] + gt_ref[0] * o_ref[0]


def _merge(x, nw, shift, scale, gate, act_a, act_b, act_c, wg, wa, wb, wc, wo, tm):
    b, t, d = x.shape
    tn = 512
    nj = d // tn
    vec = pl.BlockSpec((1, 1, d), lambda bb, i, j: (bb, 0, 0))
    act =pl.BlockSpec((1, tm, ATTN_WIDTH), lambda bb, i, j: (bb, i, 0))
    wproj = pl.BlockSpec((ATTN_WIDTH, tn), lambda bb, i, j: (0, j))
    return pl.pallas_call(
        _merge_kernel,
        grid=(b, t // tm, nj),
        in_specs=[
            pl.BlockSpec((1, tm, d), lambda bb, i, j: (bb, i, 0)),
            pl.BlockSpec((1, d), lambda bb, i, j: (0, 0)),
            vec, vec, vec, act, act, act,
            pl.BlockSpec((d, tn), lambda bb, i, j: (0, j)),
            pl.BlockSpec((d, tn), lambda bb, i, j: (0, nj + j)),
            pl.BlockSpec((d, tn), lambda bb, i, j: (0, 2 * nj + j)),
            wproj, wproj, wproj,
            pl.BlockSpec((tn, d), lambda bb, i, j: (j, 0)),
        ],
        out_specs=pl.BlockSpec((1, tm, d), lambda bb, i, j: (bb, i, 0)),
        out_shape=jax.ShapeDtypeStruct((b, t, d), F32),
        scratch_shapes=[pltpu.VMEM((tm, d), BF16)],
        compiler_params=_cparams(("parallel", "parallel", "arbitrary")),
        name="branch_merge",
    )(x, nw, shift, scale, gate, act_a, act_b, act_c, wg, wg, wg, wa, wb, wc, wo)


def _ffn_kernel(x_ref, xp_ref, xn_ref, nw_ref, sh_ref, sc_ref, gt_ref, wug_ref, wuv_ref, cwg_ref, cwv_ref,
                cbg_ref, cbv_ref, wd_ref, o_ref, h_ref, *, tm):
    i = pl.program_id(1)
    j = pl.program_id(2)

    @pl.when(j == 0)
    def _():
        nw, sh, sc = nw_ref[...], sh_ref[0], sc_ref[0]
        h_ref[HALO:HALO + tm] = _rms_mod(x_ref[0], nw, sh, sc).astype(BF16)
        hp = _rms_mod(xp_ref[0], nw, sh, sc)
        h_ref[0:HALO] = jnp.where(i > 0, hp, 0.0).astype(BF16)
        hn = _rms_mod(xn_ref[0], nw, sh, sc)
        h_ref[HALO + tm:] = jnp.where(i < pl.num_programs(1) - 1, hn, 0.0).astype(BF16)
        o_ref[...] = jnp.zeros_like(o_ref)

    h = h_ref[...]
    rows = tm + 2 * HALO

    def conv(p, cw_ref, cb_ref):
        prev = pltpu.roll(p, 1, 0)[HALO:HALO + tm]
        nxt = pltpu.roll(p, rows - 1, 0)[HALO:HALO + tm]
        return prev * cw_ref[0:1, :] + p[HALO:HALO + tm] * cw_ref[1:2, :] + nxt * cw_ref[2:3, :] + cb_ref[...]

    ug = conv(jnp.dot(h, wug_ref[...], preferred_element_type=F32), cwg_ref, cbg_ref)
    uv = conv(jnp.dot(h, wuv_ref[...], preferred_element_type=F32), cwv_ref, cbv_ref)
    s = (_silu(ug) * uv).astype(BF16)
    o_ref[0] += jnp.dot(s, wd_ref[...], preferred_element_type=F32)

    @pl.when(j == pl.num_programs(2) - 1)
    def _():
        o_ref[0] = x_ref[0] + gt_ref[0] * o_ref[0]


def _ffn(x, nw, shift, scale, gate, w_up, conv_w, conv_b, w_down, tm):
    b, t, d = x.shape
    dff = w_down.shape[0]
    tf = 512
    nj = dff // tf
    per = tm // HALO
    last = t // HALO - 1
    vec = pl.BlockSpec((1, 1, d), lambda bb, i, j: (bb, 0, 0))
    return pl.pallas_call(
        functools.partial(_ffn_kernel, tm=tm),
        grid=(b, t // tm, nj),
        in_specs=[
            pl.BlockSpec((1, tm, d), lambda bb, i, j: (bb, i, 0)),
            pl.BlockSpec((1, HALO, d), lambda bb, i, j: (bb, jnp.maximum(i * per - 1, 0), 0)),
            pl.BlockSpec((1, HALO, d), lambda bb, i, j: (bb, jnp.minimum((i + 1) * per, last), 0)),
            pl.BlockSpec((1, d), lambda bb, i, j: (0, 0)),
            vec, vec, vec,
            pl.BlockSpec((d, tf), lambda bb, i, j: (0, j)),
            pl.BlockSpec((d, tf), lambda bb, i, j: (0, nj + j)),
            pl.BlockSpec((FFN_CONV, tf), lambda bb, i, j: (0, j)),
            pl.BlockSpec((FFN_CONV, tf), lambda bb, i, j: (0, nj + j)),
            pl.BlockSpec((1, tf), lambda bb, i, j: (0, j)),
            pl.BlockSpec((1, tf), lambda bb, i, j: (0, nj + j)),
            pl.BlockSpec((tf, d), lambda bb, i, j: (j, 0)),
        ],
        out_specs=pl.BlockSpec((1, tm, d), lambda bb, i, j: (bb, i, 0)),
        out_shape=jax.ShapeDtypeStruct((b, t, d), F32),
        scratch_shapes=[pltpu.VMEM((tm + 2 * HALO, d), BF16)],
        compiler_params=_cparams(("parallel", "parallel", "arbitrary")),
        name="conv_ffn",
    )(x, x, x, nw, shift, scale, gate, w_up, w_up, conv_w, conv_w, conv_b, conv_b, w_down)


def _final_norm_kernel(x_ref, w_ref, o_ref):
    x = x_ref[0]
    var = jnp.mean(x * x, axis=-1, keepdims=True)
    o_ref[0] = x * lax.rsqrt(var + EPS) * w_ref[...]


def _final_norm(x, w, tm):
    b, t, d = x.shape
    return pl.pallas_call(
        _final_norm_kernel,
        grid=(b, t // tm),
        in_specs=[pl.BlockSpec((1, tm, d), lambda bb, i: (bb, i, 0)), pl.BlockSpec((1, d), lambda bb, i: (0, 0))],
        out_specs=pl.BlockSpec((1, tm, d), lambda bb, i: (bb, i, 0)),
        out_shape=jax.ShapeDtypeStruct((b, t, d), F32),
        compiler_params=_cparams(("parallel", "parallel")),
        name="final_norm",
    )(x, w)


def _rope_tables(n_tokens):
    rows = n_tokens // GRID_W
    row = jnp.repeat(jnp.arange(rows), GRID_W).astype(F32)
    col = jnp.tile(jnp.arange(GRID_W), rows).astype(F32)
    inv = ROPE_THETA ** (-jnp.arange(0, ROPE_AXIS_DIM, 2, dtype=F32) / ROPE_AXIS_DIM)
    ar, ac = row[:, None] * inv, col[:, None] * inv
    cos = jnp.concatenate([jnp.cos(ar), jnp.cos(ar), jnp.cos(ac), jnp.cos(ac)], axis=1)
    sin = jnp.concatenate([-jnp.sin(ar), jnp.sin(ar), -jnp.sin(ac), jnp.sin(ac)], axis=1)
    return cos, sin


def _expansion_matrix(direction):
    head_of_col = jnp.arange(SSM_WIDTH) // SSM_HEAD_DIM
    row = jnp.arange(LANES)[:, None]
    e = (row == head_of_col[None, :] + SSM_HEADS * direction).astype(BF16)
    return jnp.concatenate([e, e], axis=0)


def _pad_lanes(v, offset):
    return jnp.zeros((1, LANES), F32).at[0, offset:offset + v.shape[0]].set(v.astype(F32))


def kernel(x, c, ctx, c_ctx, w_mod, b_mod, norm_mix_w, norm_ffn_w, w_in, q_norm_w, k_norm_w, w_attn_o, conf_conv_w, conf_conv_b, conf_ln_w, conf_ln_b, w_conf_o, ssm_conv_w, ssm_conv_b, ssm_a_log, ssm_dt_bias, ssm_d, ssm_norm_w, w_ssm_o, w_out, ffn_w_up, ffn_conv_w, ffn_conv_b, ffn_w_down, final_norm_w):
    bsz, n_lat, d = x.shape
    n_ctx = ctx.shape[1]
    depth = w_mod.shape[0]
    assert bsz <= 7 and n_lat % 512 == 0 and n_ctx % 256 == 0

    q_end = ATTN_WIDTH
    v_end = q_end + 2 * KV_WIDTH
    conf_end = v_end + 2 * CONF_WIDTH
    z_end = conf_end + SSM_WIDTH
    xbc_end = z_end + SSM_WIDTH + SSM_BC
    dt_end = xbc_end + 2 * SSM_HEADS

    cc = jnp.zeros((8, d), F32).at[:bsz].set(c).at[bsz].set(c_ctx)
    mods = _mod_vectors(cc, w_mod, b_mod).reshape(depth, 8, N_MOD, d)
    rope = _rope_tables(n_lat)
    e_mats = [_expansion_matrix(0), _expansion_matrix(1)]
    zero_state = jnp.zeros((bsz, SSM_GROUPS, SSM_STATE, SSM_GROUP_WIDTH), F32)

    for l in range(depth):
        need_ctx = l < depth - 1
        wi = w_in[l]
        w_mix = jnp.concatenate([
            wi[:, :q_end], wi[:, v_end:conf_end], wi[:, conf_end:z_end], wi[:, z_end:z_end + SSM_WIDTH],
            wi[:, q_end:v_end], wi[:, z_end + SSM_WIDTH:xbc_end], wi[:, xbc_end:dt_end],
            jnp.zeros((d, MIX_WIDTH - dt_end), F32)], axis=1).astype(BF16)
        w_gate = wi[:, dt_end:].astype(BF16)
        wa, wb, wc, wo = (w[l].astype(BF16) for w in (w_attn_o, w_conf_o, w_ssm_o, w_out))
        w_up, w_down = ffn_w_up[l].astype(BF16), ffn_w_down[l].astype(BF16)
        row = lambda v: v.reshape(1, -1)
        nmix, nffn = row(norm_mix_w[l]), row(norm_ffn_w[l])
        cwx, cwb = ssm_conv_w[l][:, :SSM_WIDTH], ssm_conv_w[l][:, SSM_WIDTH:]
        cbx, cbb = row(ssm_conv_b[l][:SSM_WIDTH]), row(ssm_conv_b[l][SSM_WIDTH:])
        a_coef = -jnp.exp(ssm_a_log[l].astype(F32))
        a_rows = [_pad_lanes(a_coef[0], 0), _pad_lanes(a_coef[1], SSM_HEADS)]
        dtb = _pad_lanes(ssm_dt_bias[l].reshape(-1), 0)
        dsum_e = row(jnp.repeat(ssm_d[l][0] + ssm_d[l][1], SSM_HEAD_DIM))
        m_lat = [mods[l, :bsz, k].reshape(bsz, 1, d) for k in range(N_MOD)]
        m_ctx = [jnp.broadcast_to(mods[l, bsz, k].reshape(1, 1, d), (bsz, 1, d)) for k in range(N_MOD)]

        proj_l = _inproj(x, nmix, m_lat[0], m_lat[1], w_mix, 1024 if n_lat % 1024 == 0 else 512)
        proj_c = _inproj(ctx, nmix, m_ctx[0], m_ctx[1], w_mix, 256)

        q_l, k_l, v_l = _qkprep(proj_l, row(q_norm_w[l]), row(k_norm_w[l]), rope, 256)
        q_c, k_c, v_c = _qkprep(proj_c, row(q_norm_w[l]), row(k_norm_w[l]), None, 256)
        act_a_l = _attention(q_l, jnp.concatenate([k_c, k_l], axis=1), jnp.concatenate([v_c, v_l], axis=2), 256, 256)
        conf_args = (conf_conv_w[l], row(conf_conv_b[l]), row(conf_ln_w[l]), row(conf_ln_b[l]))
        act_b_l = _conformer(proj_l, *conf_args, 256)

        prep_l = _ssdprep(proj_l, cwx, cbx, cwb, cbb, dtb, 256)
        prep_c = _ssdprep(proj_c, cwx, cbx, cwb, cbb, dtb, 256)
        fin_args = (dsum_e, row(ssm_norm_w[l]))
        if need_ctx:
            y_cf, h_cf = _ssd_scan(*prep_c, a_rows[0], e_mats[0], zero_state, 0, True)
            h_cb, act_c_c = _ssd_scan(*prep_c, a_rows[1], e_mats[1], zero_state, 1, True, (y_cf, proj_c) + fin_args)
        else:
            (h_cf,) = _ssd_scan(*prep_c, a_rows[0], e_mats[0], zero_state, 0, False)
            (h_cb,) = _ssd_scan(*prep_c, a_rows[1], e_mats[1], zero_state, 1, False)
        y_lf, _ = _ssd_scan(*prep_l, a_rows[0], e_mats[0], h_cf, 0, True)
        _, act_c_l = _ssd_scan(*prep_l, a_rows[1], e_mats[1], h_cb, 1, True, (y_lf, proj_l) + fin_args)

        x = _merge(x, nmix, m_lat[0], m_lat[1], m_lat[2], act_a_l, act_b_l, act_c_l, w_gate, wa, wb, wc, wo, 512)
        ffn_args = (w_up, ffn_conv_w[l], row(ffn_conv_b[l]), w_down)
        x = _ffn(x, nffn, m_lat[3], m_lat[4], m_lat[5], *ffn_args, 512)
        if need_ctx:
            act_a_c = _attention(q_c, k_c, v_c, 256, 256)
            act_b_c = _conformer(proj_c, *conf_args, 256)
            ctx = _merge(ctx, nmix, m_ctx[0], m_ctx[1], m_ctx[2], act_a_c, act_b_c, act_c_c, w_gate, wa, wb, wc, wo, 256)
            ctx = _ffn(ctx, nffn, m_ctx[3], m_ctx[4], m_ctx[5], *ffn_args, 256)
    return _final_norm(x, final_norm_w.reshape(1, d), 512)
```

```python
import functools
import math

import jax
import jax.numpy as jnp
from jax import lax
from jax.experimental import pallas as pl
from jax.experimental.pallas import tpu as pltpu

F32 = jnp.float32
BF16 = jnp.bfloat16

EPS = 1e-6
N_MOD = 6
GRID_W = 64
HEAD_DIM = 128
ATTN_HEADS = 8
ATTN_KV_HEADS = 2
ATTN_REP = ATTN_HEADS // ATTN_KV_HEADS
ATTN_WIDTH = ATTN_HEADS * HEAD_DIM
KV_WIDTH = ATTN_KV_HEADS * HEAD_DIM
ROPE_THETA = 10000.0
ROPE_AXIS_DIM = HEAD_DIM // 2
ROPE_FREQS = ROPE_AXIS_DIM // 2
CONF_WIDTH = 1024
CONF_KERNEL = 31
SSM_HEADS = 16
SSM_HEAD_DIM = 64
SSM_WIDTH = SSM_HEADS * SSM_HEAD_DIM
SSM_GROUPS = 2
SSM_STATE = 128
SSM_GROUP_WIDTH = SSM_WIDTH // SSM_GROUPS
SSM_BC = 2 * SSM_GROUPS * SSM_STATE
SSM_CONV = 5
SSM_CHUNK = 128
FFN_CONV = 3
N_BRANCHES = 3

LANES = 128
HALO = 16
VMEM_LIMIT = 56 * 1024 * 1024
LOG2E = 1.4426950408889634
ATTN_QSCALE = HEAD_DIM ** -0.5 * LOG2E
VT_PAD = 16
VT_ROWS = HEAD_DIM + VT_PAD
NEG_BIG = -1e30

COL_Q = 0
COL_A = 1024
COL_G = 2048
COL_Z = 3072
COL_XS = 4096
COL_KV = 5120
COL_BC = 5632
COL_DT = 6144
MIX_WIDTH = 6272


def _cparams(sem):
    return pltpu.CompilerParams(dimension_semantics=sem, vmem_limit_bytes=VMEM_LIMIT)


def _sigmoid(x):
    return 1.0 / (1.0 + jnp.exp(-x))


def _silu(x):
    return x * _sigmoid(x)


def _rms_mod(x, nw, shift, scale):
    var = jnp.mean(x * x, axis=-1, keepdims=True)
    y = x * lax.rsqrt(var + EPS) * nw
    return y * (1.0 + scale) + shift


def _mod_kernel(c_ref, w_ref, b_ref, o_ref):
    s = _silu(c_ref[...]).astype(BF16)
    o_ref[0] = jnp.dot(s, w_ref[0].astype(BF16), preferred_element_type=F32) + b_ref[0]


def _mod_vectors(cc, w_mod, b_mod):
    depth, d, nm = w_mod.shape
    tn = 1024
    return pl.pallas_call(
        _mod_kernel,
        grid=(depth, nm // tn),
        in_specs=[
            pl.BlockSpec((8, d), lambda l, j: (0, 0)),
            pl.BlockSpec((1, d, tn), lambda l, j: (l, 0, j)),
            pl.BlockSpec((1, 1, tn), lambda l, j: (l, 0, j)),
        ],
        out_specs=pl.BlockSpec((1, 8, tn), lambda l, j: (l, 0, j)),
        out_shape=jax.ShapeDtypeStruct((depth, 8, nm), F32),
        compiler_params=_cparams(("parallel", "parallel")),
        name="mod_vectors",
    )(cc, w_mod, b_mod.reshape(depth, 1, nm))


ROW_CHUNK = 256


def _inproj_kernel(x_ref, nw_ref, sh_ref, sc_ref, w_ref, o_ref, h_ref):
    j = pl.program_id(2)

    @pl.when(j == 0)
    def _():
        for r in range(0, h_ref.shape[0], ROW_CHUNK):
            rows = slice(r, r + ROW_CHUNK)
            hc = _rms_mod(x_ref[0, rows], nw_ref[...], sh_ref[0], sc_ref[0]).astype(BF16)
            h_ref[rows] = hc
            o_ref[0, rows] = jnp.dot(hc, w_ref[...], preferred_element_type=F32)

    @pl.when(j > 0)
    def _():
        o_ref[0] = jnp.dot(h_ref[...], w_ref[...], preferred_element_type=F32)


def _inproj(x, nw, shift, scale, w, tm):
    b, t, d = x.shape
    n = w.shape[1]
    tn = 896
    return pl.pallas_call(
        _inproj_kernel,
        grid=(b, t // tm, n // tn),
        in_specs=[
            pl.BlockSpec((1, tm, d), lambda bb, i, j: (bb, i, 0)),
            pl.BlockSpec((1, d), lambda bb, i, j: (0, 0)),
            pl.BlockSpec((1, 1, d), lambda bb, i, j: (bb, 0, 0)),
            pl.BlockSpec((1, 1, d), lambda bb, i, j: (bb, 0, 0)),
            pl.BlockSpec((d, tn), lambda bb, i, j: (0, j)),
        ],
        out_specs=pl.BlockSpec((1, tm, tn), lambda bb, i, j: (bb, i, j)),
        out_shape=jax.ShapeDtypeStruct((b, t, n), F32),
        scratch_shapes=[pltpu.VMEM((tm, d), BF16)],
        compiler_params=_cparams(("parallel", "parallel", "arbitrary")),
        name="mixer_inproj",
    )(x, nw, shift, scale, w)


def _head_norm(x, w):
    var = jnp.mean(x * x, axis=-1, keepdims=True)
    return x * lax.rsqrt(var + EPS) * w


def _rope(x, c, s):
    lane = lax.broadcasted_iota(jnp.int32, x.shape, 1)
    first = (lane % ROPE_AXIS_DIM) < ROPE_FREQS
    partner = jnp.where(first, pltpu.roll(x, HEAD_DIM - ROPE_FREQS, 1), pltpu.roll(x, ROPE_FREQS, 1))
    return x * c + partner * s


def _qkprep_kernel(*refs, use_rope):
    if use_rope:
        q_ref, kv_ref, qw_ref, kw_ref, c_ref, s_ref, qo_ref, ko_ref, vo_ref = refs
    else:
        q_ref, kv_ref, qw_ref, kw_ref, qo_ref, ko_ref, vo_ref = refs
    q = q_ref[0]
    kv = kv_ref[0]
    for h in range(ATTN_HEADS):
        y = _head_norm(q[:, h * HEAD_DIM:(h + 1) * HEAD_DIM], qw_ref[...])
        if use_rope:
            y = _rope(y, c_ref[...], s_ref[...])
        qo_ref[0, :, h * HEAD_DIM:(h + 1) * HEAD_DIM] = (y * ATTN_QSCALE).astype(BF16)
    for h in range(ATTN_KV_HEADS):
        y = _head_norm(kv[:, h * HEAD_DIM:(h + 1) * HEAD_DIM], kw_ref[...])
        if use_rope:
            y = _rope(y, c_ref[...], s_ref[...])
        ko_ref[0, :, h * HEAD_DIM:(h + 1) * HEAD_DIM] = y.astype(BF16)
    vt = kv[:, KV_WIDTH:].T.astype(BF16)
    ones = jnp.ones((VT_PAD, vt.shape[1]), BF16)
    for g in range(ATTN_KV_HEADS):
        vo_ref[0, g * VT_ROWS:g * VT_ROWS + HEAD_DIM] = vt[g * HEAD_DIM:(g + 1) * HEAD_DIM]
        vo_ref[0, g * VT_ROWS + HEAD_DIM:(g + 1) * VT_ROWS] = ones


def _qkprep(proj, qw, kw, rope, tm):
    b, t, _ = proj.shape
    use_rope = rope is not None
    in_specs = [
        pl.BlockSpec((1, tm, ATTN_WIDTH), lambda bb, i: (bb, i, COL_Q // ATTN_WIDTH)),
        pl.BlockSpec((1, tm, 2 * KV_WIDTH), lambda bb, i: (bb, i, COL_KV // (2 * KV_WIDTH))),
        pl.BlockSpec((1, HEAD_DIM), lambda bb, i: (0, 0)),
        pl.BlockSpec((1, HEAD_DIM), lambda bb, i: (0, 0)),
    ]
    args = [proj, proj, qw, kw]
    if use_rope:
        in_specs += [pl.BlockSpec((tm, HEAD_DIM), lambda bb, i: (i, 0))] * 2
        args += list(rope)
    return pl.pallas_call(
        functools.partial(_qkprep_kernel, use_rope=use_rope),
        grid=(b, t // tm),
        in_specs=in_specs,
        out_specs=[
            pl.BlockSpec((1, tm, ATTN_WIDTH), lambda bb, i: (bb, i, 0)),
            pl.BlockSpec((1, tm, KV_WIDTH), lambda bb, i: (bb, i, 0)),
            pl.BlockSpec((1, ATTN_KV_HEADS * VT_ROWS, tm), lambda bb, i: (bb, 0, i)),
        ],
        out_shape=[
            jax.ShapeDtypeStruct((b, t, ATTN_WIDTH), BF16),
            jax.ShapeDtypeStruct((b, t, KV_WIDTH), BF16),
            jax.ShapeDtypeStruct((b, ATTN_KV_HEADS * VT_ROWS, t), BF16),
        ],
        compiler_params=_cparams(("parallel", "parallel")),
        name="qk_prep",
    )(*args)


def _attn_kernel(q_ref, k_ref, vt_ref, o_ref, s0_ref, s1_ref, acc_ref, *, tq, tk, nk):
    q = q_ref[0]
    qs = jnp.concatenate([q[:, r * HEAD_DIM:(r + 1) * HEAD_DIM] for r in range(ATTN_REP)], axis=0)
    nq = ATTN_REP * tq

    def scores(j):
        return lax.dot_general(k_ref[0, j * tk:(j + 1) * tk, :], qs, (((1,), (1,)), ((), ())),
                               preferred_element_type=F32)

    def update(j, s_ref, m):
        s = s_ref[...]
        m_new = jnp.maximum(m, jnp.max(s, axis=0, keepdims=True))
        alpha = jnp.exp2(m - m_new)
        p = jnp.exp2(s - m_new).astype(BF16)
        acc_ref[...] = alpha * acc_ref[...] + jnp.dot(vt_ref[0, :, j * tk:(j + 1) * tk], p,
                                                      preferred_element_type=F32)
        return m_new

    s_refs = (s0_ref, s1_ref)
    s0_ref[...] = scores(0)
    acc_ref[...] = jnp.zeros_like(acc_ref)
    m = jnp.full((1, nq), NEG_BIG, F32)
    for j in range(nk):
        if j + 1 < nk:
            s_refs[(j + 1) % 2][...] = scores(j + 1)
        m = update(j, s_refs[j % 2], m)
    o = acc_ref[:HEAD_DIM] / acc_ref[HEAD_DIM:HEAD_DIM + 1]
    for r in range(ATTN_REP):
        o_ref[0, :, r * HEAD_DIM:(r + 1) * HEAD_DIM] = o[:, r * tq:(r + 1) * tq].T.astype(BF16)


def _attention(q, k, vt, tq, tk):
    b, t, _ = q.shape
    tkv = k.shape[1]
    nk = tkv // tk
    gw = ATTN_REP * HEAD_DIM
    nq = ATTN_REP * tq
    return pl.pallas_call(
        functools.partial(_attn_kernel, tq=tq, tk=tk, nk=nk),
        grid=(b, ATTN_KV_HEADS, t // tq),
        in_specs=[
            pl.BlockSpec((1, tq, gw), lambda bb, g, i: (bb, i, g)),
            pl.BlockSpec((1, tkv, HEAD_DIM), lambda bb, g, i: (bb, 0, g)),
            pl.BlockSpec((1, VT_ROWS, tkv), lambda bb, g, i: (bb, g, 0)),
        ],
        out_specs=pl.BlockSpec((1, tq, gw), lambda bb, g, i: (bb, i, g)),
        out_shape=jax.ShapeDtypeStruct((b, t, ATTN_WIDTH), BF16),
        scratch_shapes=[pltpu.VMEM((tk, nq), F32), pltpu.VMEM((tk, nq), F32), pltpu.VMEM((VT_ROWS, nq), F32)],
        compiler_params=_cparams(("parallel", "parallel", "arbitrary")),
        name="attention",
    )(q, k, vt)


def _halo_specs(tm, t, width, col_block):
    per = tm // HALO
    last = t // HALO - 1
    return [
        pl.BlockSpec((1, tm, width), lambda bb, i: (bb, i, col_block)),
        pl.BlockSpec((1, HALO, width), lambda bb, i: (bb, jnp.maximum(i * per - 1, 0), col_block)),
        pl.BlockSpec((1, HALO, width), lambda bb, i: (bb, jnp.minimum((i + 1) * per, last), col_block)),
    ]


SUBLANES = 8


def _conv_rows(ext_ref, w_ref, taps, first_row, tm, width, out_fn, shift_ref=None, rows_per=64):
    if shift_ref is not None:
        n_shift = shift_ref.shape[1]
        for c0 in range(0, width, LANES):
            for r in range(1, SUBLANES):
                shift_ref[r - 1, :, c0:c0 + LANES] = ext_ref[r:r + n_shift, c0:c0 + LANES]
    for c0 in range(0, width, LANES):
        for r0 in range(0, tm, rows_per):
            acc = None
            for k in range(taps):
                lo = first_row + r0 + k
                res = lo % SUBLANES
                if shift_ref is None or res == 0:
                    rows = ext_ref[lo:lo + rows_per, c0:c0 + LANES]
                else:
                    rows = shift_ref[res - 1, lo - res:lo - res + rows_per, c0:c0 + LANES]
                term = rows * w_ref[k:k + 1, c0:c0 + LANES]
                acc = term if acc is None else acc + term
            out_fn(r0, rows_per, c0, acc)


def _conf_kernel(a_ref, ap_ref, an_ref, g_ref, gp_ref, gn_ref, cw_ref, cb_ref, lw_ref, lb_ref,
                 o_ref, u_ref, v_ref, us_ref, *, tm):
    i = pl.program_id(1)
    nt = pl.num_programs(1)
    u_ref[HALO:HALO + tm] = a_ref[0] * _sigmoid(g_ref[0])
    u_ref[0:HALO] = jnp.where(i > 0, ap_ref[0] * _sigmoid(gp_ref[0]), 0.0)
    u_ref[HALO + tm:] = jnp.where(i < nt - 1, an_ref[0] * _sigmoid(gn_ref[0]), 0.0)

    def put(r0, nr, c0, acc):
        v_ref[r0:r0 + nr, c0:c0 + LANES] = acc + cb_ref[:, c0:c0 + LANES]

    pad = (CONF_KERNEL - 1) // 2
    _conv_rows(u_ref, cw_ref, CONF_KERNEL, HALO - pad, tm, CONF_WIDTH, put, shift_ref=us_ref)
    v = v_ref[...]
    mu = jnp.mean(v, axis=-1, keepdims=True)
    d = v - mu
    var = jnp.mean(d * d, axis=-1, keepdims=True)
    y = d * lax.rsqrt(var + EPS) * lw_ref[...] + lb_ref[...]
    o_ref[0] = _silu(y).astype(BF16)


def _conformer(proj, cw, cb, lw, lb, tm):
    b, t, _ = proj.shape
    vec = pl.BlockSpec((1, CONF_WIDTH), lambda bb, i: (0, 0))
    return pl.pallas_call(
        functools.partial(_conf_kernel, tm=tm),
        grid=(b, t // tm),
        in_specs=_halo_specs(tm, t, CONF_WIDTH, COL_A // CONF_WIDTH)
        + _halo_specs(tm, t, CONF_WIDTH, COL_G // CONF_WIDTH)
        + [pl.BlockSpec((CONF_KERNEL, CONF_WIDTH), lambda bb, i: (0, 0)), vec, vec, vec],
        out_specs=pl.BlockSpec((1, tm, CONF_WIDTH), lambda bb, i: (bb, i, 0)),
        out_shape=jax.ShapeDtypeStruct((b, t, CONF_WIDTH), BF16),
        scratch_shapes=[
            pltpu.VMEM((tm + 2 * HALO, CONF_WIDTH), F32),
            pltpu.VMEM((tm, CONF_WIDTH), F32),
            pltpu.VMEM((SUBLANES - 1, tm + 2 * HALO - SUBLANES, CONF_WIDTH), F32),
        ],
        compiler_params=_cparams(("parallel", "parallel")),
        name="conformer",
    )(proj, proj, proj, proj, proj, proj, cw, cb, lw, lb)


def _ssdprep_kernel(x_ref, xp_ref, xn_ref, bc_ref, bcp_ref, bcn_ref, dt_ref, cwx_ref, cbx_ref,
                    cwb_ref, cbb_ref, dtb_ref, xo_ref, bco_ref, bmt_ref, dto_ref, xe_ref, be_ref, bv_ref, *, tm):
    i = pl.program_id(1)
    nt = pl.num_programs(1)
    xe_ref[HALO:HALO + tm] = x_ref[0]
    xe_ref[0:HALO] = jnp.where(i > 0, xp_ref[0], 0.0)
    xe_ref[HALO + tm:] = jnp.where(i < nt - 1, xn_ref[0], 0.0)
    be_ref[HALO:HALO + tm] = bc_ref[0]
    be_ref[0:HALO] = jnp.where(i > 0, bcp_ref[0], 0.0)
    be_ref[HALO + tm:] = jnp.where(i < nt - 1, bcn_ref[0], 0.0)
    pad = (SSM_CONV - 1) // 2

    def put_x(r0, nr, c0, acc):
        xo_ref[0, r0:r0 + nr, c0:c0 + LANES] = _silu(acc + cbx_ref[:, c0:c0 + LANES])

    def put_b(r0, nr, c0, acc):
        bv_ref[r0:r0 + nr, c0:c0 + LANES] = _silu(acc + cbb_ref[:, c0:c0 + LANES])

    _conv_rows(xe_ref, cwx_ref, SSM_CONV, HALO - pad, tm, SSM_WIDTH, put_x)
    _conv_rows(be_ref, cwb_ref, SSM_CONV, HALO - pad, tm, SSM_BC, put_b)
    bv = bv_ref[...]
    bco_ref[0] = bv.astype(BF16)
    bmt_ref[0] = bv[:, :SSM_GROUPS * SSM_STATE].T.astype(BF16)
    x = dt_ref[0] + dtb_ref[...]
    dto_ref[0] = jnp.maximum(x, 0.0) + jnp.log(1.0 + jnp.exp(-jnp.abs(x)))


def _ssdprep(proj, cwx, cbx, cwb, cbb, dtb, tm):
    b, t, _ = proj.shape
    gs = SSM_GROUPS * SSM_STATE
    return pl.pallas_call(
        functools.partial(_ssdprep_kernel, tm=tm),
        grid=(b, t // tm),
        in_specs=_halo_specs(tm, t, SSM_WIDTH, COL_XS // SSM_WIDTH)
        + _halo_specs(tm, t, SSM_BC, COL_BC // SSM_BC)
        + [
            pl.BlockSpec((1, tm, LANES), lambda bb, i: (bb, i, COL_DT // LANES)),
            pl.BlockSpec((SSM_CONV, SSM_WIDTH), lambda bb, i: (0, 0)),
            pl.BlockSpec((1, SSM_WIDTH), lambda bb, i: (0, 0)),
            pl.BlockSpec((SSM_CONV, SSM_BC), lambda bb, i: (0, 0)),
            pl.BlockSpec((1, SSM_BC), lambda bb, i: (0, 0)),
            pl.BlockSpec((1, LANES), lambda bb, i: (0, 0)),
        ],
        out_specs=[
            pl.BlockSpec((1, tm, SSM_WIDTH), lambda bb, i: (bb, i, 0)),
            pl.BlockSpec((1, tm, SSM_BC), lambda bb, i: (bb, i, 0)),
            pl.BlockSpec((1, gs, tm), lambda bb, i: (bb, 0, i)),
            pl.BlockSpec((1, tm, LANES), lambda bb, i: (bb, i, 0)),
        ],
        out_shape=[
            jax.ShapeDtypeStruct((b, t, SSM_WIDTH), F32),
            jax.ShapeDtypeStruct((b, t, SSM_BC), BF16),
            jax.ShapeDtypeStruct((b, gs, t), BF16),
            jax.ShapeDtypeStruct((b, t, LANES), F32),
        ],
        scratch_shapes=[
            pltpu.VMEM((tm + 2 * HALO, SSM_WIDTH), F32),
            pltpu.VMEM((tm + 2 * HALO, SSM_BC), F32),
            pltpu.VMEM((tm, SSM_BC), F32),
        ],
        compiler_params=_cparams(("parallel", "parallel")),
        name="ssd_prep",
    )(proj, proj, proj, proj, proj, proj, proj, cwx, cbx, cwb, cbb, dtb)


def _split_hi_lo(v):
    hi = v.astype(BF16)
    lo = (v - hi.astype(F32)).astype(BF16)
    return jnp.concatenate([hi, lo], axis=1)


def _ssd_kernel(*refs, direction, with_y, finish):
    it = iter(refs)
    xs_ref, bc_ref, bmt_ref, dt_ref, a_ref, e_ref, h0_ref = (next(it) for _ in range(7))
    if finish:
        yf_ref, z_ref, dsum_ref, nw_ref = (next(it) for _ in range(4))
    if with_y:
        y_ref = next(it)
    hfin_ref = next(it)
    st_ref = next(it)
    q = SSM_CHUNK
    c = pl.program_id(1)

    @pl.when(c == 0)
    def _():
        st_ref[...] = h0_ref[0]

    ii = lax.broadcasted_iota(jnp.int32, (q, q), 0)
    jj = lax.broadcasted_iota(jnp.int32, (q, q), 1)
    if direction == 0:
        valid, valid_t, end_row = jj <= ii, ii <= jj, q - 1
    else:
        valid, valid_t, end_row = jj >= ii, ii >= jj, 0
    dt = dt_ref[0]
    da = dt * a_ref[...]
    tri = jnp.where(valid, 1.0, 0.0)
    tri_t = jnp.where(valid_t, 1.0, 0.0)
    acum_c = jnp.dot(tri, da, precision=lax.Precision.HIGHEST, preferred_element_type=F32)
    acum_r = jnp.dot(da.T, tri_t, precision=lax.Precision.HIGHEST, preferred_element_type=F32)
    tot = acum_c[end_row:end_row + 1, :]
    to_end = jnp.exp(tot - acum_c)
    ea = jnp.exp(acum_c)

    e = e_ref[...]

    def expand(v):
        return jnp.dot(_split_hi_lo(v), e, preferred_element_type=F32)

    ea_e = expand(ea)
    xs = xs_ref[0]
    xw = (xs * expand(dt * to_end)).astype(BF16)
    cdec_e = ea_e[end_row:end_row + 1, :]
    bc = bc_ref[0]
    bmt = bmt_ref[0]
    if with_y:
        xdt = (xs * expand(dt)).astype(BF16)
        lane = lax.broadcasted_iota(jnp.int32, xdt.shape, 1)
        left = (lane % LANES) < SSM_HEAD_DIM
        zero = jnp.zeros_like(xdt)
        xdt_l = jnp.where(left, xdt, zero)
        xdt_r = jnp.where(left, zero, xdt)

    gw = SSM_GROUP_WIDTH
    hpg = SSM_HEADS // SSM_GROUPS
    for g in range(SSM_GROUPS):
        bm_t = bmt[g * SSM_STATE:(g + 1) * SSM_STATE, :]
        st = st_ref[g]
        if with_y:
            cm = bc[:, (SSM_GROUPS + g) * SSM_STATE:(SSM_GROUPS + g + 1) * SSM_STATE]
            bm = bc[:, g * SSM_STATE:(g + 1) * SSM_STATE]
            cb = lax.dot_general(cm, bm, (((1,), (1,)), ((), ())), preferred_element_type=F32)
            y_inter = jnp.dot(cm, st.astype(BF16), preferred_element_type=F32) * ea_e[:, g * gw:(g + 1) * gw]
            parts = []
            for k in range(hpg // 2):
                ms = []
                for r in (2 * k, 2 * k + 1):
                    hl = SSM_HEADS * direction + g * hpg + r
                    seg = acum_c[:, hl:hl + 1] - acum_r[hl:hl + 1, :]
                    ms.append((cb * jnp.where(valid, jnp.exp(seg), 0.0)).astype(BF16))
                c0 = g * gw + k * LANES
                rhs = jnp.concatenate([xdt_l[:, c0:c0 + LANES], xdt_r[:, c0:c0 + LANES]], axis=0)
                parts.append(jnp.dot(jnp.concatenate(ms, axis=1), rhs, preferred_element_type=F32))
            y_g = y_inter + jnp.concatenate(parts, axis=1)
            if finish:
                sl = slice(g * gw, (g + 1) * gw)
                y_ref[:, sl] = yf_ref[0, :, sl] + y_g + dsum_ref[:, sl] * xs[:, sl]
            else:
                y_ref[0, :, g * gw:(g + 1) * gw] = y_g
        st_ref[g] = st * cdec_e[:, g * gw:(g + 1) * gw] + jnp.dot(
            bm_t, xw[:, g * gw:(g + 1) * gw], preferred_element_type=F32)

    if finish:
        out_ref = next(it)
        gz = y_ref[...] * _silu(z_ref[0])
        var = jnp.mean(gz * gz, axis=-1, keepdims=True)
        out_ref[0] = (gz * lax.rsqrt(var + EPS) * nw_ref[...]).astype(BF16)

    @pl.when(c == pl.num_programs(1) - 1)
    def _():
        hfin_ref[0] = st_ref[...]


def _ssd_scan(xs, bc, bmt, dt, a_row, e_mat, h0, direction, with_y, fin=None):
    b, t, _ = xs.shape
    nc = t // SSM_CHUNK
    q = SSM_CHUNK
    gs = SSM_GROUPS * SSM_STATE
    finish = fin is not None
    if direction == 0:
        tmap = lambda bb, c: (bb, c, 0)
        tmap_t = lambda bb, c: (bb, 0, c)
        zmap = lambda bb, c: (bb, c, COL_Z // SSM_WIDTH)
    else:
        tmap = lambda bb, c: (bb, nc - 1 - c, 0)
        tmap_t = lambda bb, c: (bb, 0, nc - 1 - c)
        zmap = lambda bb, c: (bb, nc - 1 - c, COL_Z // SSM_WIDTH)
    st_spec = pl.BlockSpec((1, SSM_GROUPS, SSM_STATE, SSM_GROUP_WIDTH), lambda bb, c: (bb, 0, 0, 0))
    in_specs = [
        pl.BlockSpec((1, q, SSM_WIDTH), tmap),
        pl.BlockSpec((1, q, SSM_BC), tmap),
        pl.BlockSpec((1, gs, q), tmap_t),
        pl.BlockSpec((1, q, LANES), tmap),
        pl.BlockSpec((1, LANES), lambda bb, c: (0, 0)),
        pl.BlockSpec((2 * LANES, SSM_WIDTH), lambda bb, c: (0, 0)),
        st_spec,
    ]
    args = [xs, bc, bmt, dt, a_row, e_mat, h0]
    out_specs, out_shape, scratch = [], [], []
    if finish:
        y_fwd, proj, dsum_e, nw = fin
        in_specs += [
            pl.BlockSpec((1, q, SSM_WIDTH), tmap),
            pl.BlockSpec((1, q, SSM_WIDTH), zmap),
            pl.BlockSpec((1, SSM_WIDTH), lambda bb, c: (0, 0)),
            pl.BlockSpec((1, SSM_WIDTH), lambda bb, c: (0, 0)),
        ]
        args += [y_fwd, proj, dsum_e, nw]
    elif with_y:
        out_specs.append(pl.BlockSpec((1, q, SSM_WIDTH), tmap))
        out_shape.append(jax.ShapeDtypeStruct((b, t, SSM_WIDTH), F32))
    out_specs.append(st_spec)
    out_shape.append(jax.ShapeDtypeStruct((b, SSM_GROUPS, SSM_STATE, SSM_GROUP_WIDTH), F32))
    scratch.append(pltpu.VMEM((SSM_GROUPS, SSM_STATE, SSM_GROUP_WIDTH), F32))
    if finish:
        out_specs.append(pl.BlockSpec((1, q, SSM_WIDTH), tmap))
        out_shape.append(jax.ShapeDtypeStruct((b, t, SSM_WIDTH), BF16))
    return pl.pallas_call(
        functools.partial(_ssd_kernel_ordered, direction=direction, with_y=with_y, finish=finish),
        grid=(b, nc),
        in_specs=in_specs,
        out_specs=out_specs,
        out_shape=out_shape,
        scratch_shapes=scratch + ([pltpu.VMEM((q, SSM_WIDTH), F32)] if finish else []),
        compiler_params=_cparams(("parallel", "arbitrary")),
        name="ssd_scan",
    )(*args)


def _ssd_kernel_ordered(*refs, direction, with_y, finish):
    n_in = 11 if finish else 7
    ins = list(refs[:n_in])
    rest = list(refs[n_in:])
    if finish:
        hfin, out, st, ybuf = rest
        ordered = ins + [ybuf, hfin, st, out]
    elif with_y:
        y, hfin, st = rest
        ordered = ins + [y, hfin, st]
    else:
        hfin, st = rest
        ordered = ins + [hfin, st]
    _ssd_kernel(*ordered, direction=direction, with_y=with_y, finish=finish)


def _merge_kernel(x_ref, nw_ref, sh_ref, sc_ref, gt_ref, a_ref, b_ref, c_ref, wg0_ref, wg1_ref, wg2_ref,
                  wa_ref, wb_ref, wc_ref, wo_ref, o_ref, h_ref):
    j = pl.program_id(2)

    def contribution(rows, h):
        s = None
        for act_ref, wp_ref, wg_ref in ((a_ref, wa_ref, wg0_ref), (b_ref, wb_ref, wg1_ref), (c_ref, wc_ref, wg2_ref)):
            y = jnp.dot(act_ref[0, rows], wp_ref[...], preferred_element_type=F32)
            gate = _sigmoid(jnp.dot(h, wg_ref[...], preferred_element_type=F32))
            s = gate * y if s is None else s + gate * y
        return jnp.dot(s.astype(BF16), wo_ref[...], preferred_element_type=F32)

    @pl.when(j == 0)
    def _():
        for r in range(0, h_ref.shape[0], ROW_CHUNK):
            rows = slice(r, r + ROW_CHUNK)
            hc = _rms_mod(x_ref[0, rows], nw_ref[...], sh_ref[0], sc_ref[0]).astype(BF16)
            h_ref[rows] = hc
            o_ref[0, rows] = contribution(rows, hc)

    @pl.when(j > 0)
    def _():
        o_ref[0] += contribution(slice(None), h_ref[...])

    @pl.when(j == pl.num_programs(2) - 1)
    def _():
        o_ref[0] = x_ref[0] + gt_ref[0] * o_ref[0]


def _merge(x, nw, shift, scale, gate, act_a, act_b, act_c, wg, wa, wb, wc, wo, tm):
    b, t, d = x.shape
    tn = 512
    nj = d // tn
    vec = pl.BlockSpec((1, 1, d), lambda bb, i, j: (bb, 0, 0))
    act =pl.BlockSpec((1, tm, ATTN_WIDTH), lambda bb, i, j: (bb, i, 0))
    wproj = pl.BlockSpec((ATTN_WIDTH, tn), lambda bb, i, j: (0, j))
    return pl.pallas_call(
        _merge_kernel,
        grid=(b, t // tm, nj),
        in_specs=[
            pl.BlockSpec((1, tm, d), lambda bb, i, j: (bb, i, 0)),
            pl.BlockSpec((1, d), lambda bb, i, j: (0, 0)),
            vec, vec, vec, act, act, act,
            pl.BlockSpec((d, tn), lambda bb, i, j: (0, j)),
            pl.BlockSpec((d, tn), lambda bb, i, j: (0, nj + j)),
            pl.BlockSpec((d, tn), lambda bb, i, j: (0, 2 * nj + j)),
            wproj, wproj, wproj,
            pl.BlockSpec((tn, d), lambda bb, i, j: (j, 0)),
        ],
        out_specs=pl.BlockSpec((1, tm, d), lambda bb, i, j: (bb, i, 0)),
        out_shape=jax.ShapeDtypeStruct((b, t, d), F32),
        scratch_shapes=[pltpu.VMEM((tm, d), BF16)],
        compiler_params=_cparams(("parallel", "parallel", "arbitrary")),
        name="branch_merge",
    )(x, nw, shift, scale, gate, act_a, act_b, act_c, wg, wg, wg, wa, wb, wc, wo)


def _ffn_kernel(x_ref, xp_ref, xn_ref, nw_ref, sh_ref, sc_ref, gt_ref, wug_ref, wuv_ref, cwg_ref, cwv_ref,
                cbg_ref, cbv_ref, wd_ref, o_ref, h_ref, pg_ref, pv_ref, *, tm):
    i = pl.program_id(1)
    j = pl.program_id(2)
    rows = tm + 2 * HALO

    def conv(p, cw_ref, cb_ref):
        prev = pltpu.roll(p, 1, 0)[HALO:HALO + tm]
        nxt = pltpu.roll(p, rows - 1, 0)[HALO:HALO + tm]
        return prev * cw_ref[0:1, :] + p[HALO:HALO + tm] * cw_ref[1:2, :] + nxt * cw_ref[2:3, :] + cb_ref[...]

    def down(pg, pv):
        s = (_silu(conv(pg, cwg_ref, cbg_ref)) * conv(pv, cwv_ref, cbv_ref)).astype(BF16)
        return jnp.dot(s, wd_ref[...], preferred_element_type=F32)

    @pl.when(j == 0)
    def _():
        nw, sh, sc = nw_ref[...], sh_ref[0], sc_ref[0]
        for r in range(0, tm, ROW_CHUNK):
            hc = _rms_mod(x_ref[0, r:r + ROW_CHUNK], nw, sh, sc).astype(BF16)
            lo, hi = HALO + r, HALO + r + ROW_CHUNK
            if r == 0:
                hp = jnp.where(i > 0, _rms_mod(xp_ref[0], nw, sh, sc), 0.0).astype(BF16)
                hc, lo = jnp.concatenate([hp, hc], axis=0), 0
            if r + ROW_CHUNK == tm:
                hn = jnp.where(i < pl.num_programs(1) - 1, _rms_mod(xn_ref[0], nw, sh, sc), 0.0).astype(BF16)
                hc, hi = jnp.concatenate([hc, hn], axis=0), rows
            h_ref[lo:hi] = hc
            pg_ref[lo:hi] = jnp.dot(hc, wug_ref[...], preferred_element_type=F32)
            pv_ref[lo:hi] = jnp.dot(hc, wuv_ref[...], preferred_element_type=F32)
        o_ref[0] = down(pg_ref[...], pv_ref[...])

    @pl.when(j > 0)
    def _():
        h = h_ref[...]
        o_ref[0] += down(jnp.dot(h, wug_ref[...], preferred_element_type=F32),
                         jnp.dot(h, wuv_ref[...], preferred_element_type=F32))

    @pl.when(j == pl.num_programs(2) - 1)
    def _():
        o_ref[0] = x_ref[0] + gt_ref[0] * o_ref[0]


def _ffn(x, nw, shift, scale, gate, w_up, conv_w, conv_b, w_down, tm):
    b, t, d = x.shape
    dff = w_down.shape[0]
    tf = 512
    nj = dff // tf
    per = tm // HALO
    last = t // HALO - 1
    vec = pl.BlockSpec((1, 1, d), lambda bb, i, j: (bb, 0, 0))
    return pl.pallas_call(
        functools.partial(_ffn_kernel, tm=tm),
        grid=(b, t // tm, nj),
        in_specs=[
            pl.BlockSpec((1, tm, d), lambda bb, i, j: (bb, i, 0), pipeline_mode=pl.Buffered(1)),
            pl.BlockSpec((1, HALO, d), lambda bb, i, j: (bb, jnp.maximum(i * per - 1, 0), 0)),
            pl.BlockSpec((1, HALO, d), lambda bb, i, j: (bb, jnp.minimum((i + 1) * per, last), 0)),
            pl.BlockSpec((1, d), lambda bb, i, j: (0, 0)),
            vec, vec, vec,
            pl.BlockSpec((d, tf), lambda bb, i, j: (0, j)),
            pl.BlockSpec((d, tf), lambda bb, i, j: (0, nj + j)),
            pl.BlockSpec((FFN_CONV, tf), lambda bb, i, j: (0, j)),
            pl.BlockSpec((FFN_CONV, tf), lambda bb, i, j: (0, nj + j)),
            pl.BlockSpec((1, tf), lambda bb, i, j: (0, j)),
            pl.BlockSpec((1, tf), lambda bb, i, j: (0, nj + j)),
            pl.BlockSpec((tf, d), lambda bb, i, j: (j, 0)),
        ],
        out_specs=pl.BlockSpec((1, tm, d), lambda bb, i, j: (bb, i, 0), pipeline_mode=pl.Buffered(1)),
        out_shape=jax.ShapeDtypeStruct((b, t, d), F32),
        scratch_shapes=[
            pltpu.VMEM((tm + 2 * HALO, d), BF16),
            pltpu.VMEM((tm + 2 * HALO, tf), F32),
            pltpu.VMEM((tm + 2 * HALO, tf), F32),
        ],
        compiler_params=_cparams(("parallel", "parallel", "arbitrary")),
        name="conv_ffn",
    )(x, x, x, nw, shift, scale, gate, w_up, w_up, conv_w, conv_w, conv_b, conv_b, w_down)


def _final_norm_kernel(x_ref, w_ref, o_ref):
    x = x_ref[0]
    var = jnp.mean(x * x, axis=-1, keepdims=True)
    o_ref[0] = x * lax.rsqrt(var + EPS) * w_ref[...]


def _final_norm(x, w, tm):
    b, t, d = x.shape
    return pl.pallas_call(
        _final_norm_kernel,
        grid=(b, t // tm),
        in_specs=[pl.BlockSpec((1, tm, d), lambda bb, i: (bb, i, 0)), pl.BlockSpec((1, d), lambda bb, i: (0, 0))],
        out_specs=pl.BlockSpec((1, tm, d), lambda bb, i: (bb, i, 0)),
        out_shape=jax.ShapeDtypeStruct((b, t, d), F32),
        compiler_params=_cparams(("parallel", "parallel")),
        name="final_norm",
    )(x, w)


def _rope_tables(n_tokens):
    rows = n_tokens // GRID_W
    row = jnp.repeat(jnp.arange(rows), GRID_W).astype(F32)
    col = jnp.tile(jnp.arange(GRID_W), rows).astype(F32)
    inv = ROPE_THETA ** (-jnp.arange(0, ROPE_AXIS_DIM, 2, dtype=F32) / ROPE_AXIS_DIM)
    ar, ac = row[:, None] * inv, col[:, None] * inv
    cos = jnp.concatenate([jnp.cos(ar), jnp.cos(ar), jnp.cos(ac), jnp.cos(ac)], axis=1)
    sin = jnp.concatenate([-jnp.sin(ar), jnp.sin(ar), -jnp.sin(ac), jnp.sin(ac)], axis=1)
    return cos, sin


def _expansion_matrix(direction):
    head_of_col = jnp.arange(SSM_WIDTH) // SSM_HEAD_DIM
    row = jnp.arange(LANES)[:, None]
    e = (row == head_of_col[None, :] + SSM_HEADS * direction).astype(BF16)
    return jnp.concatenate([e, e], axis=0)


def _pad_lanes(v, offset):
    return jnp.zeros((1, LANES), F32).at[0, offset:offset + v.shape[0]].set(v.astype(F32))


def kernel(x, c, ctx, c_ctx, w_mod, b_mod, norm_mix_w, norm_ffn_w, w_in, q_norm_w, k_norm_w, w_attn_o, conf_conv_w, conf_conv_b, conf_ln_w, conf_ln_b, w_conf_o, ssm_conv_w, ssm_conv_b, ssm_a_log, ssm_dt_bias, ssm_d, ssm_norm_w, w_ssm_o, w_out, ffn_w_up, ffn_conv_w, ffn_conv_b, ffn_w_down, final_norm_w):
    bsz, n_lat, d = x.shape
    n_ctx = ctx.shape[1]
    depth = w_mod.shape[0]
    assert bsz <= 7 and n_lat % 512 == 0 and n_ctx % 256 == 0

    q_end = ATTN_WIDTH
    v_end = q_end + 2 * KV_WIDTH
    conf_end = v_end + 2 * CONF_WIDTH
    z_end = conf_end + SSM_WIDTH
    xbc_end = z_end + SSM_WIDTH + SSM_BC
    dt_end = xbc_end + 2 * SSM_HEADS

    cc = jnp.zeros((8, d), F32).at[:bsz].set(c).at[bsz].set(c_ctx)
    mods = _mod_vectors(cc, w_mod, b_mod).reshape(depth, 8, N_MOD, d)
    rope = _rope_tables(n_lat)
    e_mats = [_expansion_matrix(0), _expansion_matrix(1)]
    zero_state = jnp.zeros((bsz, SSM_GROUPS, SSM_STATE, SSM_GROUP_WIDTH), F32)

    for l in range(depth):
        need_ctx = l < depth - 1
        wi = w_in[l]
        w_mix = jnp.concatenate([
            wi[:, :q_end], wi[:, v_end:conf_end], wi[:, conf_end:z_end], wi[:, z_end:z_end + SSM_WIDTH],
            wi[:, q_end:v_end], wi[:, z_end + SSM_WIDTH:xbc_end], wi[:, xbc_end:dt_end],
            jnp.zeros((d, MIX_WIDTH - dt_end), F32)], axis=1).astype(BF16)
        w_gate = wi[:, dt_end:].astype(BF16)
        wa, wb, wc, wo = (w[l].astype(BF16) for w in (w_attn_o, w_conf_o, w_ssm_o, w_out))
        w_up, w_down = ffn_w_up[l].astype(BF16), ffn_w_down[l].astype(BF16)
        row = lambda v: v.reshape(1, -1)
        nmix, nffn = row(norm_mix_w[l]), row(norm_ffn_w[l])
        cwx, cwb = ssm_conv_w[l][:, :SSM_WIDTH], ssm_conv_w[l][:, SSM_WIDTH:]
        cbx, cbb = row(ssm_conv_b[l][:SSM_WIDTH]), row(ssm_conv_b[l][SSM_WIDTH:])
        a_coef = -jnp.exp(ssm_a_log[l].astype(F32))
        a_rows = [_pad_lanes(a_coef[0], 0), _pad_lanes(a_coef[1], SSM_HEADS)]
        dtb = _pad_lanes(ssm_dt_bias[l].reshape(-1), 0)
        dsum_e = row(jnp.repeat(ssm_d[l][0] + ssm_d[l][1], SSM_HEAD_DIM))
        m_lat = [mods[l, :bsz, k].reshape(bsz, 1, d) for k in range(N_MOD)]
        m_ctx = [jnp.broadcast_to(mods[l, bsz, k].reshape(1, 1, d), (bsz, 1, d)) for k in range(N_MOD)]

        proj_l = _inproj(x, nmix, m_lat[0], m_lat[1], w_mix, 1024 if n_lat % 1024 == 0 else 512)
        proj_c = _inproj(ctx, nmix, m_ctx[0], m_ctx[1], w_mix, 256)

        q_l, k_l, v_l = _qkprep(proj_l, row(q_norm_w[l]), row(k_norm_w[l]), rope, 256)
        q_c, k_c, v_c = _qkprep(proj_c, row(q_norm_w[l]), row(k_norm_w[l]), None, 256)
        act_a_l = _attention(q_l, jnp.concatenate([k_c, k_l], axis=1), jnp.concatenate([v_c, v_l], axis=2), 256, 256)
        conf_args = (conf_conv_w[l], row(conf_conv_b[l]), row(conf_ln_w[l]), row(conf_ln_b[l]))
        act_b_l = _conformer(proj_l, *conf_args, 256)

        prep_l = _ssdprep(proj_l, cwx, cbx, cwb, cbb, dtb, 256)
        prep_c = _ssdprep(proj_c, cwx, cbx, cwb, cbb, dtb, 256)
        fin_args = (dsum_e, row(ssm_norm_w[l]))
        if need_ctx:
            y_cf, h_cf = _ssd_scan(*prep_c, a_rows[0], e_mats[0], zero_state, 0, True)
            h_cb, act_c_c = _ssd_scan(*prep_c, a_rows[1], e_mats[1], zero_state, 1, True, (y_cf, proj_c) + fin_args)
        else:
            (h_cf,) = _ssd_scan(*prep_c, a_rows[0], e_mats[0], zero_state, 0, False)
            (h_cb,) = _ssd_scan(*prep_c, a_rows[1], e_mats[1], zero_state, 1, False)
        y_lf, _ = _ssd_scan(*prep_l, a_rows[0], e_mats[0], h_cf, 0, True)
        _, act_c_l = _ssd_scan(*prep_l, a_rows[1], e_mats[1], h_cb, 1, True, (y_lf, proj_l) + fin_args)

        x = _merge(x, nmix, m_lat[0], m_lat[1], m_lat[2], act_a_l, act_b_l, act_c_l, w_gate, wa, wb, wc, wo, 512)
        ffn_args = (w_up, ffn_conv_w[l], row(ffn_conv_b[l]), w_down)
        x = _ffn(x, nffn, m_lat[3], m_lat[4], m_lat[5], *ffn_args, 1024 if n_lat % 1024 == 0 else 512)
        if need_ctx:
            act_a_c = _attention(q_c, k_c, v_c, 256, 256)
            act_b_c = _conformer(proj_c, *conf_args, 256)
            ctx = _merge(ctx, nmix, m_ctx[0], m_ctx[1], m_ctx[2], act_a_c, act_b_c, act_c_c, w_gate, wa, wb, wc, wo, 256)
            ctx = _ffn(ctx, nffn, m_ctx[3], m_ctx[4], m_ctx[5], *ffn_args, 256)
    return _final_norm(x, final_norm_w.reshape(1, d), 512)
```

```python
import functools
import math

import jax
import jax.numpy as jnp
from jax import lax
from jax.experimental import pallas as pl
from jax.experimental.pallas import tpu as pltpu

F32 = jnp.float32
BF16 = jnp.bfloat16

EPS = 1e-6
N_MOD = 6
GRID_W = 64
HEAD_DIM = 128
ATTN_HEADS = 8
ATTN_KV_HEADS = 2
ATTN_REP = ATTN_HEADS // ATTN_KV_HEADS
ATTN_WIDTH = ATTN_HEADS * HEAD_DIM
KV_WIDTH = ATTN_KV_HEADS * HEAD_DIM
ROPE_THETA = 10000.0
ROPE_AXIS_DIM = HEAD_DIM // 2
ROPE_FREQS = ROPE_AXIS_DIM // 2
CONF_WIDTH = 1024
CONF_KERNEL = 31
SSM_HEADS = 16
SSM_HEAD_DIM = 64
SSM_WIDTH = SSM_HEADS * SSM_HEAD_DIM
SSM_GROUPS = 2
SSM_STATE = 128
SSM_GROUP_WIDTH = SSM_WIDTH // SSM_GROUPS
SSM_BC = 2 * SSM_GROUPS * SSM_STATE
SSM_CONV = 5
SSM_CHUNK = 128
FFN_CONV = 3
N_BRANCHES = 3

LANES = 128
HALO = 16
VMEM_LIMIT = 56 * 1024 * 1024
LOG2E = 1.4426950408889634
ATTN_QSCALE = HEAD_DIM ** -0.5 * LOG2E
VT_PAD = 16
VT_ROWS = HEAD_DIM + VT_PAD
NEG_BIG = -1e30

COL_Q = 0
COL_A = 1024
COL_G = 2048
COL_Z = 3072
COL_XS = 4096
COL_KV = 5120
COL_BC = 5632
COL_DT = 6144
MIX_WIDTH = 6272


def _cparams(sem):
    return pltpu.CompilerParams(dimension_semantics=sem, vmem_limit_bytes=VMEM_LIMIT)


def _sigmoid(x):
    return 1.0 / (1.0 + jnp.exp(-x))


def _silu(x):
    return x * _sigmoid(x)


def _rms_mod(x, nw, shift, scale):
    var = jnp.mean(x * x, axis=-1, keepdims=True)
    y = x * lax.rsqrt(var + EPS) * nw
    return y * (1.0 + scale) + shift


def _mod_kernel(c_ref, w_ref, b_ref, o_ref):
    s = _silu(c_ref[...]).astype(BF16)
    o_ref[0] = jnp.dot(s, w_ref[0].astype(BF16), preferred_element_type=F32) + b_ref[0]


def _mod_vectors(cc, w_mod, b_mod):
    depth, d, nm = w_mod.shape
    tn = 1024
    return pl.pallas_call(
        _mod_kernel,
        grid=(depth, nm // tn),
        in_specs=[
            pl.BlockSpec((8, d), lambda l, j: (0, 0)),
            pl.BlockSpec((1, d, tn), lambda l, j: (l, 0, j)),
            pl.BlockSpec((1, 1, tn), lambda l, j: (l, 0, j)),
        ],
        out_specs=pl.BlockSpec((1, 8, tn), lambda l, j: (l, 0, j)),
        out_shape=jax.ShapeDtypeStruct((depth, 8, nm), F32),
        compiler_params=_cparams(("parallel", "parallel")),
        name="mod_vectors",
    )(cc, w_mod, b_mod.reshape(depth, 1, nm))


ROW_CHUNK = 256


def _inproj_kernel(x_ref, nw_ref, sh_ref, sc_ref, w_ref, o_ref, h_ref):
    j = pl.program_id(2)

    @pl.when(j == 0)
    def _():
        for r in range(0, h_ref.shape[0], ROW_CHUNK):
            rows = slice(r, r + ROW_CHUNK)
            hc = _rms_mod(x_ref[0, rows], nw_ref[...], sh_ref[0], sc_ref[0]).astype(BF16)
            h_ref[rows] = hc
            o_ref[0, rows] = jnp.dot(hc, w_ref[...], preferred_element_type=F32)

    @pl.when(j > 0)
    def _():
        o_ref[0] = jnp.dot(h_ref[...], w_ref[...], preferred_element_type=F32)


def _inproj(x, nw, shift, scale, w, tm):
    b, t, d = x.shape
    n = w.shape[1]
    tn = 896
    return pl.pallas_call(
        _inproj_kernel,
        grid=(b, t // tm, n // tn),
        in_specs=[
            pl.BlockSpec((1, tm, d), lambda bb, i, j: (bb, i, 0)),
            pl.BlockSpec((1, d), lambda bb, i, j: (0, 0)),
            pl.BlockSpec((1, 1, d), lambda bb, i, j: (bb, 0, 0)),
            pl.BlockSpec((1, 1, d), lambda bb, i, j: (bb, 0, 0)),
            pl.BlockSpec((d, tn), lambda bb, i, j: (0, j)),
        ],
        out_specs=pl.BlockSpec((1, tm, tn), lambda bb, i, j: (bb, i, j)),
        out_shape=jax.ShapeDtypeStruct((b, t, n), F32),
        scratch_shapes=[pltpu.VMEM((tm, d), BF16)],
        compiler_params=_cparams(("parallel", "parallel", "arbitrary")),
        name="mixer_inproj",
    )(x, nw, shift, scale, w)


def _head_norm(x, w):
    var = jnp.mean(x * x, axis=-1, keepdims=True)
    return x * lax.rsqrt(var + EPS) * w


def _rope(x, c, s):
    lane = lax.broadcasted_iota(jnp.int32, x.shape, 1)
    first = (lane % ROPE_AXIS_DIM) < ROPE_FREQS
    partner = jnp.where(first, pltpu.roll(x, HEAD_DIM - ROPE_FREQS, 1), pltpu.roll(x, ROPE_FREQS, 1))
    return x * c + partner * s


def _qkprep_kernel(*refs, use_rope):
    if use_rope:
        q_ref, kv_ref, qw_ref, kw_ref, c_ref, s_ref, qo_ref, ko_ref, vo_ref = refs
    else:
        q_ref, kv_ref, qw_ref, kw_ref, qo_ref, ko_ref, vo_ref = refs
    q = q_ref[0]
    kv = kv_ref[0]
    for h in range(ATTN_HEADS):
        y = _head_norm(q[:, h * HEAD_DIM:(h + 1) * HEAD_DIM], qw_ref[...])
        if use_rope:
            y = _rope(y, c_ref[...], s_ref[...])
        qo_ref[0, h * HEAD_DIM:(h + 1) * HEAD_DIM, :] = (y * ATTN_QSCALE).T.astype(BF16)
    for h in range(ATTN_KV_HEADS):
        y = _head_norm(kv[:, h * HEAD_DIM:(h + 1) * HEAD_DIM], kw_ref[...])
        if use_rope:
            y = _rope(y, c_ref[...], s_ref[...])
        ko_ref[0, :, h * HEAD_DIM:(h + 1) * HEAD_DIM] = y.astype(BF16)
    vt = kv[:, KV_WIDTH:].T.astype(BF16)
    ones = jnp.ones((VT_PAD, vt.shape[1]), BF16)
    for g in range(ATTN_KV_HEADS):
        vo_ref[0, g * VT_ROWS:g * VT_ROWS + HEAD_DIM] = vt[g * HEAD_DIM:(g + 1) * HEAD_DIM]
        vo_ref[0, g * VT_ROWS + HEAD_DIM:(g + 1) * VT_ROWS] = ones


def _qkprep(proj, qw, kw, rope, tm):
    b, t, _ = proj.shape
    use_rope = rope is not None
    in_specs = [
        pl.BlockSpec((1, tm, ATTN_WIDTH), lambda bb, i: (bb, i, COL_Q // ATTN_WIDTH)),
        pl.BlockSpec((1, tm, 2 * KV_WIDTH), lambda bb, i: (bb, i, COL_KV // (2 * KV_WIDTH))),
        pl.BlockSpec((1, HEAD_DIM), lambda bb, i: (0, 0)),
        pl.BlockSpec((1, HEAD_DIM), lambda bb, i: (0, 0)),
    ]
    args = [proj, proj, qw, kw]
    if use_rope:
        in_specs += [pl.BlockSpec((tm, HEAD_DIM), lambda bb, i: (i, 0))] * 2
        args += list(rope)
    return pl.pallas_call(
        functools.partial(_qkprep_kernel, use_rope=use_rope),
        grid=(b, t // tm),
        in_specs=in_specs,
        out_specs=[
            pl.BlockSpec((1, ATTN_WIDTH, tm), lambda bb, i: (bb, 0, i)),
            pl.BlockSpec((1, tm, KV_WIDTH), lambda bb, i: (bb, i, 0)),
            pl.BlockSpec((1, ATTN_KV_HEADS * VT_ROWS, tm), lambda bb, i: (bb, 0, i)),
        ],
        out_shape=[
            jax.ShapeDtypeStruct((b, ATTN_WIDTH, t), BF16),
            jax.ShapeDtypeStruct((b, t, KV_WIDTH), BF16),
            jax.ShapeDtypeStruct((b, ATTN_KV_HEADS * VT_ROWS, t), BF16),
        ],
        compiler_params=_cparams(("parallel", "parallel")),
        name="qk_prep",
    )(*args)


def _attn_kernel(q_ref, k_ref, vt_ref, o_ref, s0_ref, s1_ref, acc_ref, *, tq, tk, nk):
    qt = jnp.concatenate([q_ref[0, r * HEAD_DIM:(r + 1) * HEAD_DIM, :] for r in range(ATTN_REP)], axis=1)
    nq = ATTN_REP * tq

    def scores(j):
        return jnp.dot(k_ref[0, j * tk:(j + 1) * tk, :], qt, preferred_element_type=F32)

    def update(j, s_ref, m):
        s = s_ref[...]
        m_new = jnp.maximum(m, jnp.max(s, axis=0, keepdims=True))
        alpha = jnp.exp2(m - m_new)
        p = jnp.exp2(s - m_new).astype(BF16)
        acc_ref[...] = alpha * acc_ref[...] + jnp.dot(vt_ref[0, :, j * tk:(j + 1) * tk], p,
                                                      preferred_element_type=F32)
        return m_new

    s_refs = (s0_ref, s1_ref)
    s0_ref[...] = scores(0)
    acc_ref[...] = jnp.zeros_like(acc_ref)
    m = jnp.full((1, nq), NEG_BIG, F32)
    for j in range(nk):
        if j + 1 < nk:
            s_refs[(j + 1) % 2][...] = scores(j + 1)
        m = update(j, s_refs[j % 2], m)
    o = acc_ref[:HEAD_DIM] / acc_ref[HEAD_DIM:HEAD_DIM + 1]
    for r in range(ATTN_REP):
        o_ref[0, :, r * HEAD_DIM:(r + 1) * HEAD_DIM] = o[:, r * tq:(r + 1) * tq].T.astype(BF16)


def _attention(qt, k, vt, tq, tk):
    b, _, t = qt.shape
    tkv = k.shape[1]
    nk = tkv // tk
    gw = ATTN_REP * HEAD_DIM
    nq = ATTN_REP * tq
    return pl.pallas_call(
        functools.partial(_attn_kernel, tq=tq, tk=tk, nk=nk),
        grid=(b, ATTN_KV_HEADS, t // tq),
        in_specs=[
            pl.BlockSpec((1, gw, tq), lambda bb, g, i: (bb, g, i)),
            pl.BlockSpec((1, tkv, HEAD_DIM), lambda bb, g, i: (bb, 0, g)),
            pl.BlockSpec((1, VT_ROWS, tkv), lambda bb, g, i: (bb, g, 0)),
        ],
        out_specs=pl.BlockSpec((1, tq, gw), lambda bb, g, i: (bb, i, g)),
        out_shape=jax.ShapeDtypeStruct((b, t, ATTN_WIDTH), BF16),
        scratch_shapes=[pltpu.VMEM((tk, nq), F32), pltpu.VMEM((tk, nq), F32), pltpu.VMEM((VT_ROWS, nq), F32)],
        compiler_params=_cparams(("parallel", "parallel", "arbitrary")),
        name="attention",
    )(qt, k, vt)


def _halo_specs(tm, t, width, col_block):
    per = tm // HALO
    last = t // HALO - 1
    return [
        pl.BlockSpec((1, tm, width), lambda bb, i: (bb, i, col_block)),
        pl.BlockSpec((1, HALO, width), lambda bb, i: (bb, jnp.maximum(i * per - 1, 0), col_block)),
        pl.BlockSpec((1, HALO, width), lambda bb, i: (bb, jnp.minimum((i + 1) * per, last), col_block)),
    ]


SUBLANES = 8


def _conv_rows(ext_ref, w_ref, taps, first_row, tm, width, out_fn, shift_ref=None, rows_per=64):
    if shift_ref is not None:
        n_shift = shift_ref.shape[1]
        for c0 in range(0, width, LANES):
            for r in range(1, SUBLANES):
                shift_ref[r - 1, :, c0:c0 + LANES] = ext_ref[r:r + n_shift, c0:c0 + LANES]
    for c0 in range(0, width, LANES):
        for r0 in range(0, tm, rows_per):
            acc = None
            for k in range(taps):
                lo = first_row + r0 + k
                res = lo % SUBLANES
                if shift_ref is None or res == 0:
                    rows = ext_ref[lo:lo + rows_per, c0:c0 + LANES]
                else:
                    rows = shift_ref[res - 1, lo - res:lo - res + rows_per, c0:c0 + LANES]
                term = rows * w_ref[k:k + 1, c0:c0 + LANES]
                acc = term if acc is None else acc + term
            out_fn(r0, rows_per, c0, acc)


def _conf_kernel(a_ref, ap_ref, an_ref, g_ref, gp_ref, gn_ref, cw_ref, cb_ref, lw_ref, lb_ref,
                 o_ref, u_ref, v_ref, us_ref, *, tm):
    i = pl.program_id(1)
    nt = pl.num_programs(1)
    u_ref[HALO:HALO + tm] = a_ref[0] * _sigmoid(g_ref[0])
    u_ref[0:HALO] = jnp.where(i > 0, ap_ref[0] * _sigmoid(gp_ref[0]), 0.0)
    u_ref[HALO + tm:] = jnp.where(i < nt - 1, an_ref[0] * _sigmoid(gn_ref[0]), 0.0)

    def put(r0, nr, c0, acc):
        v_ref[r0:r0 + nr, c0:c0 + LANES] = acc + cb_ref[:, c0:c0 + LANES]

    pad = (CONF_KERNEL - 1) // 2
    _conv_rows(u_ref, cw_ref, CONF_KERNEL, HALO - pad, tm, CONF_WIDTH, put, shift_ref=us_ref)
    v = v_ref[...]
    mu = jnp.mean(v, axis=-1, keepdims=True)
    d = v - mu
    var = jnp.mean(d * d, axis=-1, keepdims=True)
    y = d * lax.rsqrt(var + EPS) * lw_ref[...] + lb_ref[...]
    o_ref[0] = _silu(y).astype(BF16)


def _conformer(proj, cw, cb, lw, lb, tm):
    b, t, _ = proj.shape
    vec = pl.BlockSpec((1, CONF_WIDTH), lambda bb, i: (0, 0))
    return pl.pallas_call(
        functools.partial(_conf_kernel, tm=tm),
        grid=(b, t // tm),
        in_specs=_halo_specs(tm, t, CONF_WIDTH, COL_A // CONF_WIDTH)
        + _halo_specs(tm, t, CONF_WIDTH, COL_G // CONF_WIDTH)
        + [pl.BlockSpec((CONF_KERNEL, CONF_WIDTH), lambda bb, i: (0, 0)), vec, vec, vec],
        out_specs=pl.BlockSpec((1, tm, CONF_WIDTH), lambda bb, i: (bb, i, 0)),
        out_shape=jax.ShapeDtypeStruct((b, t, CONF_WIDTH), BF16),
        scratch_shapes=[
            pltpu.VMEM((tm + 2 * HALO, CONF_WIDTH), F32),
            pltpu.VMEM((tm, CONF_WIDTH), F32),
            pltpu.VMEM((SUBLANES - 1, tm + 2 * HALO - SUBLANES, CONF_WIDTH), F32),
        ],
        compiler_params=_cparams(("parallel", "parallel")),
        name="conformer",
    )(proj, proj, proj, proj, proj, proj, cw, cb, lw, lb)


def _ssdprep_kernel(x_ref, xp_ref, xn_ref, bc_ref, bcp_ref, bcn_ref, dt_ref, cwx_ref, cbx_ref,
                    cwb_ref, cbb_ref, dtb_ref, xo_ref, bco_ref, bmt_ref, dto_ref, xe_ref, be_ref, bv_ref, *, tm):
    i = pl.program_id(1)
    nt = pl.num_programs(1)
    xe_ref[HALO:HALO + tm] = x_ref[0]
    xe_ref[0:HALO] = jnp.where(i > 0, xp_ref[0], 0.0)
    xe_ref[HALO + tm:] = jnp.where(i < nt - 1, xn_ref[0], 0.0)
    be_ref[HALO:HALO + tm] = bc_ref[0]
    be_ref[0:HALO] = jnp.where(i > 0, bcp_ref[0], 0.0)
    be_ref[HALO + tm:] = jnp.where(i < nt - 1, bcn_ref[0], 0.0)
    pad = (SSM_CONV - 1) // 2

    def put_x(r0, nr, c0, acc):
        xo_ref[0, r0:r0 + nr, c0:c0 + LANES] = _silu(acc + cbx_ref[:, c0:c0 + LANES])

    def put_b(r0, nr, c0, acc):
        bv_ref[r0:r0 + nr, c0:c0 + LANES] = _silu(acc + cbb_ref[:, c0:c0 + LANES])

    _conv_rows(xe_ref, cwx_ref, SSM_CONV, HALO - pad, tm, SSM_WIDTH, put_x)
    _conv_rows(be_ref, cwb_ref, SSM_CONV, HALO - pad, tm, SSM_BC, put_b)
    bv = bv_ref[...]
    bco_ref[0] = bv.astype(BF16)
    bmt_ref[0] = bv[:, :SSM_GROUPS * SSM_STATE].T.astype(BF16)
    x = dt_ref[0] + dtb_ref[...]
    dto_ref[0] = jnp.maximum(x, 0.0) + jnp.log(1.0 + jnp.exp(-jnp.abs(x)))


def _ssdprep(proj, cwx, cbx, cwb, cbb, dtb, tm):
    b, t, _ = proj.shape
    gs = SSM_GROUPS * SSM_STATE
    return pl.pallas_call(
        functools.partial(_ssdprep_kernel, tm=tm),
        grid=(b, t // tm),
        in_specs=_halo_specs(tm, t, SSM_WIDTH, COL_XS // SSM_WIDTH)
        + _halo_specs(tm, t, SSM_BC, COL_BC // SSM_BC)
        + [
            pl.BlockSpec((1, tm, LANES), lambda bb, i: (bb, i, COL_DT // LANES)),
            pl.BlockSpec((SSM_CONV, SSM_WIDTH), lambda bb, i: (0, 0)),
            pl.BlockSpec((1, SSM_WIDTH), lambda bb, i: (0, 0)),
            pl.BlockSpec((SSM_CONV, SSM_BC), lambda bb, i: (0, 0)),
            pl.BlockSpec((1, SSM_BC), lambda bb, i: (0, 0)),
            pl.BlockSpec((1, LANES), lambda bb, i: (0, 0)),
        ],
        out_specs=[
            pl.BlockSpec((1, tm, SSM_WIDTH), lambda bb, i: (bb, i, 0)),
            pl.BlockSpec((1, tm, SSM_BC), lambda bb, i: (bb, i, 0)),
            pl.BlockSpec((1, gs, tm), lambda bb, i: (bb, 0, i)),
            pl.BlockSpec((1, tm, LANES), lambda bb, i: (bb, i, 0)),
        ],
        out_shape=[
            jax.ShapeDtypeStruct((b, t, SSM_WIDTH), F32),
            jax.ShapeDtypeStruct((b, t, SSM_BC), BF16),
            jax.ShapeDtypeStruct((b, gs, t), BF16),
            jax.ShapeDtypeStruct((b, t, LANES), F32),
        ],
        scratch_shapes=[
            pltpu.VMEM((tm + 2 * HALO, SSM_WIDTH), F32),
            pltpu.VMEM((tm + 2 * HALO, SSM_BC), F32),
            pltpu.VMEM((tm, SSM_BC), F32),
        ],
        compiler_params=_cparams(("parallel", "parallel")),
        name="ssd_prep",
    )(proj, proj, proj, proj, proj, proj, proj, cwx, cbx, cwb, cbb, dtb)


def _split_hi_lo(v):
    hi = v.astype(BF16)
    lo = (v - hi.astype(F32)).astype(BF16)
    return jnp.concatenate([hi, lo], axis=1)


def _ssd_kernel(*refs, direction, with_y, finish, nb):
    it = iter(refs)
    xs_ref, bc_ref, bmt_ref, dt_ref, a_ref, e_ref, h0_ref = (next(it) for _ in range(7))
    if finish:
        yf_ref, z_ref, dsum_ref, nw_ref = (next(it) for _ in range(4))
    if with_y:
        y_ref = next(it)
    hfin_ref = next(it)
    st_ref = next(it)
    if finish:
        out_ref = next(it)
    q = SSM_CHUNK
    c = pl.program_id(1)

    @pl.when(c == 0)
    def _():
        st_ref[...] = h0_ref[...]

    ii = lax.broadcasted_iota(jnp.int32, (q, q), 0)
    jj = lax.broadcasted_iota(jnp.int32, (q, q), 1)
    if direction == 0:
        valid, valid_t, end_row = jj <= ii, ii <= jj, q - 1
    else:
        valid, valid_t, end_row = jj >= ii, ii >= jj, 0
    tri = jnp.where(valid, 1.0, 0.0)
    tri_t = jnp.where(valid_t, 1.0, 0.0)
    e = e_ref[...]

    def expand(v):
        return jnp.dot(_split_hi_lo(v), e, preferred_element_type=F32)

    gw = SSM_GROUP_WIDTH
    hpg = SSM_HEADS // SSM_GROUPS
    for bi in range(nb):
        dt = dt_ref[bi]
        da = dt * a_ref[...]
        acum_c = jnp.dot(tri, da, precision=lax.Precision.HIGHEST, preferred_element_type=F32)
        acum_r = jnp.dot(da.T, tri_t, precision=lax.Precision.HIGHEST, preferred_element_type=F32)
        tot = acum_c[end_row:end_row + 1, :]
        to_end = jnp.exp(tot - acum_c)
        ea = jnp.exp(acum_c)
        ea_e = expand(ea)
        xs = xs_ref[bi]
        xw = (xs * expand(dt * to_end)).astype(BF16)
        cdec_e = ea_e[end_row:end_row + 1, :]
        bc = bc_ref[bi]
        bmt = bmt_ref[bi]
        if with_y:
            xdt = (xs * expand(dt)).astype(BF16)
            lane = lax.broadcasted_iota(jnp.int32, xdt.shape, 1)
            left = (lane % LANES) < SSM_HEAD_DIM
            zero = jnp.zeros_like(xdt)
            xdt_l = jnp.where(left, xdt, zero)
            xdt_r = jnp.where(left, zero, xdt)
        for g in range(SSM_GROUPS):
            bm_t = bmt[g * SSM_STATE:(g + 1) * SSM_STATE, :]
            st = st_ref[bi, g]
            if with_y:
                cm = bc[:, (SSM_GROUPS + g) * SSM_STATE:(SSM_GROUPS + g + 1) * SSM_STATE]
                bm = bc[:, g * SSM_STATE:(g + 1) * SSM_STATE]
                cb = lax.dot_general(cm, bm, (((1,), (1,)), ((), ())), preferred_element_type=F32)
                y_inter = jnp.dot(cm, st.astype(BF16), preferred_element_type=F32) * ea_e[:, g * gw:(g + 1) * gw]
                parts = []
                for k in range(hpg // 2):
                    ms = []
                    for r in (2 * k, 2 * k + 1):
                        hl = SSM_HEADS * direction + g * hpg + r
                        seg = acum_c[:, hl:hl + 1] - acum_r[hl:hl + 1, :]
                        ms.append((cb * jnp.where(valid, jnp.exp(seg), 0.0)).astype(BF16))
                    c0 = g * gw + k * LANES
                    rhs = jnp.concatenate([xdt_l[:, c0:c0 + LANES], xdt_r[:, c0:c0 + LANES]], axis=0)
                    parts.append(jnp.dot(jnp.concatenate(ms, axis=1), rhs, preferred_element_type=F32))
                y_g = y_inter + jnp.concatenate(parts, axis=1)
                sl = slice(g * gw, (g + 1) * gw)
                if finish:
                    y_ref[bi, :, sl] = yf_ref[bi, :, sl] + y_g + dsum_ref[:, sl] * xs[:, sl]
                else:
                    y_ref[bi, :, sl] = y_g
            st_ref[bi, g] = st * cdec_e[:, g * gw:(g + 1) * gw] + jnp.dot(
                bm_t, xw[:, g * gw:(g + 1) * gw], preferred_element_type=F32)
        if finish:
            gz = y_ref[bi] * _silu(z_ref[bi])
            var = jnp.mean(gz * gz, axis=-1, keepdims=True)
            out_ref[bi] = (gz * lax.rsqrt(var + EPS) * nw_ref[...]).astype(BF16)

    @pl.when(c == pl.num_programs(1) - 1)
    def _():
        hfin_ref[...] = st_ref[...]


def _ssd_scan(xs, bc, bmt, dt, a_row, e_mat, h0, direction, with_y, fin=None):
    b, t, _ = xs.shape
    nb = 2 if b % 2 == 0 else 1
    nc = t // SSM_CHUNK
    q = SSM_CHUNK
    gs = SSM_GROUPS * SSM_STATE
    finish = fin is not None
    if direction == 0:
        tmap = lambda bb, c: (bb, c, 0)
        tmap_t = lambda bb, c: (bb, 0, c)
        zmap = lambda bb, c: (bb, c, COL_Z // SSM_WIDTH)
    else:
        tmap = lambda bb, c: (bb, nc - 1 - c, 0)
        tmap_t = lambda bb, c: (bb, 0, nc - 1 - c)
        zmap = lambda bb, c: (bb, nc - 1 - c, COL_Z // SSM_WIDTH)
    st_spec = pl.BlockSpec((nb, SSM_GROUPS, SSM_STATE, SSM_GROUP_WIDTH), lambda bb, c: (bb, 0, 0, 0))
    in_specs = [
        pl.BlockSpec((nb, q, SSM_WIDTH), tmap),
        pl.BlockSpec((nb, q, SSM_BC), tmap),
        pl.BlockSpec((nb, gs, q), tmap_t),
        pl.BlockSpec((nb, q, LANES), tmap),
        pl.BlockSpec((1, LANES), lambda bb, c: (0, 0)),
        pl.BlockSpec((2 * LANES, SSM_WIDTH), lambda bb, c: (0, 0)),
        st_spec,
    ]
    args = [xs, bc, bmt, dt, a_row, e_mat, h0]
    out_specs, out_shape, scratch = [], [], []
    if finish:
        y_fwd, proj, dsum_e, nw = fin
        in_specs += [
            pl.BlockSpec((nb, q, SSM_WIDTH), tmap),
            pl.BlockSpec((nb, q, SSM_WIDTH), zmap),
            pl.BlockSpec((1, SSM_WIDTH), lambda bb, c: (0, 0)),
            pl.BlockSpec((1, SSM_WIDTH), lambda bb, c: (0, 0)),
        ]
        args += [y_fwd, proj, dsum_e, nw]
    elif with_y:
        out_specs.append(pl.BlockSpec((nb, q, SSM_WIDTH), tmap))
        out_shape.append(jax.ShapeDtypeStruct((b, t, SSM_WIDTH), F32))
    out_specs.append(st_spec)
    out_shape.append(jax.ShapeDtypeStruct((b, SSM_GROUPS, SSM_STATE, SSM_GROUP_WIDTH), F32))
    scratch.append(pltpu.VMEM((nb, SSM_GROUPS, SSM_STATE, SSM_GROUP_WIDTH), F32))
    if finish:
        out_specs.append(pl.BlockSpec((nb, q, SSM_WIDTH), tmap))
        out_shape.append(jax.ShapeDtypeStruct((b, t, SSM_WIDTH), BF16))
    return pl.pallas_call(
        functools.partial(_ssd_kernel_ordered, direction=direction, with_y=with_y, finish=finish, nb=nb),
        grid=(b // nb, nc),
        in_specs=in_specs,
        out_specs=out_specs,
        out_shape=out_shape,
        scratch_shapes=scratch + ([pltpu.VMEM((nb, q, SSM_WIDTH), F32)] if finish else []),
        compiler_params=_cparams(("parallel", "arbitrary")),
        name="ssd_scan",
    )(*args)


def _ssd_kernel_ordered(*refs, direction, with_y, finish, nb):
    n_in = 11 if finish else 7
    ins = list(refs[:n_in])
    rest = list(refs[n_in:])
    if finish:
        hfin, out, st, ybuf = rest
        ordered = ins + [ybuf, hfin, st, out]
    elif with_y:
        y, hfin, st = rest
        ordered = ins + [y, hfin, st]
    else:
        hfin, st = rest
        ordered = ins + [hfin, st]
    _ssd_kernel(*ordered, direction=direction, with_y=with_y, finish=finish, nb=nb)


def _merge_kernel(x_ref, nw_ref, sh_ref, sc_ref, gt_ref, a_ref, b_ref, c_ref, wg0_ref, wg1_ref, wg2_ref,
                  wa_ref, wb_ref, wc_ref, wo_ref, o_ref, h_ref):
    j = pl.program_id(2)

    def contribution(rows, h):
        s = None
        for act_ref, wp_ref, wg_ref in ((a_ref, wa_ref, wg0_ref), (b_ref, wb_ref, wg1_ref), (c_ref, wc_ref, wg2_ref)):
            y = jnp.dot(act_ref[0, rows], wp_ref[...], preferred_element_type=F32)
            gate = _sigmoid(jnp.dot(h, wg_ref[...], preferred_element_type=F32))
            s = gate * y if s is None else s + gate * y
        return jnp.dot(s.astype(BF16), wo_ref[...], preferred_element_type=F32)

    @pl.when(j == 0)
    def _():
        for r in range(0, h_ref.shape[0], ROW_CHUNK):
            rows = slice(r, r + ROW_CHUNK)
            hc = _rms_mod(x_ref[0, rows], nw_ref[...], sh_ref[0], sc_ref[0]).astype(BF16)
            h_ref[rows] = hc
            o_ref[0, rows] = contribution(rows, hc)

    @pl.when(j > 0)
    def _():
        o_ref[0] += contribution(slice(None), h_ref[...])

    @pl.when(j == pl.num_programs(2) - 1)
    def _():
        o_ref[0] = x_ref[0] + gt_ref[0] * o_ref[0]


def _merge(x, nw, shift, scale, gate, act_a, act_b, act_c, wg, wa, wb, wc, wo, tm):
    b, t, d = x.shape
    tn = 512
    nj = d // tn
    vec = pl.BlockSpec((1, 1, d), lambda bb, i, j: (bb, 0, 0))
    act =pl.BlockSpec((1, tm, ATTN_WIDTH), lambda bb, i, j: (bb, i, 0))
    wproj = pl.BlockSpec((ATTN_WIDTH, tn), lambda bb, i, j: (0, j))
    return pl.pallas_call(
        _merge_kernel,
        grid=(b, t // tm, nj),
        in_specs=[
            pl.BlockSpec((1, tm, d), lambda bb, i, j: (bb, i, 0)),
            pl.BlockSpec((1, d), lambda bb, i, j: (0, 0)),
            vec, vec, vec, act, act, act,
            pl.BlockSpec((d, tn), lambda bb, i, j: (0, j)),
            pl.BlockSpec((d, tn), lambda bb, i, j: (0, nj + j)),
            pl.BlockSpec((d, tn), lambda bb, i, j: (0, 2 * nj + j)),
            wproj, wproj, wproj,
            pl.BlockSpec((tn, d), lambda bb, i, j: (j, 0)),
        ],
        out_specs=pl.BlockSpec((1, tm, d), lambda bb, i, j: (bb, i, 0)),
        out_shape=jax.ShapeDtypeStruct((b, t, d), F32),
        scratch_shapes=[pltpu.VMEM((tm, d), BF16)],
        compiler_params=_cparams(("parallel", "parallel", "arbitrary")),
        name="branch_merge",
    )(x, nw, shift, scale, gate, act_a, act_b, act_c, wg, wg, wg, wa, wb, wc, wo)


def _ffn_kernel(x_ref, xp_ref, xn_ref, nw_ref, sh_ref, sc_ref, gt_ref, wug_ref, wuv_ref, cwg_ref, cwv_ref,
                cbg_ref, cbv_ref, wd_ref, o_ref, h_ref, pg_ref, pv_ref, *, tm):
    i = pl.program_id(1)
    j = pl.program_id(2)
    rows = tm + 2 * HALO

    def conv(p, cw_ref, cb_ref):
        prev = pltpu.roll(p, 1, 0)[HALO:HALO + tm]
        nxt = pltpu.roll(p, rows - 1, 0)[HALO:HALO + tm]
        return prev * cw_ref[0:1, :] + p[HALO:HALO + tm] * cw_ref[1:2, :] + nxt * cw_ref[2:3, :] + cb_ref[...]

    def down(pg, pv):
        s = (_silu(conv(pg, cwg_ref, cbg_ref)) * conv(pv, cwv_ref, cbv_ref)).astype(BF16)
        return jnp.dot(s, wd_ref[...], preferred_element_type=F32)

    @pl.when(j == 0)
    def _():
        nw, sh, sc = nw_ref[...], sh_ref[0], sc_ref[0]
        for r in range(0, tm, ROW_CHUNK):
            hc = _rms_mod(x_ref[0, r:r + ROW_CHUNK], nw, sh, sc).astype(BF16)
            lo, hi = HALO + r, HALO + r + ROW_CHUNK
            if r == 0:
                hp = jnp.where(i > 0, _rms_mod(xp_ref[0], nw, sh, sc), 0.0).astype(BF16)
                hc, lo = jnp.concatenate([hp, hc], axis=0), 0
            if r + ROW_CHUNK == tm:
                hn = jnp.where(i < pl.num_programs(1) - 1, _rms_mod(xn_ref[0], nw, sh, sc), 0.0).astype(BF16)
                hc, hi = jnp.concatenate([hc, hn], axis=0), rows
            h_ref[lo:hi] = hc
            pg_ref[lo:hi] = jnp.dot(hc, wug_ref[...], preferred_element_type=F32)
            pv_ref[lo:hi] = jnp.dot(hc, wuv_ref[...], preferred_element_type=F32)
        o_ref[0] = down(pg_ref[...], pv_ref[...])

    @pl.when(j > 0)
    def _():
        h = h_ref[...]
        o_ref[0] += down(jnp.dot(h, wug_ref[...], preferred_element_type=F32),
                         jnp.dot(h, wuv_ref[...], preferred_element_type=F32))

    @pl.when(j == pl.num_programs(2) - 1)
    def _():
        o_ref[0] = x_ref[0] + gt_ref[0] * o_ref[0]


def _ffn(x, nw, shift, scale, gate, w_up, conv_w, conv_b, w_down, tm):
    b, t, d = x.shape
    dff = w_down.shape[0]
    tf = 512
    nj = dff // tf
    per = tm // HALO
    last = t // HALO - 1
    vec = pl.BlockSpec((1, 1, d), lambda bb, i, j: (bb, 0, 0))
    return pl.pallas_call(
        functools.partial(_ffn_kernel, tm=tm),
        grid=(b, t // tm, nj),
        in_specs=[
            pl.BlockSpec((1, tm, d), lambda bb, i, j: (bb, i, 0)),
            pl.BlockSpec((1, HALO, d), lambda bb, i, j: (bb, jnp.maximum(i * per - 1, 0), 0)),
            pl.BlockSpec((1, HALO, d), lambda bb, i, j: (bb, jnp.minimum((i + 1) * per, last), 0)),
            pl.BlockSpec((1, d), lambda bb, i, j: (0, 0)),
            vec, vec, vec,
            pl.BlockSpec((d, tf), lambda bb, i, j: (0, j)),
            pl.BlockSpec((d, tf), lambda bb, i, j: (0, nj + j)),
            pl.BlockSpec((FFN_CONV, tf), lambda bb, i, j: (0, j)),
            pl.BlockSpec((FFN_CONV, tf), lambda bb, i, j: (0, nj + j)),
            pl.BlockSpec((1, tf), lambda bb, i, j: (0, j)),
            pl.BlockSpec((1, tf), lambda bb, i, j: (0, nj + j)),
            pl.BlockSpec((tf, d), lambda bb, i, j: (j, 0)),
        ],
        out_specs=pl.BlockSpec((1, tm, d), lambda bb, i, j: (bb, i, 0), pipeline_mode=pl.Buffered(1)),
        out_shape=jax.ShapeDtypeStruct((b, t, d), F32),
        scratch_shapes=[
            pltpu.VMEM((tm + 2 * HALO, d), BF16),
            pltpu.VMEM((tm + 2 * HALO, tf), F32),
            pltpu.VMEM((tm + 2 * HALO, tf), F32),
        ],
        compiler_params=_cparams(("parallel", "parallel", "arbitrary")),
        name="conv_ffn",
    )(x, x, x, nw, shift, scale, gate, w_up, w_up, conv_w, conv_w, conv_b, conv_b, w_down)


def _final_norm_kernel(x_ref, w_ref, o_ref):
    x = x_ref[0]
    var = jnp.mean(x * x, axis=-1, keepdims=True)
    o_ref[0] = x * lax.rsqrt(var + EPS) * w_ref[...]


def _final_norm(x, w, tm):
    b, t, d = x.shape
    return pl.pallas_call(
        _final_norm_kernel,
        grid=(b, t // tm),
        in_specs=[pl.BlockSpec((1, tm, d), lambda bb, i: (bb, i, 0)), pl.BlockSpec((1, d), lambda bb, i: (0, 0))],
        out_specs=pl.BlockSpec((1, tm, d), lambda bb, i: (bb, i, 0)),
        out_shape=jax.ShapeDtypeStruct((b, t, d), F32),
        compiler_params=_cparams(("parallel", "parallel")),
        name="final_norm",
    )(x, w)


def _rope_tables(n_tokens):
    rows = n_tokens // GRID_W
    row = jnp.repeat(jnp.arange(rows), GRID_W).astype(F32)
    col = jnp.tile(jnp.arange(GRID_W), rows).astype(F32)
    inv = ROPE_THETA ** (-jnp.arange(0, ROPE_AXIS_DIM, 2, dtype=F32) / ROPE_AXIS_DIM)
    ar, ac = row[:, None] * inv, col[:, None] * inv
    cos = jnp.concatenate([jnp.cos(ar), jnp.cos(ar), jnp.cos(ac), jnp.cos(ac)], axis=1)
    sin = jnp.concatenate([-jnp.sin(ar), jnp.sin(ar), -jnp.sin(ac), jnp.sin(ac)], axis=1)
    return cos, sin


def _expansion_matrix(direction):
    head_of_col = jnp.arange(SSM_WIDTH) // SSM_HEAD_DIM
    row = jnp.arange(LANES)[:, None]
    e = (row == head_of_col[None, :] + SSM_HEADS * direction).astype(BF16)
    return jnp.concatenate([e, e], axis=0)


def _pad_lanes(v, offset):
    return jnp.zeros((1, LANES), F32).at[0, offset:offset + v.shape[0]].set(v.astype(F32))


def kernel(x, c, ctx, c_ctx, w_mod, b_mod, norm_mix_w, norm_ffn_w, w_in, q_norm_w, k_norm_w, w_attn_o, conf_conv_w, conf_conv_b, conf_ln_w, conf_ln_b, w_conf_o, ssm_conv_w, ssm_conv_b, ssm_a_log, ssm_dt_bias, ssm_d, ssm_norm_w, w_ssm_o, w_out, ffn_w_up, ffn_conv_w, ffn_conv_b, ffn_w_down, final_norm_w):
    bsz, n_lat, d = x.shape
    n_ctx = ctx.shape[1]
    depth = w_mod.shape[0]
    assert bsz <= 7 and n_lat % 512 == 0 and n_ctx % 256 == 0

    q_end = ATTN_WIDTH
    v_end = q_end + 2 * KV_WIDTH
    conf_end = v_end + 2 * CONF_WIDTH
    z_end = conf_end + SSM_WIDTH
    xbc_end = z_end + SSM_WIDTH + SSM_BC
    dt_end = xbc_end + 2 * SSM_HEADS

    cc = jnp.zeros((8, d), F32).at[:bsz].set(c).at[bsz].set(c_ctx)
    mods = _mod_vectors(cc, w_mod, b_mod).reshape(depth, 8, N_MOD, d)
    rope = _rope_tables(n_lat)
    e_mats = [_expansion_matrix(0), _expansion_matrix(1)]
    zero_state = jnp.zeros((bsz, SSM_GROUPS, SSM_STATE, SSM_GROUP_WIDTH), F32)

    for l in range(depth):
        need_ctx = l < depth - 1
        wi = w_in[l]
        w_mix = jnp.concatenate([
            wi[:, :q_end], wi[:, v_end:conf_end], wi[:, conf_end:z_end], wi[:, z_end:z_end + SSM_WIDTH],
            wi[:, q_end:v_end], wi[:, z_end + SSM_WIDTH:xbc_end], wi[:, xbc_end:dt_end],
            jnp.zeros((d, MIX_WIDTH - dt_end), F32)], axis=1).astype(BF16)
        w_gate = wi[:, dt_end:].astype(BF16)
        wa, wb, wc, wo = (w[l].astype(BF16) for w in (w_attn_o, w_conf_o, w_ssm_o, w_out))
        w_up, w_down = ffn_w_up[l].astype(BF16), ffn_w_down[l].astype(BF16)
        row = lambda v: v.reshape(1, -1)
        nmix, nffn = row(norm_mix_w[l]), row(norm_ffn_w[l])
        cwx, cwb = ssm_conv_w[l][:, :SSM_WIDTH], ssm_conv_w[l][:, SSM_WIDTH:]
        cbx, cbb = row(ssm_conv_b[l][:SSM_WIDTH]), row(ssm_conv_b[l][SSM_WIDTH:])
        a_coef = -jnp.exp(ssm_a_log[l].astype(F32))
        a_rows = [_pad_lanes(a_coef[0], 0), _pad_lanes(a_coef[1], SSM_HEADS)]
        dtb = _pad_lanes(ssm_dt_bias[l].reshape(-1), 0)
        dsum_e = row(jnp.repeat(ssm_d[l][0] + ssm_d[l][1], SSM_HEAD_DIM))
        m_lat = [mods[l, :bsz, k].reshape(bsz, 1, d) for k in range(N_MOD)]
        m_ctx = [jnp.broadcast_to(mods[l, bsz, k].reshape(1, 1, d), (bsz, 1, d)) for k in range(N_MOD)]

        proj_l = _inproj(x, nmix, m_lat[0], m_lat[1], w_mix, 1024 if n_lat % 1024 == 0 else 512)
        proj_c = _inproj(ctx, nmix, m_ctx[0], m_ctx[1], w_mix, 256)

        q_l, k_l, v_l = _qkprep(proj_l, row(q_norm_w[l]), row(k_norm_w[l]), rope, 256)
        q_c, k_c, v_c = _qkprep(proj_c, row(q_norm_w[l]), row(k_norm_w[l]), None, 256)
        act_a_l = _attention(q_l, jnp.concatenate([k_c, k_l], axis=1), jnp.concatenate([v_c, v_l], axis=2), 256, 256)
        conf_args = (conf_conv_w[l], row(conf_conv_b[l]), row(conf_ln_w[l]), row(conf_ln_b[l]))
        act_b_l = _conformer(proj_l, *conf_args, 256)

        prep_l = _ssdprep(proj_l, cwx, cbx, cwb, cbb, dtb, 256)
        prep_c = _ssdprep(proj_c, cwx, cbx, cwb, cbb, dtb, 256)
        fin_args = (dsum_e, row(ssm_norm_w[l]))
        if need_ctx:
            y_cf, h_cf = _ssd_scan(*prep_c, a_rows[0], e_mats[0], zero_state, 0, True)
            h_cb, act_c_c = _ssd_scan(*prep_c, a_rows[1], e_mats[1], zero_state, 1, True, (y_cf, proj_c) + fin_args)
        else:
            (h_cf,) = _ssd_scan(*prep_c, a_rows[0], e_mats[0], zero_state, 0, False)
            (h_cb,) = _ssd_scan(*prep_c, a_rows[1], e_mats[1], zero_state, 1, False)
        y_lf, _ = _ssd_scan(*prep_l, a_rows[0], e_mats[0], h_cf, 0, True)
        _, act_c_l = _ssd_scan(*prep_l, a_rows[1], e_mats[1], h_cb, 1, True, (y_lf, proj_l) + fin_args)

        x = _merge(x, nmix, m_lat[0], m_lat[1], m_lat[2], act_a_l, act_b_l, act_c_l, w_gate, wa, wb, wc, wo, 512)
        ffn_args = (w_up, ffn_conv_w[l], row(ffn_conv_b[l]), w_down)
        x = _ffn(x, nffn, m_lat[3], m_lat[4], m_lat[5], *ffn_args, 1024 if n_lat % 1024 == 0 else 512)
        if need_ctx:
            act_a_c = _attention(q_c, k_c, v_c, 256, 256)
            act_b_c = _conformer(proj_c, *conf_args, 256)
            ctx = _merge(ctx, nmix, m_ctx[0], m_ctx[1], m_ctx[2], act_a_c, act_b_c, act_c_c, w_gate, wa, wb, wc, wo, 256)
            ctx = _ffn(ctx, nffn, m_ctx[3], m_ctx[4], m_ctx[5], *ffn_args, 256)
    return _final_norm(x, final_norm_w.reshape(1, d), 512)
```

```python
import functools
import math

import jax
import jax.numpy as jnp
from jax import lax
from jax.experimental import pallas as pl
from jax.experimental.pallas import tpu as pltpu

F32 = jnp.float32
BF16 = jnp.bfloat16

EPS = 1e-6
N_MOD = 6
GRID_W = 64
HEAD_DIM = 128
ATTN_HEADS = 8
ATTN_KV_HEADS = 2
ATTN_REP = ATTN_HEADS // ATTN_KV_HEADS
ATTN_WIDTH = ATTN_HEADS * HEAD_DIM
KV_WIDTH = ATTN_KV_HEADS * HEAD_DIM
ROPE_THETA = 10000.0
ROPE_AXIS_DIM = HEAD_DIM // 2
ROPE_FREQS = ROPE_AXIS_DIM // 2
CONF_WIDTH = 1024
CONF_KERNEL = 31
SSM_HEADS = 16
SSM_HEAD_DIM = 64
SSM_WIDTH = SSM_HEADS * SSM_HEAD_DIM
SSM_GROUPS = 2
SSM_STATE = 128
SSM_GROUP_WIDTH = SSM_WIDTH // SSM_GROUPS
SSM_BC = 2 * SSM_GROUPS * SSM_STATE
SSM_CONV = 5
SSM_CHUNK = 128
FFN_CONV = 3
N_BRANCHES = 3

LANES = 128
HALO = 16
VMEM_LIMIT = 56 * 1024 * 1024
LOG2E = 1.4426950408889634
ATTN_QSCALE = HEAD_DIM ** -0.5 * LOG2E
VT_PAD = 16
VT_ROWS = HEAD_DIM + VT_PAD
NEG_BIG = -1e30

COL_Q = 0
COL_A = 1024
COL_G = 2048
COL_Z = 3072
COL_XS = 4096
COL_KV = 5120
COL_BC = 5632
COL_DT = 6144
MIX_WIDTH = 6400


def _cparams(sem):
    return pltpu.CompilerParams(dimension_semantics=sem, vmem_limit_bytes=VMEM_LIMIT)


def _sigmoid(x):
    return 1.0 / (1.0 + jnp.exp(-x))


def _silu(x):
    return x * _sigmoid(x)


def _rms_mod(x, nw, shift, scale):
    var = jnp.mean(x * x, axis=-1, keepdims=True)
    y = x * lax.rsqrt(var + EPS) * nw
    return y * (1.0 + scale) + shift


def _mod_kernel(c_ref, w_ref, b_ref, o_ref):
    s = _silu(c_ref[...]).astype(BF16)
    o_ref[0] = jnp.dot(s, w_ref[0].astype(BF16), preferred_element_type=F32) + b_ref[0]


def _mod_vectors(cc, w_mod, b_mod):
    depth, d, nm = w_mod.shape
    tn = 2048
    return pl.pallas_call(
        _mod_kernel,
        grid=(depth, nm // tn),
        in_specs=[
            pl.BlockSpec((8, d), lambda l, j: (0, 0)),
            pl.BlockSpec((1, d, tn), lambda l, j: (l, 0, j)),
            pl.BlockSpec((1, 1, tn), lambda l, j: (l, 0, j)),
        ],
        out_specs=pl.BlockSpec((1, 8, tn), lambda l, j: (l, 0, j)),
        out_shape=jax.ShapeDtypeStruct((depth, 8, nm), F32),
        compiler_params=_cparams(("parallel", "parallel")),
        name="mod_vectors",
    )(cc, w_mod, b_mod.reshape(depth, 1, nm))


ROW_CHUNK = 256


def _inproj_kernel(x_ref, nw_ref, sh_ref, sc_ref, w_ref, o_ref, h_ref):
    j = pl.program_id(2)

    @pl.when(j == 0)
    def _():
        for r in range(0, h_ref.shape[0], ROW_CHUNK):
            rows = slice(r, r + ROW_CHUNK)
            hc = _rms_mod(x_ref[0, rows], nw_ref[...], sh_ref[0], sc_ref[0]).astype(BF16)
            h_ref[rows] = hc
            o_ref[0, rows] = jnp.dot(hc, w_ref[...], preferred_element_type=F32)

    @pl.when(j > 0)
    def _():
        o_ref[0] = jnp.dot(h_ref[...], w_ref[...], preferred_element_type=F32)


def _inproj(x, nw, shift, scale, w, tm):
    b, t, d = x.shape
    n = w.shape[1]
    tn = 1280
    return pl.pallas_call(
        _inproj_kernel,
        grid=(b, t // tm, n // tn),
        in_specs=[
            pl.BlockSpec((1, tm, d), lambda bb, i, j: (bb, i, 0)),
            pl.BlockSpec((1, d), lambda bb, i, j: (0, 0)),
            pl.BlockSpec((1, 1, d), lambda bb, i, j: (bb, 0, 0)),
            pl.BlockSpec((1, 1, d), lambda bb, i, j: (bb, 0, 0)),
            pl.BlockSpec((d, tn), lambda bb, i, j: (0, j)),
        ],
        out_specs=pl.BlockSpec((1, tm, tn), lambda bb, i, j: (bb, i, j)),
        out_shape=jax.ShapeDtypeStruct((b, t, n), F32),
        scratch_shapes=[pltpu.VMEM((tm, d), BF16)],
        compiler_params=_cparams(("parallel", "parallel", "arbitrary")),
        name="mixer_inproj",
    )(x, nw, shift, scale, w)


def _head_norm(x, w):
    var = jnp.mean(x * x, axis=-1, keepdims=True)
    return x * lax.rsqrt(var + EPS) * w


def _rope(x, c, s):
    lane = lax.broadcasted_iota(jnp.int32, x.shape, 1)
    first = (lane % ROPE_AXIS_DIM) < ROPE_FREQS
    partner = jnp.where(first, pltpu.roll(x, HEAD_DIM - ROPE_FREQS, 1), pltpu.roll(x, ROPE_FREQS, 1))
    return x * c + partner * s


def _qkprep_kernel(*refs, use_rope):
    if use_rope:
        q_ref, kv_ref, qw_ref, kw_ref, c_ref, s_ref, qo_ref, ko_ref, vo_ref = refs
    else:
        q_ref, kv_ref, qw_ref, kw_ref, qo_ref, ko_ref, vo_ref = refs
    q = q_ref[0]
    kv = kv_ref[0]
    for h in range(ATTN_HEADS):
        y = _head_norm(q[:, h * HEAD_DIM:(h + 1) * HEAD_DIM], qw_ref[...])
        if use_rope:
            y = _rope(y, c_ref[...], s_ref[...])
        qo_ref[0, :, h * HEAD_DIM:(h + 1) * HEAD_DIM] = (y * ATTN_QSCALE).astype(BF16)
    for h in range(ATTN_KV_HEADS):
        y = _head_norm(kv[:, h * HEAD_DIM:(h + 1) * HEAD_DIM], kw_ref[...])
        if use_rope:
            y = _rope(y, c_ref[...], s_ref[...])
        ko_ref[0, :, h * HEAD_DIM:(h + 1) * HEAD_DIM] = y.astype(BF16)
    vt = kv[:, KV_WIDTH:].T.astype(BF16)
    ones = jnp.ones((VT_PAD, vt.shape[1]), BF16)
    for g in range(ATTN_KV_HEADS):
        vo_ref[0, g * VT_ROWS:g * VT_ROWS + HEAD_DIM] = vt[g * HEAD_DIM:(g + 1) * HEAD_DIM]
        vo_ref[0, g * VT_ROWS + HEAD_DIM:(g + 1) * VT_ROWS] = ones


def _qkprep(proj, qw, kw, rope, tm):
    b, t, _ = proj.shape
    use_rope = rope is not None
    in_specs = [
        pl.BlockSpec((1, tm, ATTN_WIDTH), lambda bb, i: (bb, i, COL_Q // ATTN_WIDTH)),
        pl.BlockSpec((1, tm, 2 * KV_WIDTH), lambda bb, i: (bb, i, COL_KV // (2 * KV_WIDTH))),
        pl.BlockSpec((1, HEAD_DIM), lambda bb, i: (0, 0)),
        pl.BlockSpec((1, HEAD_DIM), lambda bb, i: (0, 0)),
    ]
    args = [proj, proj, qw, kw]
    if use_rope:
        in_specs += [pl.BlockSpec((tm, HEAD_DIM), lambda bb, i: (i, 0))] * 2
        args += list(rope)
    return pl.pallas_call(
        functools.partial(_qkprep_kernel, use_rope=use_rope),
        grid=(b, t // tm),
        in_specs=in_specs,
        out_specs=[
            pl.BlockSpec((1, tm, ATTN_WIDTH), lambda bb, i: (bb, i, 0)),
            pl.BlockSpec((1, tm, KV_WIDTH), lambda bb, i: (bb, i, 0)),
            pl.BlockSpec((1, ATTN_KV_HEADS * VT_ROWS, tm), lambda bb, i: (bb, 0, i)),
        ],
        out_shape=[
            jax.ShapeDtypeStruct((b, t, ATTN_WIDTH), BF16),
            jax.ShapeDtypeStruct((b, t, KV_WIDTH), BF16),
            jax.ShapeDtypeStruct((b, ATTN_KV_HEADS * VT_ROWS, t), BF16),
        ],
        compiler_params=_cparams(("parallel", "parallel")),
        name="qk_prep",
    )(*args)


def _attn_kernel(*refs, tq, tk, seg_chunks):
    n_seg = len(seg_chunks)
    q_ref = refs[0]
    kv_refs = refs[1:1 + 2 * n_seg]
    o_ref, s0_ref, s1_ref, acc_ref = refs[1 + 2 * n_seg:]
    q = q_ref[0]
    qs = jnp.concatenate([q[:, r * HEAD_DIM:(r + 1) * HEAD_DIM] for r in range(ATTN_REP)], axis=0)
    nq = ATTN_REP * tq
    chunks = [(kv_refs[2 * s], kv_refs[2 * s + 1], slice(c * tk, (c + 1) * tk))
              for s in range(n_seg) for c in range(seg_chunks[s])]

    def scores(j):
        k_ref, _, rows = chunks[j]
        return lax.dot_general(k_ref[0, rows, :], qs, (((1,), (1,)), ((), ())), preferred_element_type=F32)

    def update(j, s_ref, m):
        _, vt_ref, cols = chunks[j]
        s = s_ref[...]
        m_new = jnp.maximum(m, jnp.max(s, axis=0, keepdims=True))
        alpha = jnp.exp2(m - m_new)
        p = jnp.exp2(s - m_new).astype(BF16)
        acc_ref[...] = alpha * acc_ref[...] + jnp.dot(vt_ref[0, :, cols], p, preferred_element_type=F32)
        return m_new

    s_refs = (s0_ref, s1_ref)
    s0_ref[...] = scores(0)
    acc_ref[...] = jnp.zeros_like(acc_ref)
    m = jnp.full((1, nq), NEG_BIG, F32)
    for j in range(len(chunks)):
        if j + 1 < len(chunks):
            s_refs[(j + 1) % 2][...] = scores(j + 1)
        m = update(j, s_refs[j % 2], m)
    o = acc_ref[:HEAD_DIM] / acc_ref[HEAD_DIM:HEAD_DIM + 1]
    for r in range(ATTN_REP):
        o_ref[0, :, r * HEAD_DIM:(r + 1) * HEAD_DIM] = o[:, r * tq:(r + 1) * tq].T.astype(BF16)


def _attention(q, kv_segments, tq, tk):
    b, t, _ = q.shape
    gw = ATTN_REP * HEAD_DIM
    nq = ATTN_REP * tq
    in_specs = [pl.BlockSpec((1, tq, gw), lambda bb, g, i: (bb, i, g))]
    args = [q]
    for k, vt in kv_segments:
        tkv = k.shape[1]
        in_specs += [
            pl.BlockSpec((1, tkv, HEAD_DIM), lambda bb, g, i: (bb, 0, g)),
            pl.BlockSpec((1, VT_ROWS, tkv), lambda bb, g, i: (bb, g, 0)),
        ]
        args += [k, vt]
    return pl.pallas_call(
        functools.partial(_attn_kernel, tq=tq, tk=tk, seg_chunks=tuple(k.shape[1] // tk for k, _ in kv_segments)),
        grid=(b, ATTN_KV_HEADS, t // tq),
        in_specs=in_specs,
        out_specs=pl.BlockSpec((1, tq, gw), lambda bb, g, i: (bb, i, g)),
        out_shape=jax.ShapeDtypeStruct((b, t, ATTN_WIDTH), BF16),
        scratch_shapes=[pltpu.VMEM((tk, nq), F32), pltpu.VMEM((tk, nq), F32), pltpu.VMEM((VT_ROWS, nq), F32)],
        compiler_params=_cparams(("parallel", "parallel", "arbitrary")),
        name="attention",
    )(*args)


def _halo_specs(tm, t, width, col_block):
    per = tm // HALO
    last = t // HALO - 1
    return [
        pl.BlockSpec((1, tm, width), lambda bb, i: (bb, i, col_block)),
        pl.BlockSpec((1, HALO, width), lambda bb, i: (bb, jnp.maximum(i * per - 1, 0), col_block)),
        pl.BlockSpec((1, HALO, width), lambda bb, i: (bb, jnp.minimum((i + 1) * per, last), col_block)),
    ]


SUBLANES = 8


def _conv_rows(ext_ref, w_ref, taps, first_row, tm, width, out_fn, shift_ref=None, rows_per=64):
    if shift_ref is not None:
        n_shift = shift_ref.shape[1]
        for c0 in range(0, width, LANES):
            for r in range(1, SUBLANES):
                shift_ref[r - 1, :, c0:c0 + LANES] = ext_ref[r:r + n_shift, c0:c0 + LANES]
    for c0 in range(0, width, LANES):
        for r0 in range(0, tm, rows_per):
            acc = None
            for k in range(taps):
                lo = first_row + r0 + k
                res = lo % SUBLANES
                if shift_ref is None or res == 0:
                    rows = ext_ref[lo:lo + rows_per, c0:c0 + LANES]
                else:
                    rows = shift_ref[res - 1, lo - res:lo - res + rows_per, c0:c0 + LANES]
                term = rows * w_ref[k:k + 1, c0:c0 + LANES]
                acc = term if acc is None else acc + term
            out_fn(r0, rows_per, c0, acc)


def _conf_kernel(a_ref, ap_ref, an_ref, g_ref, gp_ref, gn_ref, cw_ref, cb_ref, lw_ref, lb_ref,
                 o_ref, u_ref, v_ref, us_ref, *, tm):
    i = pl.program_id(1)
    nt = pl.num_programs(1)
    u_ref[HALO:HALO + tm] = a_ref[0] * _sigmoid(g_ref[0])
    u_ref[0:HALO] = jnp.where(i > 0, ap_ref[0] * _sigmoid(gp_ref[0]), 0.0)
    u_ref[HALO + tm:] = jnp.where(i < nt - 1, an_ref[0] * _sigmoid(gn_ref[0]), 0.0)

    def put(r0, nr, c0, acc):
        v_ref[r0:r0 + nr, c0:c0 + LANES] = acc + cb_ref[:, c0:c0 + LANES]

    pad = (CONF_KERNEL - 1) // 2
    _conv_rows(u_ref, cw_ref, CONF_KERNEL, HALO - pad, tm, CONF_WIDTH, put, shift_ref=us_ref)
    v = v_ref[...]
    mu = jnp.mean(v, axis=-1, keepdims=True)
    d = v - mu
    var = jnp.mean(d * d, axis=-1, keepdims=True)
    y = d * lax.rsqrt(var + EPS) * lw_ref[...] + lb_ref[...]
    o_ref[0] = _silu(y).astype(BF16)


def _conformer(proj, cw, cb, lw, lb, tm):
    b, t, _ = proj.shape
    vec = pl.BlockSpec((1, CONF_WIDTH), lambda bb, i: (0, 0))
    return pl.pallas_call(
        functools.partial(_conf_kernel, tm=tm),
        grid=(b, t // tm),
        in_specs=_halo_specs(tm, t, CONF_WIDTH, COL_A // CONF_WIDTH)
        + _halo_specs(tm, t, CONF_WIDTH, COL_G // CONF_WIDTH)
        + [pl.BlockSpec((CONF_KERNEL, CONF_WIDTH), lambda bb, i: (0, 0)), vec, vec, vec],
        out_specs=pl.BlockSpec((1, tm, CONF_WIDTH), lambda bb, i: (bb, i, 0)),
        out_shape=jax.ShapeDtypeStruct((b, t, CONF_WIDTH), BF16),
        scratch_shapes=[
            pltpu.VMEM((tm + 2 * HALO, CONF_WIDTH), F32),
            pltpu.VMEM((tm, CONF_WIDTH), F32),
            pltpu.VMEM((SUBLANES - 1, tm + 2 * HALO - SUBLANES, CONF_WIDTH), F32),
        ],
        compiler_params=_cparams(("parallel", "parallel")),
        name="conformer",
    )(proj, proj, proj, proj, proj, proj, cw, cb, lw, lb)


def _ssdprep_kernel(x_ref, xp_ref, xn_ref, bc_ref, bcp_ref, bcn_ref, dt_ref, cwx_ref, cbx_ref,
                    cwb_ref, cbb_ref, dtb_ref, xo_ref, bco_ref, bmt_ref, dto_ref, xe_ref, be_ref, bv_ref, *, tm):
    i = pl.program_id(1)
    nt = pl.num_programs(1)
    xe_ref[HALO:HALO + tm] = x_ref[0]
    xe_ref[0:HALO] = jnp.where(i > 0, xp_ref[0], 0.0)
    xe_ref[HALO + tm:] = jnp.where(i < nt - 1, xn_ref[0], 0.0)
    be_ref[HALO:HALO + tm] = bc_ref[0]
    be_ref[0:HALO] = jnp.where(i > 0, bcp_ref[0], 0.0)
    be_ref[HALO + tm:] = jnp.where(i < nt - 1, bcn_ref[0], 0.0)
    pad = (SSM_CONV - 1) // 2

    def put_x(r0, nr, c0, acc):
        xo_ref[0, r0:r0 + nr, c0:c0 + LANES] = _silu(acc + cbx_ref[:, c0:c0 + LANES])

    def put_b(r0, nr, c0, acc):
        bv_ref[r0:r0 + nr, c0:c0 + LANES] = _silu(acc + cbb_ref[:, c0:c0 + LANES])

    _conv_rows(xe_ref, cwx_ref, SSM_CONV, HALO - pad, tm, SSM_WIDTH, put_x)
    _conv_rows(be_ref, cwb_ref, SSM_CONV, HALO - pad, tm, SSM_BC, put_b)
    bv = bv_ref[...]
    bco_ref[0] = bv.astype(BF16)
    bmt_ref[0] = bv[:, :SSM_GROUPS * SSM_STATE].T.astype(BF16)
    x = dt_ref[0] + dtb_ref[...]
    dto_ref[0] = jnp.maximum(x, 0.0) + jnp.log(1.0 + jnp.exp(-jnp.abs(x)))


def _ssdprep(proj, cwx, cbx, cwb, cbb, dtb, tm):
    b, t, _ = proj.shape
    gs = SSM_GROUPS * SSM_STATE
    return pl.pallas_call(
        functools.partial(_ssdprep_kernel, tm=tm),
        grid=(b, t // tm),
        in_specs=_halo_specs(tm, t, SSM_WIDTH, COL_XS // SSM_WIDTH)
        + _halo_specs(tm, t, SSM_BC, COL_BC // SSM_BC)
        + [
            pl.BlockSpec((1, tm, LANES), lambda bb, i: (bb, i, COL_DT // LANES)),
            pl.BlockSpec((SSM_CONV, SSM_WIDTH), lambda bb, i: (0, 0)),
            pl.BlockSpec((1, SSM_WIDTH), lambda bb, i: (0, 0)),
            pl.BlockSpec((SSM_CONV, SSM_BC), lambda bb, i: (0, 0)),
            pl.BlockSpec((1, SSM_BC), lambda bb, i: (0, 0)),
            pl.BlockSpec((1, LANES), lambda bb, i: (0, 0)),
        ],
        out_specs=[
            pl.BlockSpec((1, tm, SSM_WIDTH), lambda bb, i: (bb, i, 0)),
            pl.BlockSpec((1, tm, SSM_BC), lambda bb, i: (bb, i, 0)),
            pl.BlockSpec((1, gs, tm), lambda bb, i: (bb, 0, i)),
            pl.BlockSpec((1, tm, LANES), lambda bb, i: (bb, i, 0)),
        ],
        out_shape=[
            jax.ShapeDtypeStruct((b, t, SSM_WIDTH), F32),
            jax.ShapeDtypeStruct((b, t, SSM_BC), BF16),
            jax.ShapeDtypeStruct((b, gs, t), BF16),
            jax.ShapeDtypeStruct((b, t, LANES), F32),
        ],
        scratch_shapes=[
            pltpu.VMEM((tm + 2 * HALO, SSM_WIDTH), F32),
            pltpu.VMEM((tm + 2 * HALO, SSM_BC), F32),
            pltpu.VMEM((tm, SSM_BC), F32),
        ],
        compiler_params=_cparams(("parallel", "parallel")),
        name="ssd_prep",
    )(proj, proj, proj, proj, proj, proj, proj, cwx, cbx, cwb, cbb, dtb)


def _split_hi_lo(v):
    hi = v.astype(BF16)
    lo = (v - hi.astype(F32)).astype(BF16)
    return jnp.concatenate([hi, lo], axis=1)


def _ssd_kernel(*refs, direction, with_y, finish, nb):
    it = iter(refs)
    xs_ref, bc_ref, bmt_ref, dt_ref, a_ref, e_ref, h0_ref = (next(it) for _ in range(7))
    if finish:
        yf_ref, z_ref, dsum_ref, nw_ref = (next(it) for _ in range(4))
    if with_y:
        y_ref = next(it)
    hfin_ref = next(it)
    st_ref = next(it)
    if finish:
        out_ref = next(it)
    q = SSM_CHUNK
    c = pl.program_id(1)

    @pl.when(c == 0)
    def _():
        st_ref[...] = h0_ref[...]

    ii = lax.broadcasted_iota(jnp.int32, (q, q), 0)
    jj = lax.broadcasted_iota(jnp.int32, (q, q), 1)
    if direction == 0:
        valid, valid_t, end_row = jj <= ii, ii <= jj, q - 1
    else:
        valid, valid_t, end_row = jj >= ii, ii >= jj, 0
    tri = jnp.where(valid, 1.0, 0.0)
    tri_t = jnp.where(valid_t, 1.0, 0.0)
    e = e_ref[...]

    def expand(v):
        return jnp.dot(_split_hi_lo(v), e, preferred_element_type=F32)

    gw = SSM_GROUP_WIDTH
    hpg = SSM_HEADS // SSM_GROUPS
    for bi in range(nb):
        dt = dt_ref[bi]
        da = dt * a_ref[...]
        acum_c = jnp.dot(tri, da, precision=lax.Precision.HIGHEST, preferred_element_type=F32)
        acum_r = jnp.dot(da.T, tri_t, precision=lax.Precision.HIGHEST, preferred_element_type=F32)
        tot = acum_c[end_row:end_row + 1, :]
        to_end = jnp.exp(tot - acum_c)
        ea = jnp.exp(acum_c)
        ea_e = expand(ea)
        xs = xs_ref[bi]
        xw = (xs * expand(dt * to_end)).astype(BF16)
        cdec_e = ea_e[end_row:end_row + 1, :]
        bc = bc_ref[bi]
        bmt = bmt_ref[bi]
        if with_y:
            xdt = (xs * expand(dt)).astype(BF16)
            lane = lax.broadcasted_iota(jnp.int32, xdt.shape, 1)
            left = (lane % LANES) < SSM_HEAD_DIM
            zero = jnp.zeros_like(xdt)
            xdt_l = jnp.where(left, xdt, zero)
            xdt_r = jnp.where(left, zero, xdt)
        for g in range(SSM_GROUPS):
            bm_t = bmt[g * SSM_STATE:(g + 1) * SSM_STATE, :]
            st = st_ref[bi, g]
            if with_y:
                cm = bc[:, (SSM_GROUPS + g) * SSM_STATE:(SSM_GROUPS + g + 1) * SSM_STATE]
                bm = bc[:, g * SSM_STATE:(g + 1) * SSM_STATE]
                cb = lax.dot_general(cm, bm, (((1,), (1,)), ((), ())), preferred_element_type=F32)
                y_inter = jnp.dot(cm, st.astype(BF16), preferred_element_type=F32) * ea_e[:, g * gw:(g + 1) * gw]
                parts = []
                for k in range(hpg // 2):
                    ms = []
                    for r in (2 * k, 2 * k + 1):
                        hl = SSM_HEADS * direction + g * hpg + r
                        seg = acum_c[:, hl:hl + 1] - acum_r[hl:hl + 1, :]
                        ms.append((cb * jnp.where(valid, jnp.exp(seg), 0.0)).astype(BF16))
                    c0 = g * gw + k * LANES
                    rhs = jnp.concatenate([xdt_l[:, c0:c0 + LANES], xdt_r[:, c0:c0 + LANES]], axis=0)
                    parts.append(jnp.dot(jnp.concatenate(ms, axis=1), rhs, preferred_element_type=F32))
                y_g = y_inter + jnp.concatenate(parts, axis=1)
                sl = slice(g * gw, (g + 1) * gw)
                if finish:
                    y_ref[bi, :, sl] = yf_ref[bi, :, sl] + y_g + dsum_ref[:, sl] * xs[:, sl]
                else:
                    y_ref[bi, :, sl] = y_g
            st_ref[bi, g] = st * cdec_e[:, g * gw:(g + 1) * gw] + jnp.dot(
                bm_t, xw[:, g * gw:(g + 1) * gw], preferred_element_type=F32)
        if finish:
            gz = y_ref[bi] * _silu(z_ref[bi])
            var = jnp.mean(gz * gz, axis=-1, keepdims=True)
            out_ref[bi] = (gz * lax.rsqrt(var + EPS) * nw_ref[...]).astype(BF16)

    @pl.when(c == pl.num_programs(1) - 1)
    def _():
        hfin_ref[...] = st_ref[...]


def _ssd_scan(xs, bc, bmt, dt, a_row, e_mat, h0, direction, with_y, fin=None):
    b, t, _ = xs.shape
    nb = 2 if b % 2 == 0 else 1
    nc = t // SSM_CHUNK
    q = SSM_CHUNK
    gs = SSM_GROUPS * SSM_STATE
    finish = fin is not None
    if direction == 0:
        tmap = lambda bb, c: (bb, c, 0)
        tmap_t = lambda bb, c: (bb, 0, c)
        zmap = lambda bb, c: (bb, c, COL_Z // SSM_WIDTH)
    else:
        tmap = lambda bb, c: (bb, nc - 1 - c, 0)
        tmap_t = lambda bb, c: (bb, 0, nc - 1 - c)
        zmap = lambda bb, c: (bb, nc - 1 - c, COL_Z // SSM_WIDTH)
    st_spec = pl.BlockSpec((nb, SSM_GROUPS, SSM_STATE, SSM_GROUP_WIDTH), lambda bb, c: (bb, 0, 0, 0))
    in_specs = [
        pl.BlockSpec((nb, q, SSM_WIDTH), tmap),
        pl.BlockSpec((nb, q, SSM_BC), tmap),
        pl.BlockSpec((nb, gs, q), tmap_t),
        pl.BlockSpec((nb, q, LANES), tmap),
        pl.BlockSpec((1, LANES), lambda bb, c: (0, 0)),
        pl.BlockSpec((2 * LANES, SSM_WIDTH), lambda bb, c: (0, 0)),
        st_spec,
    ]
    args = [xs, bc, bmt, dt, a_row, e_mat, h0]
    out_specs, out_shape, scratch = [], [], []
    if finish:
        y_fwd, proj, dsum_e, nw = fin
        in_specs += [
            pl.BlockSpec((nb, q, SSM_WIDTH), tmap),
            pl.BlockSpec((nb, q, SSM_WIDTH), zmap),
            pl.BlockSpec((1, SSM_WIDTH), lambda bb, c: (0, 0)),
            pl.BlockSpec((1, SSM_WIDTH), lambda bb, c: (0, 0)),
        ]
        args += [y_fwd, proj, dsum_e, nw]
    elif with_y:
        out_specs.append(pl.BlockSpec((nb, q, SSM_WIDTH), tmap))
        out_shape.append(jax.ShapeDtypeStruct((b, t, SSM_WIDTH), F32))
    out_specs.append(st_spec)
    out_shape.append(jax.ShapeDtypeStruct((b, SSM_GROUPS, SSM_STATE, SSM_GROUP_WIDTH), F32))
    scratch.append(pltpu.VMEM((nb, SSM_GROUPS, SSM_STATE, SSM_GROUP_WIDTH), F32))
    if finish:
        out_specs.append(pl.BlockSpec((nb, q, SSM_WIDTH), tmap))
        out_shape.append(jax.ShapeDtypeStruct((b, t, SSM_WIDTH), BF16))
    return pl.pallas_call(
        functools.partial(_ssd_kernel_ordered, direction=direction, with_y=with_y, finish=finish, nb=nb),
        grid=(b // nb, nc),
        in_specs=in_specs,
        out_specs=out_specs,
        out_shape=out_shape,
        scratch_shapes=scratch + ([pltpu.VMEM((nb, q, SSM_WIDTH), F32)] if finish else []),
        compiler_params=_cparams(("parallel", "arbitrary")),
        name="ssd_scan",
    )(*args)


def _ssd_kernel_ordered(*refs, direction, with_y, finish, nb):
    n_in = 11 if finish else 7
    ins = list(refs[:n_in])
    rest = list(refs[n_in:])
    if finish:
        hfin, out, st, ybuf = rest
        ordered = ins + [ybuf, hfin, st, out]
    elif with_y:
        y, hfin, st = rest
        ordered = ins + [y, hfin, st]
    else:
        hfin, st = rest
        ordered = ins + [hfin, st]
    _ssd_kernel(*ordered, direction=direction, with_y=with_y, finish=finish, nb=nb)


def _merge_kernel(x_ref, nw_ref, sh_ref, sc_ref, gt_ref, a_ref, b_ref, c_ref, wg0_ref, wg1_ref, wg2_ref,
                  wa_ref, wb_ref, wc_ref, wo_ref, o_ref, h_ref):
    j = pl.program_id(2)

    def contribution(rows, h):
        s = None
        for act_ref, wp_ref, wg_ref in ((a_ref, wa_ref, wg0_ref), (b_ref, wb_ref, wg1_ref), (c_ref, wc_ref, wg2_ref)):
            y = jnp.dot(act_ref[0, rows], wp_ref[...], preferred_element_type=F32)
            gate = _sigmoid(jnp.dot(h, wg_ref[...], preferred_element_type=F32))
            s = gate * y if s is None else s + gate * y
        return jnp.dot(s.astype(BF16), wo_ref[...], preferred_element_type=F32)

    @pl.when(j == 0)
    def _():
        for r in range(0, h_ref.shape[0], ROW_CHUNK):
            rows = slice(r, r + ROW_CHUNK)
            hc = _rms_mod(x_ref[0, rows], nw_ref[...], sh_ref[0], sc_ref[0]).astype(BF16)
            h_ref[rows] = hc
            o_ref[0, rows] = contribution(rows, hc)

    @pl.when(j > 0)
    def _():
        o_ref[0] += contribution(slice(None), h_ref[...])

    @pl.when(j == pl.num_programs(2) - 1)
    def _():
        o_ref[0] = x_ref[0] + gt_ref[0] * o_ref[0]


def _merge(x, nw, shift, scale, gate, act_a, act_b, act_c, wg, wa, wb, wc, wo, tm):
    b, t, d = x.shape
    tn = 512
    nj = d // tn
    vec = pl.BlockSpec((1, 1, d), lambda bb, i, j: (bb, 0, 0))
    act =pl.BlockSpec((1, tm, ATTN_WIDTH), lambda bb, i, j: (bb, i, 0))
    wproj = pl.BlockSpec((ATTN_WIDTH, tn), lambda bb, i, j: (0, j))
    return pl.pallas_call(
        _merge_kernel,
        grid=(b, t // tm, nj),
        in_specs=[
            pl.BlockSpec((1, tm, d), lambda bb, i, j: (bb, i, 0)),
            pl.BlockSpec((1, d), lambda bb, i, j: (0, 0)),
            vec, vec, vec, act, act, act,
            pl.BlockSpec((d, tn), lambda bb, i, j: (0, j)),
            pl.BlockSpec((d, tn), lambda bb, i, j: (0, nj + j)),
            pl.BlockSpec((d, tn), lambda bb, i, j: (0, 2 * nj + j)),
            wproj, wproj, wproj,
            pl.BlockSpec((tn, d), lambda bb, i, j: (j, 0)),
        ],
        out_specs=pl.BlockSpec((1, tm, d), lambda bb, i, j: (bb, i, 0)),
        out_shape=jax.ShapeDtypeStruct((b, t, d), F32),
        scratch_shapes=[pltpu.VMEM((tm, d), BF16)],
        compiler_params=_cparams(("parallel", "parallel", "arbitrary")),
        name="branch_merge",
    )(x, nw, shift, scale, gate, act_a, act_b, act_c, wg, wg, wg, wa, wb, wc, wo)


def _ffn_kernel(x_ref, xp_ref, xn_ref, nw_ref, sh_ref, sc_ref, gt_ref, wug_ref, wuv_ref, cwg_ref, cwv_ref,
                cbg_ref, cbv_ref, wd_ref, fw_ref, o_ref, h_ref, pg_ref, pv_ref, *, tm, final):
    i = pl.program_id(1)
    j = pl.program_id(2)
    rows = tm + 2 * HALO

    def conv(p, cw_ref, cb_ref):
        prev = pltpu.roll(p, 1, 0)[HALO:HALO + tm]
        nxt = pltpu.roll(p, rows - 1, 0)[HALO:HALO + tm]
        return prev * cw_ref[0:1, :] + p[HALO:HALO + tm] * cw_ref[1:2, :] + nxt * cw_ref[2:3, :] + cb_ref[...]

    def down(pg, pv):
        s = (_silu(conv(pg, cwg_ref, cbg_ref)) * conv(pv, cwv_ref, cbv_ref)).astype(BF16)
        return jnp.dot(s, wd_ref[...], preferred_element_type=F32)

    @pl.when(j == 0)
    def _():
        nw, sh, sc = nw_ref[...], sh_ref[0], sc_ref[0]
        for r in range(0, tm, ROW_CHUNK):
            hc = _rms_mod(x_ref[0, r:r + ROW_CHUNK], nw, sh, sc).astype(BF16)
            lo, hi = HALO + r, HALO + r + ROW_CHUNK
            if r == 0:
                hp = jnp.where(i > 0, _rms_mod(xp_ref[0], nw, sh, sc), 0.0).astype(BF16)
                hc, lo = jnp.concatenate([hp, hc], axis=0), 0
            if r + ROW_CHUNK == tm:
                hn = jnp.where(i < pl.num_programs(1) - 1, _rms_mod(xn_ref[0], nw, sh, sc), 0.0).astype(BF16)
                hc, hi = jnp.concatenate([hc, hn], axis=0), rows
            h_ref[lo:hi] = hc
            pg_ref[lo:hi] = jnp.dot(hc, wug_ref[...], preferred_element_type=F32)
            pv_ref[lo:hi] = jnp.dot(hc, wuv_ref[...], preferred_element_type=F32)
        o_ref[0] = down(pg_ref[...], pv_ref[...])

    @pl.when(j > 0)
    def _():
        h = h_ref[...]
        o_ref[0] += down(jnp.dot(h, wug_ref[...], preferred_element_type=F32),
                         jnp.dot(h, wuv_ref[...], preferred_element_type=F32))

    @pl.when(j == pl.num_programs(2) - 1)
    def _():
        y = x_ref[0] + gt_ref[0] * o_ref[0]
        if final:
            y = y * lax.rsqrt(jnp.mean(y * y, axis=-1, keepdims=True) + EPS) * fw_ref[...]
        o_ref[0] = y


def _ffn(x, nw, shift, scale, gate, w_up, conv_w, conv_b, w_down, tm, final_w=None):
    b, t, d = x.shape
    final = final_w is not None
    if not final:
        final_w = nw
    dff = w_down.shape[0]
    tf = 512
    nj = dff // tf
    per = tm // HALO
    last = t // HALO - 1
    vec = pl.BlockSpec((1, 1, d), lambda bb, i, j: (bb, 0, 0))
    return pl.pallas_call(
        functools.partial(_ffn_kernel, tm=tm, final=final),
        grid=(b, t // tm, nj),
        in_specs=[
            pl.BlockSpec((1, tm, d), lambda bb, i, j: (bb, i, 0)),
            pl.BlockSpec((1, HALO, d), lambda bb, i, j: (bb, jnp.maximum(i * per - 1, 0), 0)),
            pl.BlockSpec((1, HALO, d), lambda bb, i, j: (bb, jnp.minimum((i + 1) * per, last), 0)),
            pl.BlockSpec((1, d), lambda bb, i, j: (0, 0)),
            vec, vec, vec,
            pl.BlockSpec((d, tf), lambda bb, i, j: (0, j)),
            pl.BlockSpec((d, tf), lambda bb, i, j: (0, nj + j)),
            pl.BlockSpec((FFN_CONV, tf), lambda bb, i, j: (0, j)),
            pl.BlockSpec((FFN_CONV, tf), lambda bb, i, j: (0, nj + j)),
            pl.BlockSpec((1, tf), lambda bb, i, j: (0, j)),
            pl.BlockSpec((1, tf), lambda bb, i, j: (0, nj + j)),
            pl.BlockSpec((tf, d), lambda bb, i, j: (j, 0)),
            pl.BlockSpec((1, d), lambda bb, i, j: (0, 0)),
        ],
        out_specs=pl.BlockSpec((1, tm, d), lambda bb, i, j: (bb, i, 0), pipeline_mode=pl.Buffered(1)),
        out_shape=jax.ShapeDtypeStruct((b, t, d), F32),
        scratch_shapes=[
            pltpu.VMEM((tm + 2 * HALO, d), BF16),
            pltpu.VMEM((tm + 2 * HALO, tf), F32),
            pltpu.VMEM((tm + 2 * HALO, tf), F32),
        ],
        compiler_params=_cparams(("parallel", "parallel", "arbitrary")),
        name="conv_ffn",
    )(x, x, x, nw, shift, scale, gate, w_up, w_up, conv_w, conv_w, conv_b, conv_b, w_down, final_w)


def _rope_tables(n_tokens):
    rows = n_tokens // GRID_W
    row = jnp.repeat(jnp.arange(rows), GRID_W).astype(F32)
    col = jnp.tile(jnp.arange(GRID_W), rows).astype(F32)
    inv = ROPE_THETA ** (-jnp.arange(0, ROPE_AXIS_DIM, 2, dtype=F32) / ROPE_AXIS_DIM)
    ar, ac = row[:, None] * inv, col[:, None] * inv
    cos = jnp.concatenate([jnp.cos(ar), jnp.cos(ar), jnp.cos(ac), jnp.cos(ac)], axis=1)
    sin = jnp.concatenate([-jnp.sin(ar), jnp.sin(ar), -jnp.sin(ac), jnp.sin(ac)], axis=1)
    return cos, sin


def _expansion_matrix(direction):
    head_of_col = jnp.arange(SSM_WIDTH) // SSM_HEAD_DIM
    row = jnp.arange(LANES)[:, None]
    e = (row == head_of_col[None, :] + SSM_HEADS * direction).astype(BF16)
    return jnp.concatenate([e, e], axis=0)


def _pad_lanes(v, offset):
    return jnp.zeros((1, LANES), F32).at[0, offset:offset + v.shape[0]].set(v.astype(F32))


def kernel(x, c, ctx, c_ctx, w_mod, b_mod, norm_mix_w, norm_ffn_w, w_in, q_norm_w, k_norm_w, w_attn_o, conf_conv_w, conf_conv_b, conf_ln_w, conf_ln_b, w_conf_o, ssm_conv_w, ssm_conv_b, ssm_a_log, ssm_dt_bias, ssm_d, ssm_norm_w, w_ssm_o, w_out, ffn_w_up, ffn_conv_w, ffn_conv_b, ffn_w_down, final_norm_w):
    bsz, n_lat, d = x.shape
    n_ctx = ctx.shape[1]
    depth = w_mod.shape[0]
    assert bsz <= 7 and n_lat % 512 == 0 and n_ctx % 256 == 0

    q_end = ATTN_WIDTH
    v_end = q_end + 2 * KV_WIDTH
    conf_end = v_end + 2 * CONF_WIDTH
    z_end = conf_end + SSM_WIDTH
    xbc_end = z_end + SSM_WIDTH + SSM_BC
    dt_end = xbc_end + 2 * SSM_HEADS

    cc = jnp.zeros((8, d), F32).at[:bsz].set(c).at[bsz].set(c_ctx)
    mods = _mod_vectors(cc, w_mod, b_mod).reshape(depth, 8, N_MOD, d)
    rope = _rope_tables(n_lat)
    e_mats = [_expansion_matrix(0), _expansion_matrix(1)]
    zero_state = jnp.zeros((bsz, SSM_GROUPS, SSM_STATE, SSM_GROUP_WIDTH), F32)

    for l in range(depth):
        need_ctx = l < depth - 1
        wi = w_in[l]
        w_mix = jnp.concatenate([
            wi[:, :q_end], wi[:, v_end:conf_end], wi[:, conf_end:z_end], wi[:, z_end:z_end + SSM_WIDTH],
            wi[:, q_end:v_end], wi[:, z_end + SSM_WIDTH:xbc_end], wi[:, xbc_end:dt_end],
            jnp.zeros((d, MIX_WIDTH - dt_end), F32)], axis=1).astype(BF16)
        w_gate = wi[:, dt_end:].astype(BF16)
        wa, wb, wc, wo = (w[l].astype(BF16) for w in (w_attn_o, w_conf_o, w_ssm_o, w_out))
        w_up, w_down = ffn_w_up[l].astype(BF16), ffn_w_down[l].astype(BF16)
        row = lambda v: v.reshape(1, -1)
        nmix, nffn = row(norm_mix_w[l]), row(norm_ffn_w[l])
        cwx, cwb = ssm_conv_w[l][:, :SSM_WIDTH], ssm_conv_w[l][:, SSM_WIDTH:]
        cbx, cbb = row(ssm_conv_b[l][:SSM_WIDTH]), row(ssm_conv_b[l][SSM_WIDTH:])
        a_coef = -jnp.exp(ssm_a_log[l].astype(F32))
        a_rows = [_pad_lanes(a_coef[0], 0), _pad_lanes(a_coef[1], SSM_HEADS)]
        dtb = _pad_lanes(ssm_dt_bias[l].reshape(-1), 0)
        dsum_e = row(jnp.repeat(ssm_d[l][0] + ssm_d[l][1], SSM_HEAD_DIM))
        m_lat = [mods[l, :bsz, k].reshape(bsz, 1, d) for k in range(N_MOD)]
        m_ctx = [jnp.broadcast_to(mods[l, bsz, k].reshape(1, 1, d), (bsz, 1, d)) for k in range(N_MOD)]

        proj_l = _inproj(x, nmix, m_lat[0], m_lat[1], w_mix, 1024 if n_lat % 1024 == 0 else 512)
        proj_c = _inproj(ctx, nmix, m_ctx[0], m_ctx[1], w_mix, 256)

        q_l, k_l, v_l = _qkprep(proj_l, row(q_norm_w[l]), row(k_norm_w[l]), rope, 256)
        q_c, k_c, v_c = _qkprep(proj_c, row(q_norm_w[l]), row(k_norm_w[l]), None, 256)
        act_a_l = _attention(q_l, [(k_c, v_c), (k_l, v_l)], 512, 256)
        conf_args = (conf_conv_w[l], row(conf_conv_b[l]), row(conf_ln_w[l]), row(conf_ln_b[l]))
        act_b_l = _conformer(proj_l, *conf_args, 256)

        prep_l = _ssdprep(proj_l, cwx, cbx, cwb, cbb, dtb, 256)
        prep_c = _ssdprep(proj_c, cwx, cbx, cwb, cbb, dtb, 256)
        fin_args = (dsum_e, row(ssm_norm_w[l]))
        if need_ctx:
            y_cf, h_cf = _ssd_scan(*prep_c, a_rows[0], e_mats[0], zero_state, 0, True)
            h_cb, act_c_c = _ssd_scan(*prep_c, a_rows[1], e_mats[1], zero_state, 1, True, (y_cf, proj_c) + fin_args)
        else:
            (h_cf,) = _ssd_scan(*prep_c, a_rows[0], e_mats[0], zero_state, 0, False)
            (h_cb,) = _ssd_scan(*prep_c, a_rows[1], e_mats[1], zero_state, 1, False)
        y_lf, _ = _ssd_scan(*prep_l, a_rows[0], e_mats[0], h_cf, 0, True)
        _, act_c_l = _ssd_scan(*prep_l, a_rows[1], e_mats[1], h_cb, 1, True, (y_lf, proj_l) + fin_args)

        x = _merge(x, nmix, m_lat[0], m_lat[1], m_lat[2], act_a_l, act_b_l, act_c_l, w_gate, wa, wb, wc, wo, 512)
        ffn_args = (w_up, ffn_conv_w[l], row(ffn_conv_b[l]), w_down)
        x = _ffn(x, nffn, m_lat[3], m_lat[4], m_lat[5], *ffn_args, 1024 if n_lat % 1024 == 0 else 512,
                 final_w=None if need_ctx else final_norm_w.reshape(1, d))
        if need_ctx:
            act_a_c = _attention(q_c, [(k_c, v_c)], 256, 256)
            act_b_c = _conformer(proj_c, *conf_args, 256)
            ctx = _merge(ctx, nmix, m_ctx[0], m_ctx[1], m_ctx[2], act_a_c, act_b_c, act_c_c, w_gate, wa, wb, wc, wo, 256)
            ctx = _ffn(ctx, nffn, m_ctx[3], m_ctx[4], m_ctx[5], *ffn_args, 256)
    return x
```

```python
import functools
import math

import jax
import jax.numpy as jnp
from jax import lax
from jax.experimental import pallas as pl
from jax.experimental.pallas import tpu as pltpu

F32 = jnp.float32
BF16 = jnp.bfloat16

EPS = 1e-6
N_MOD = 6
GRID_W = 64
HEAD_DIM = 128
ATTN_HEADS = 8
ATTN_KV_HEADS = 2
ATTN_REP = ATTN_HEADS // ATTN_KV_HEADS
ATTN_WIDTH = ATTN_HEADS * HEAD_DIM
KV_WIDTH = ATTN_KV_HEADS * HEAD_DIM
ROPE_THETA = 10000.0
ROPE_AXIS_DIM = HEAD_DIM // 2
ROPE_FREQS = ROPE_AXIS_DIM // 2
CONF_WIDTH = 1024
CONF_KERNEL = 31
SSM_HEADS = 16
SSM_HEAD_DIM = 64
SSM_WIDTH = SSM_HEADS * SSM_HEAD_DIM
SSM_GROUPS = 2
SSM_STATE = 128
SSM_GROUP_WIDTH = SSM_WIDTH // SSM_GROUPS
SSM_BC = 2 * SSM_GROUPS * SSM_STATE
SSM_CONV = 5
SSM_CHUNK = 128
FFN_CONV = 3
N_BRANCHES = 3

LANES = 128
HALO = 16
VMEM_LIMIT = 56 * 1024 * 1024
LOG2E = 1.4426950408889634
ATTN_QSCALE = HEAD_DIM ** -0.5 * LOG2E
VT_PAD = 16
VT_ROWS = HEAD_DIM + VT_PAD
NEG_BIG = -1e30

COL_Q = 0
COL_A = 1024
COL_G = 2048
COL_Z = 3072
COL_XS = 4096
COL_KV = 5120
COL_BC = 5632
COL_DT = 6144
MIX_WIDTH = 6400


def _cparams(sem):
    return pltpu.CompilerParams(dimension_semantics=sem, vmem_limit_bytes=VMEM_LIMIT)


def _sigmoid(x):
    return 1.0 / (1.0 + jnp.exp(-x))


def _silu(x):
    return x * _sigmoid(x)


def _rms_mod(x, nw, shift, scale):
    var = jnp.mean(x * x, axis=-1, keepdims=True)
    y = x * lax.rsqrt(var + EPS) * nw
    return y * (1.0 + scale) + shift


def _mod_kernel(c_ref, w_ref, b_ref, o_ref):
    s = _silu(c_ref[...]).astype(BF16)
    o_ref[0] = jnp.dot(s, w_ref[0].astype(BF16), preferred_element_type=F32) + b_ref[0]


def _mod_vectors(cc, w_mod, b_mod):
    depth, d, nm = w_mod.shape
    tn = 2048
    return pl.pallas_call(
        _mod_kernel,
        grid=(depth, nm // tn),
        in_specs=[
            pl.BlockSpec((8, d), lambda l, j: (0, 0)),
            pl.BlockSpec((1, d, tn), lambda l, j: (l, 0, j)),
            pl.BlockSpec((1, 1, tn), lambda l, j: (l, 0, j)),
        ],
        out_specs=pl.BlockSpec((1, 8, tn), lambda l, j: (l, 0, j)),
        out_shape=jax.ShapeDtypeStruct((depth, 8, nm), F32),
        compiler_params=_cparams(("parallel", "parallel")),
        name="mod_vectors",
    )(cc, w_mod, b_mod.reshape(depth, 1, nm))


ROW_CHUNK = 256


def _inproj_kernel(x_ref, nw_ref, sh_ref, sc_ref, w_ref, o_ref, h_ref):
    j = pl.program_id(2)

    @pl.when(j == 0)
    def _():
        for r in range(0, h_ref.shape[0], ROW_CHUNK):
            rows = slice(r, r + ROW_CHUNK)
            hc = _rms_mod(x_ref[0, rows], nw_ref[...], sh_ref[0], sc_ref[0]).astype(BF16)
            h_ref[rows] = hc
            o_ref[0, rows] = jnp.dot(hc, w_ref[...], preferred_element_type=F32)

    @pl.when(j > 0)
    def _():
        o_ref[0] = jnp.dot(h_ref[...], w_ref[...], preferred_element_type=F32)


def _inproj(x, nw, shift, scale, w, l, tm):
    b, t, d = x.shape
    n = w.shape[2]
    tn = 1280
    return pl.pallas_call(
        _inproj_kernel,
        grid=(b, t // tm, n // tn),
        in_specs=[
            pl.BlockSpec((1, tm, d), lambda bb, i, j: (bb, i, 0)),
            pl.BlockSpec((1, d), lambda bb, i, j: (0, 0)),
            pl.BlockSpec((1, 1, d), lambda bb, i, j: (bb, 0, 0)),
            pl.BlockSpec((1, 1, d), lambda bb, i, j: (bb, 0, 0)),
            pl.BlockSpec((None, d, tn), lambda bb, i, j: (l, 0, j)),
        ],
        out_specs=pl.BlockSpec((1, tm, tn), lambda bb, i, j: (bb, i, j)),
        out_shape=jax.ShapeDtypeStruct((b, t, n), F32),
        scratch_shapes=[pltpu.VMEM((tm, d), BF16)],
        compiler_params=_cparams(("parallel", "parallel", "arbitrary")),
        name="mixer_inproj",
    )(x, nw, shift, scale, w)


def _head_norm(x, w):
    var = jnp.mean(x * x, axis=-1, keepdims=True)
    return x * lax.rsqrt(var + EPS) * w


def _rope(x, c, s):
    lane = lax.broadcasted_iota(jnp.int32, x.shape, 1)
    first = (lane % ROPE_AXIS_DIM) < ROPE_FREQS
    partner = jnp.where(first, pltpu.roll(x, HEAD_DIM - ROPE_FREQS, 1), pltpu.roll(x, ROPE_FREQS, 1))
    return x * c + partner * s


def _qkprep_kernel(*refs, use_rope):
    if use_rope:
        q_ref, kv_ref, qw_ref, kw_ref, c_ref, s_ref, qo_ref, ko_ref, vo_ref = refs
    else:
        q_ref, kv_ref, qw_ref, kw_ref, qo_ref, ko_ref, vo_ref = refs
    q = q_ref[0]
    kv = kv_ref[0]
    for h in range(ATTN_HEADS):
        y = _head_norm(q[:, h * HEAD_DIM:(h + 1) * HEAD_DIM], qw_ref[...])
        if use_rope:
            y = _rope(y, c_ref[...], s_ref[...])
        qo_ref[0, :, h * HEAD_DIM:(h + 1) * HEAD_DIM] = (y * ATTN_QSCALE).astype(BF16)
    for h in range(ATTN_KV_HEADS):
        y = _head_norm(kv[:, h * HEAD_DIM:(h + 1) * HEAD_DIM], kw_ref[...])
        if use_rope:
            y = _rope(y, c_ref[...], s_ref[...])
        ko_ref[0, :, h * HEAD_DIM:(h + 1) * HEAD_DIM] = y.astype(BF16)
    vt = kv[:, KV_WIDTH:].T.astype(BF16)
    ones = jnp.ones((VT_PAD, vt.shape[1]), BF16)
    for g in range(ATTN_KV_HEADS):
        vo_ref[0, g * VT_ROWS:g * VT_ROWS + HEAD_DIM] = vt[g * HEAD_DIM:(g + 1) * HEAD_DIM]
        vo_ref[0, g * VT_ROWS + HEAD_DIM:(g + 1) * VT_ROWS] = ones


def _qkprep(proj, qw, kw, rope, tm):
    b, t, _ = proj.shape
    use_rope = rope is not None
    in_specs = [
        pl.BlockSpec((1, tm, ATTN_WIDTH), lambda bb, i: (bb, i, COL_Q // ATTN_WIDTH)),
        pl.BlockSpec((1, tm, 2 * KV_WIDTH), lambda bb, i: (bb, i, COL_KV // (2 * KV_WIDTH))),
        pl.BlockSpec((1, HEAD_DIM), lambda bb, i: (0, 0)),
        pl.BlockSpec((1, HEAD_DIM), lambda bb, i: (0, 0)),
    ]
    args = [proj, proj, qw, kw]
    if use_rope:
        in_specs += [pl.BlockSpec((tm, HEAD_DIM), lambda bb, i: (i, 0))] * 2
        args += list(rope)
    return pl.pallas_call(
        functools.partial(_qkprep_kernel, use_rope=use_rope),
        grid=(b, t // tm),
        in_specs=in_specs,
        out_specs=[
            pl.BlockSpec((1, tm, ATTN_WIDTH), lambda bb, i: (bb, i, 0)),
            pl.BlockSpec((1, tm, KV_WIDTH), lambda bb, i: (bb, i, 0)),
            pl.BlockSpec((1, ATTN_KV_HEADS * VT_ROWS, tm), lambda bb, i: (bb, 0, i)),
        ],
        out_shape=[
            jax.ShapeDtypeStruct((b, t, ATTN_WIDTH), BF16),
            jax.ShapeDtypeStruct((b, t, KV_WIDTH), BF16),
            jax.ShapeDtypeStruct((b, ATTN_KV_HEADS * VT_ROWS, t), BF16),
        ],
        compiler_params=_cparams(("parallel", "parallel")),
        name="qk_prep",
    )(*args)


def _attn_kernel(*refs, tq, tk, seg_chunks):
    n_seg = len(seg_chunks)
    q_ref = refs[0]
    kv_refs = refs[1:1 + 2 * n_seg]
    o_ref, s0_ref, s1_ref, acc_ref = refs[1 + 2 * n_seg:]
    q = q_ref[0]
    qs = jnp.concatenate([q[:, r * HEAD_DIM:(r + 1) * HEAD_DIM] for r in range(ATTN_REP)], axis=0)
    nq = ATTN_REP * tq
    chunks = [(kv_refs[2 * s], kv_refs[2 * s + 1], slice(c * tk, (c + 1) * tk))
              for s in range(n_seg) for c in range(seg_chunks[s])]

    def scores(j):
        k_ref, _, rows = chunks[j]
        return lax.dot_general(k_ref[0, rows, :], qs, (((1,), (1,)), ((), ())), preferred_element_type=F32)

    def update(j, s_ref, m):
        _, vt_ref, cols = chunks[j]
        s = s_ref[...]
        m_new = jnp.maximum(m, jnp.max(s, axis=0, keepdims=True))
        alpha = jnp.exp2(m - m_new)
        p = jnp.exp2(s - m_new).astype(BF16)
        acc_ref[...] = alpha * acc_ref[...] + jnp.dot(vt_ref[0, :, cols], p, preferred_element_type=F32)
        return m_new

    s_refs = (s0_ref, s1_ref)
    s0_ref[...] = scores(0)
    acc_ref[...] = jnp.zeros_like(acc_ref)
    m = jnp.full((1, nq), NEG_BIG, F32)
    for j in range(len(chunks)):
        if j + 1 < len(chunks):
            s_refs[(j + 1) % 2][...] = scores(j + 1)
        m = update(j, s_refs[j % 2], m)
    o = acc_ref[:HEAD_DIM] / acc_ref[HEAD_DIM:HEAD_DIM + 1]
    for r in range(ATTN_REP):
        o_ref[0, :, r * HEAD_DIM:(r + 1) * HEAD_DIM] = o[:, r * tq:(r + 1) * tq].T.astype(BF16)


def _attention(q, kv_segments, tq, tk):
    b, t, _ = q.shape
    gw = ATTN_REP * HEAD_DIM
    nq = ATTN_REP * tq
    in_specs = [pl.BlockSpec((1, tq, gw), lambda bb, g, i: (bb, i, g))]
    args = [q]
    for k, vt in kv_segments:
        tkv = k.shape[1]
        in_specs += [
            pl.BlockSpec((1, tkv, HEAD_DIM), lambda bb, g, i: (bb, 0, g)),
            pl.BlockSpec((1, VT_ROWS, tkv), lambda bb, g, i: (bb, g, 0)),
        ]
        args += [k, vt]
    return pl.pallas_call(
        functools.partial(_attn_kernel, tq=tq, tk=tk, seg_chunks=tuple(k.shape[1] // tk for k, _ in kv_segments)),
        grid=(b, ATTN_KV_HEADS, t // tq),
        in_specs=in_specs,
        out_specs=pl.BlockSpec((1, tq, gw), lambda bb, g, i: (bb, i, g)),
        out_shape=jax.ShapeDtypeStruct((b, t, ATTN_WIDTH), BF16),
        scratch_shapes=[pltpu.VMEM((tk, nq), F32), pltpu.VMEM((tk, nq), F32), pltpu.VMEM((VT_ROWS, nq), F32)],
        compiler_params=_cparams(("parallel", "parallel", "arbitrary")),
        name="attention",
    )(*args)


def _halo_specs(tm, t, width, col_block):
    per = tm // HALO
    last = t // HALO - 1
    return [
        pl.BlockSpec((1, tm, width), lambda bb, i: (bb, i, col_block)),
        pl.BlockSpec((1, HALO, width), lambda bb, i: (bb, jnp.maximum(i * per - 1, 0), col_block)),
        pl.BlockSpec((1, HALO, width), lambda bb, i: (bb, jnp.minimum((i + 1) * per, last), col_block)),
    ]


SUBLANES = 8


def _conv_rows(ext_ref, w_ref, taps, first_row, tm, width, out_fn, shift_ref=None, rows_per=64):
    if shift_ref is not None:
        n_shift = shift_ref.shape[1]
        for c0 in range(0, width, LANES):
            for r in range(1, SUBLANES):
                shift_ref[r - 1, :, c0:c0 + LANES] = ext_ref[r:r + n_shift, c0:c0 + LANES]
    for c0 in range(0, width, LANES):
        for r0 in range(0, tm, rows_per):
            acc = None
            for k in range(taps):
                lo = first_row + r0 + k
                res = lo % SUBLANES
                if shift_ref is None or res == 0:
                    rows = ext_ref[lo:lo + rows_per, c0:c0 + LANES]
                else:
                    rows = shift_ref[res - 1, lo - res:lo - res + rows_per, c0:c0 + LANES]
                term = rows * w_ref[k:k + 1, c0:c0 + LANES]
                acc = term if acc is None else acc + term
            out_fn(r0, rows_per, c0, acc)


def _conf_kernel(a_ref, ap_ref, an_ref, g_ref, gp_ref, gn_ref, cw_ref, cb_ref, lw_ref, lb_ref,
                 o_ref, u_ref, v_ref, us_ref, *, tm):
    i = pl.program_id(1)
    nt = pl.num_programs(1)
    u_ref[HALO:HALO + tm] = a_ref[0] * _sigmoid(g_ref[0])
    u_ref[0:HALO] = jnp.where(i > 0, ap_ref[0] * _sigmoid(gp_ref[0]), 0.0)
    u_ref[HALO + tm:] = jnp.where(i < nt - 1, an_ref[0] * _sigmoid(gn_ref[0]), 0.0)

    def put(r0, nr, c0, acc):
        v_ref[r0:r0 + nr, c0:c0 + LANES] = acc + cb_ref[:, c0:c0 + LANES]

    pad = (CONF_KERNEL - 1) // 2
    _conv_rows(u_ref, cw_ref, CONF_KERNEL, HALO - pad, tm, CONF_WIDTH, put, shift_ref=us_ref)
    v = v_ref[...]
    mu = jnp.mean(v, axis=-1, keepdims=True)
    d = v - mu
    var = jnp.mean(d * d, axis=-1, keepdims=True)
    y = d * lax.rsqrt(var + EPS) * lw_ref[...] + lb_ref[...]
    o_ref[0] = _silu(y).astype(BF16)


def _conformer(proj, cw, cb, lw, lb, tm):
    b, t, _ = proj.shape
    vec = pl.BlockSpec((1, CONF_WIDTH), lambda bb, i: (0, 0))
    return pl.pallas_call(
        functools.partial(_conf_kernel, tm=tm),
        grid=(b, t // tm),
        in_specs=_halo_specs(tm, t, CONF_WIDTH, COL_A // CONF_WIDTH)
        + _halo_specs(tm, t, CONF_WIDTH, COL_G // CONF_WIDTH)
        + [pl.BlockSpec((CONF_KERNEL, CONF_WIDTH), lambda bb, i: (0, 0)), vec, vec, vec],
        out_specs=pl.BlockSpec((1, tm, CONF_WIDTH), lambda bb, i: (bb, i, 0)),
        out_shape=jax.ShapeDtypeStruct((b, t, CONF_WIDTH), BF16),
        scratch_shapes=[
            pltpu.VMEM((tm + 2 * HALO, CONF_WIDTH), F32),
            pltpu.VMEM((tm, CONF_WIDTH), F32),
            pltpu.VMEM((SUBLANES - 1, tm + 2 * HALO - SUBLANES, CONF_WIDTH), F32),
        ],
        compiler_params=_cparams(("parallel", "parallel")),
        name="conformer",
    )(proj, proj, proj, proj, proj, proj, cw, cb, lw, lb)


def _ssdprep_kernel(x_ref, xp_ref, xn_ref, bc_ref, bcp_ref, bcn_ref, dt_ref, cwx_ref, cbx_ref,
                    cwb_ref, cbb_ref, dtb_ref, xo_ref, bco_ref, bmt_ref, dto_ref, xe_ref, be_ref, bv_ref, *, tm):
    i = pl.program_id(1)
    nt = pl.num_programs(1)
    xe_ref[HALO:HALO + tm] = x_ref[0]
    xe_ref[0:HALO] = jnp.where(i > 0, xp_ref[0], 0.0)
    xe_ref[HALO + tm:] = jnp.where(i < nt - 1, xn_ref[0], 0.0)
    be_ref[HALO:HALO + tm] = bc_ref[0]
    be_ref[0:HALO] = jnp.where(i > 0, bcp_ref[0], 0.0)
    be_ref[HALO + tm:] = jnp.where(i < nt - 1, bcn_ref[0], 0.0)
    pad = (SSM_CONV - 1) // 2

    def put_x(r0, nr, c0, acc):
        xo_ref[0, r0:r0 + nr, c0:c0 + LANES] = _silu(acc + cbx_ref[:, c0:c0 + LANES])

    def put_b(r0, nr, c0, acc):
        bv_ref[r0:r0 + nr, c0:c0 + LANES] = _silu(acc + cbb_ref[:, c0:c0 + LANES])

    _conv_rows(xe_ref, cwx_ref, SSM_CONV, HALO - pad, tm, SSM_WIDTH, put_x)
    _conv_rows(be_ref, cwb_ref, SSM_CONV, HALO - pad, tm, SSM_BC, put_b)
    bv = bv_ref[...]
    bco_ref[0] = bv.astype(BF16)
    bmt_ref[0] = bv[:, :SSM_GROUPS * SSM_STATE].T.astype(BF16)
    x = dt_ref[0] + dtb_ref[...]
    dto_ref[0] = jnp.maximum(x, 0.0) + jnp.log(1.0 + jnp.exp(-jnp.abs(x)))


def _ssdprep(proj, cwx, cbx, cwb, cbb, dtb, tm):
    b, t, _ = proj.shape
    gs = SSM_GROUPS * SSM_STATE
    return pl.pallas_call(
        functools.partial(_ssdprep_kernel, tm=tm),
        grid=(b, t // tm),
        in_specs=_halo_specs(tm, t, SSM_WIDTH, COL_XS // SSM_WIDTH)
        + _halo_specs(tm, t, SSM_BC, COL_BC // SSM_BC)
        + [
            pl.BlockSpec((1, tm, LANES), lambda bb, i: (bb, i, COL_DT // LANES)),
            pl.BlockSpec((SSM_CONV, SSM_WIDTH), lambda bb, i: (0, 0)),
            pl.BlockSpec((1, SSM_WIDTH), lambda bb, i: (0, 0)),
            pl.BlockSpec((SSM_CONV, SSM_BC), lambda bb, i: (0, 0)),
            pl.BlockSpec((1, SSM_BC), lambda bb, i: (0, 0)),
            pl.BlockSpec((1, LANES), lambda bb, i: (0, 0)),
        ],
        out_specs=[
            pl.BlockSpec((1, tm, SSM_WIDTH), lambda bb, i: (bb, i, 0)),
            pl.BlockSpec((1, tm, SSM_BC), lambda bb, i: (bb, i, 0)),
            pl.BlockSpec((1, gs, tm), lambda bb, i: (bb, 0, i)),
            pl.BlockSpec((1, tm, LANES), lambda bb, i: (bb, i, 0)),
        ],
        out_shape=[
            jax.ShapeDtypeStruct((b, t, SSM_WIDTH), F32),
            jax.ShapeDtypeStruct((b, t, SSM_BC), BF16),
            jax.ShapeDtypeStruct((b, gs, t), BF16),
            jax.ShapeDtypeStruct((b, t, LANES), F32),
        ],
        scratch_shapes=[
            pltpu.VMEM((tm + 2 * HALO, SSM_WIDTH), F32),
            pltpu.VMEM((tm + 2 * HALO, SSM_BC), F32),
            pltpu.VMEM((tm, SSM_BC), F32),
        ],
        compiler_params=_cparams(("parallel", "parallel")),
        name="ssd_prep",
    )(proj, proj, proj, proj, proj, proj, proj, cwx, cbx, cwb, cbb, dtb)


def _split3(v, axis):
    hi = v.astype(BF16)
    r1 = v - hi.astype(F32)
    mid = r1.astype(BF16)
    lo = (r1 - mid.astype(F32)).astype(BF16)
    return jnp.concatenate([hi, mid, lo], axis=axis)


def _split_hi_lo(v):
    hi = v.astype(BF16)
    lo = (v - hi.astype(F32)).astype(BF16)
    return jnp.concatenate([hi, lo], axis=1)


def _ssd_kernel(*refs, direction, with_y, finish, nb):
    it = iter(refs)
    xs_ref, bc_ref, bmt_ref, dt_ref, a_ref, e_ref, h0_ref = (next(it) for _ in range(7))
    if finish:
        yf_ref, z_ref, dsum_ref, nw_ref = (next(it) for _ in range(4))
    if with_y:
        y_ref = next(it)
    hfin_ref = next(it)
    st_ref = next(it)
    if finish:
        out_ref = next(it)
    q = SSM_CHUNK
    c = pl.program_id(1)

    @pl.when(c == 0)
    def _():
        st_ref[...] = h0_ref[...]

    ii = lax.broadcasted_iota(jnp.int32, (q, q), 0)
    jj = lax.broadcasted_iota(jnp.int32, (q, q), 1)
    if direction == 0:
        valid, valid_t, end_row = jj <= ii, ii <= jj, q - 1
    else:
        valid, valid_t, end_row = jj >= ii, ii >= jj, 0
    tri = jnp.where(valid, 1.0, 0.0).astype(BF16)
    tri_t = jnp.where(valid_t, 1.0, 0.0).astype(BF16)
    tri3 = jnp.concatenate([tri, tri, tri], axis=1)
    tri3_t = jnp.concatenate([tri_t, tri_t, tri_t], axis=0)
    e = e_ref[...]

    def expand(v):
        return jnp.dot(_split_hi_lo(v), e, preferred_element_type=F32)

    gw = SSM_GROUP_WIDTH
    hpg = SSM_HEADS // SSM_GROUPS
    for bi in range(nb):
        dt = dt_ref[bi]
        da = dt * a_ref[...]
        acum_c = jnp.dot(tri3, _split3(da, 0), preferred_element_type=F32)
        acum_r = jnp.dot(_split3(da.T, 1), tri3_t, preferred_element_type=F32)
        tot = acum_c[end_row:end_row + 1, :]
        to_end = jnp.exp(tot - acum_c)
        ea = jnp.exp(acum_c)
        ea_e = expand(ea)
        xs = xs_ref[bi]
        xw = (xs * expand(dt * to_end)).astype(BF16)
        cdec_e = ea_e[end_row:end_row + 1, :]
        bc = bc_ref[bi]
        bmt = bmt_ref[bi]
        if with_y:
            xdt = (xs * expand(dt)).astype(BF16)
            lane = lax.broadcasted_iota(jnp.int32, xdt.shape, 1)
            left = (lane % LANES) < SSM_HEAD_DIM
            zero = jnp.zeros_like(xdt)
            xdt_l = jnp.where(left, xdt, zero)
            xdt_r = jnp.where(left, zero, xdt)
        for g in range(SSM_GROUPS):
            bm_t = bmt[g * SSM_STATE:(g + 1) * SSM_STATE, :]
            st = st_ref[bi, g]
            if with_y:
                cm = bc[:, (SSM_GROUPS + g) * SSM_STATE:(SSM_GROUPS + g + 1) * SSM_STATE]
                bm = bc[:, g * SSM_STATE:(g + 1) * SSM_STATE]
                cb = lax.dot_general(cm, bm, (((1,), (1,)), ((), ())), preferred_element_type=F32)
                y_inter = jnp.dot(cm, st.astype(BF16), preferred_element_type=F32) * ea_e[:, g * gw:(g + 1) * gw]
                parts = []
                for k in range(hpg // 2):
                    ms = []
                    for r in (2 * k, 2 * k + 1):
                        hl = SSM_HEADS * direction + g * hpg + r
                        seg = acum_c[:, hl:hl + 1] - acum_r[hl:hl + 1, :]
                        ms.append((cb * jnp.where(valid, jnp.exp(seg), 0.0)).astype(BF16))
                    c0 = g * gw + k * LANES
                    rhs = jnp.concatenate([xdt_l[:, c0:c0 + LANES], xdt_r[:, c0:c0 + LANES]], axis=0)
                    parts.append(jnp.dot(jnp.concatenate(ms, axis=1), rhs, preferred_element_type=F32))
                y_g = y_inter + jnp.concatenate(parts, axis=1)
                sl = slice(g * gw, (g + 1) * gw)
                if finish:
                    y_ref[bi, :, sl] = yf_ref[bi, :, sl] + y_g + dsum_ref[:, sl] * xs[:, sl]
                else:
                    y_ref[bi, :, sl] = y_g
            st_ref[bi, g] = st * cdec_e[:, g * gw:(g + 1) * gw] + jnp.dot(
                bm_t, xw[:, g * gw:(g + 1) * gw], preferred_element_type=F32)
        if finish:
            gz = y_ref[bi] * _silu(z_ref[bi])
            var = jnp.mean(gz * gz, axis=-1, keepdims=True)
            out_ref[bi] = (gz * lax.rsqrt(var + EPS) * nw_ref[...]).astype(BF16)

    @pl.when(c == pl.num_programs(1) - 1)
    def _():
        hfin_ref[...] = st_ref[...]


def _ssd_scan(xs, bc, bmt, dt, a_row, e_mat, h0, direction, with_y, fin=None):
    b, t, _ = xs.shape
    nb = 2 if b % 2 == 0 else 1
    nc = t // SSM_CHUNK
    q = SSM_CHUNK
    gs = SSM_GROUPS * SSM_STATE
    finish = fin is not None
    if direction == 0:
        tmap = lambda bb, c: (bb, c, 0)
        tmap_t = lambda bb, c: (bb, 0, c)
        zmap = lambda bb, c: (bb, c, COL_Z // SSM_WIDTH)
    else:
        tmap = lambda bb, c: (bb, nc - 1 - c, 0)
        tmap_t = lambda bb, c: (bb, 0, nc - 1 - c)
        zmap = lambda bb, c: (bb, nc - 1 - c, COL_Z // SSM_WIDTH)
    st_spec = pl.BlockSpec((nb, SSM_GROUPS, SSM_STATE, SSM_GROUP_WIDTH), lambda bb, c: (bb, 0, 0, 0))
    in_specs = [
        pl.BlockSpec((nb, q, SSM_WIDTH), tmap),
        pl.BlockSpec((nb, q, SSM_BC), tmap),
        pl.BlockSpec((nb, gs, q), tmap_t),
        pl.BlockSpec((nb, q, LANES), tmap),
        pl.BlockSpec((1, LANES), lambda bb, c: (0, 0)),
        pl.BlockSpec((2 * LANES, SSM_WIDTH), lambda bb, c: (0, 0)),
        st_spec,
    ]
    args = [xs, bc, bmt, dt, a_row, e_mat, h0]
    out_specs, out_shape, scratch = [], [], []
    if finish:
        y_fwd, proj, dsum_e, nw = fin
        in_specs += [
            pl.BlockSpec((nb, q, SSM_WIDTH), tmap),
            pl.BlockSpec((nb, q, SSM_WIDTH), zmap),
            pl.BlockSpec((1, SSM_WIDTH), lambda bb, c: (0, 0)),
            pl.BlockSpec((1, SSM_WIDTH), lambda bb, c: (0, 0)),
        ]
        args += [y_fwd, proj, dsum_e, nw]
    elif with_y:
        out_specs.append(pl.BlockSpec((nb, q, SSM_WIDTH), tmap))
        out_shape.append(jax.ShapeDtypeStruct((b, t, SSM_WIDTH), F32))
    out_specs.append(st_spec)
    out_shape.append(jax.ShapeDtypeStruct((b, SSM_GROUPS, SSM_STATE, SSM_GROUP_WIDTH), F32))
    scratch.append(pltpu.VMEM((nb, SSM_GROUPS, SSM_STATE, SSM_GROUP_WIDTH), F32))
    if finish:
        out_specs.append(pl.BlockSpec((nb, q, SSM_WIDTH), tmap))
        out_shape.append(jax.ShapeDtypeStruct((b, t, SSM_WIDTH), BF16))
    return pl.pallas_call(
        functools.partial(_ssd_kernel_ordered, direction=direction, with_y=with_y, finish=finish, nb=nb),
        grid=(b // nb, nc),
        in_specs=in_specs,
        out_specs=out_specs,
        out_shape=out_shape,
        scratch_shapes=scratch + ([pltpu.VMEM((nb, q, SSM_WIDTH), F32)] if finish else []),
        compiler_params=_cparams(("parallel", "arbitrary")),
        name="ssd_scan",
    )(*args)


def _ssd_kernel_ordered(*refs, direction, with_y, finish, nb):
    n_in = 11 if finish else 7
    ins = list(refs[:n_in])
    rest = list(refs[n_in:])
    if finish:
        hfin, out, st, ybuf = rest
        ordered = ins + [ybuf, hfin, st, out]
    elif with_y:
        y, hfin, st = rest
        ordered = ins + [y, hfin, st]
    else:
        hfin, st = rest
        ordered = ins + [hfin, st]
    _ssd_kernel(*ordered, direction=direction, with_y=with_y, finish=finish, nb=nb)


def _merge_kernel(x_ref, nw_ref, sh_ref, sc_ref, gt_ref, a_ref, b_ref, c_ref, wg0_ref, wg1_ref, wg2_ref,
                  wa_ref, wb_ref, wc_ref, wo_ref, o_ref, h_ref):
    j = pl.program_id(2)

    def contribution(rows, h):
        s = None
        for act_ref, wp_ref, wg_ref in ((a_ref, wa_ref, wg0_ref), (b_ref, wb_ref, wg1_ref), (c_ref, wc_ref, wg2_ref)):
            y = jnp.dot(act_ref[0, rows], wp_ref[...], preferred_element_type=F32)
            gate = _sigmoid(jnp.dot(h, wg_ref[...], preferred_element_type=F32))
            s = gate * y if s is None else s + gate * y
        return jnp.dot(s.astype(BF16), wo_ref[...], preferred_element_type=F32)

    @pl.when(j == 0)
    def _():
        for r in range(0, h_ref.shape[0], ROW_CHUNK):
            rows = slice(r, r + ROW_CHUNK)
            hc = _rms_mod(x_ref[0, rows], nw_ref[...], sh_ref[0], sc_ref[0]).astype(BF16)
            h_ref[rows] = hc
            o_ref[0, rows] = contribution(rows, hc)

    @pl.when(j > 0)
    def _():
        o_ref[0] += contribution(slice(None), h_ref[...])

    @pl.when(j == pl.num_programs(2) - 1)
    def _():
        o_ref[0] = x_ref[0] + gt_ref[0] * o_ref[0]


def _merge(x, nw, shift, scale, gate, act_a, act_b, act_c, wg, wa, wb, wc, wo, l, tm):
    b, t, d = x.shape
    tn = 512
    nj = d // tn
    vec = pl.BlockSpec((1, 1, d), lambda bb, i, j: (bb, 0, 0))
    act = pl.BlockSpec((1, tm, ATTN_WIDTH), lambda bb, i, j: (bb, i, 0))
    wproj = pl.BlockSpec((None, ATTN_WIDTH, tn), lambda bb, i, j: (l, 0, j))
    return pl.pallas_call(
        _merge_kernel,
        grid=(b, t // tm, nj),
        in_specs=[
            pl.BlockSpec((1, tm, d), lambda bb, i, j: (bb, i, 0)),
            pl.BlockSpec((1, d), lambda bb, i, j: (0, 0)),
            vec, vec, vec, act, act, act,
            pl.BlockSpec((None, d, tn), lambda bb, i, j: (l, 0, j)),
            pl.BlockSpec((None, d, tn), lambda bb, i, j: (l, 0, nj + j)),
            pl.BlockSpec((None, d, tn), lambda bb, i, j: (l, 0, 2 * nj + j)),
            wproj, wproj, wproj,
            pl.BlockSpec((None, tn, d), lambda bb, i, j: (l, j, 0)),
        ],
        out_specs=pl.BlockSpec((1, tm, d), lambda bb, i, j: (bb, i, 0)),
        out_shape=jax.ShapeDtypeStruct((b, t, d), F32),
        scratch_shapes=[pltpu.VMEM((tm, d), BF16)],
        compiler_params=_cparams(("parallel", "parallel", "arbitrary")),
        name="branch_merge",
    )(x, nw, shift, scale, gate, act_a, act_b, act_c, wg, wg, wg, wa, wb, wc, wo)


def _ffn_kernel(x_ref, xp_ref, xn_ref, nw_ref, sh_ref, sc_ref, gt_ref, wug_ref, wuv_ref, cwg_ref, cwv_ref,
                cbg_ref, cbv_ref, wd_ref, fw_ref, o_ref, h_ref, pg_ref, pv_ref, *, tm, final):
    i = pl.program_id(1)
    j = pl.program_id(2)
    rows = tm + 2 * HALO

    def conv(p, cw_ref, cb_ref):
        prev = pltpu.roll(p, 1, 0)[HALO:HALO + tm]
        nxt = pltpu.roll(p, rows - 1, 0)[HALO:HALO + tm]
        return prev * cw_ref[0:1, :] + p[HALO:HALO + tm] * cw_ref[1:2, :] + nxt * cw_ref[2:3, :] + cb_ref[...]

    def down(pg, pv):
        s = (_silu(conv(pg, cwg_ref, cbg_ref)) * conv(pv, cwv_ref, cbv_ref)).astype(BF16)
        return jnp.dot(s, wd_ref[...], preferred_element_type=F32)

    @pl.when(j == 0)
    def _():
        nw, sh, sc = nw_ref[...], sh_ref[0], sc_ref[0]
        for r in range(0, tm, ROW_CHUNK):
            hc = _rms_mod(x_ref[0, r:r + ROW_CHUNK], nw, sh, sc).astype(BF16)
            lo, hi = HALO + r, HALO + r + ROW_CHUNK
            if r == 0:
                hp = jnp.where(i > 0, _rms_mod(xp_ref[0], nw, sh, sc), 0.0).astype(BF16)
                hc, lo = jnp.concatenate([hp, hc], axis=0), 0
            if r + ROW_CHUNK == tm:
                hn = jnp.where(i < pl.num_programs(1) - 1, _rms_mod(xn_ref[0], nw, sh, sc), 0.0).astype(BF16)
                hc, hi = jnp.concatenate([hc, hn], axis=0), rows
            h_ref[lo:hi] = hc
            pg_ref[lo:hi] = jnp.dot(hc, wug_ref[...], preferred_element_type=F32)
            pv_ref[lo:hi] = jnp.dot(hc, wuv_ref[...], preferred_element_type=F32)
        o_ref[0] = down(pg_ref[...], pv_ref[...])

    @pl.when(j > 0)
    def _():
        h = h_ref[...]
        o_ref[0] += down(jnp.dot(h, wug_ref[...], preferred_element_type=F32),
                         jnp.dot(h, wuv_ref[...], preferred_element_type=F32))

    @pl.when(j == pl.num_programs(2) - 1)
    def _():
        y = x_ref[0] + gt_ref[0] * o_ref[0]
        if final:
            y = y * lax.rsqrt(jnp.mean(y * y, axis=-1, keepdims=True) + EPS) * fw_ref[...]
        o_ref[0] = y


def _ffn(x, nw, shift, scale, gate, w_up, conv_w, conv_b, w_down, l, tm, final_w=None):
    b, t, d = x.shape
    final = final_w is not None
    if not final:
        final_w = nw
    dff = w_down.shape[1]
    tf = 512
    nj = dff // tf
    per = tm // HALO
    last = t // HALO - 1
    vec = pl.BlockSpec((1, 1, d), lambda bb, i, j: (bb, 0, 0))
    return pl.pallas_call(
        functools.partial(_ffn_kernel, tm=tm, final=final),
        grid=(b, t // tm, nj),
        in_specs=[
            pl.BlockSpec((1, tm, d), lambda bb, i, j: (bb, i, 0)),
            pl.BlockSpec((1, HALO, d), lambda bb, i, j: (bb, jnp.maximum(i * per - 1, 0), 0)),
            pl.BlockSpec((1, HALO, d), lambda bb, i, j: (bb, jnp.minimum((i + 1) * per, last), 0)),
            pl.BlockSpec((1, d), lambda bb, i, j: (0, 0)),
            vec, vec, vec,
            pl.BlockSpec((None, d, tf), lambda bb, i, j: (l, 0, j)),
            pl.BlockSpec((None, d, tf), lambda bb, i, j: (l, 0, nj + j)),
            pl.BlockSpec((FFN_CONV, tf), lambda bb, i, j: (0, j)),
            pl.BlockSpec((FFN_CONV, tf), lambda bb, i, j: (0, nj + j)),
            pl.BlockSpec((1, tf), lambda bb, i, j: (0, j)),
            pl.BlockSpec((1, tf), lambda bb, i, j: (0, nj + j)),
            pl.BlockSpec((None, tf, d), lambda bb, i, j: (l, j, 0)),
            pl.BlockSpec((1, d), lambda bb, i, j: (0, 0)),
        ],
        out_specs=pl.BlockSpec((1, tm, d), lambda bb, i, j: (bb, i, 0), pipeline_mode=pl.Buffered(1)),
        out_shape=jax.ShapeDtypeStruct((b, t, d), F32),
        scratch_shapes=[
            pltpu.VMEM((tm + 2 * HALO, d), BF16),
            pltpu.VMEM((tm + 2 * HALO, tf), F32),
            pltpu.VMEM((tm + 2 * HALO, tf), F32),
        ],
        compiler_params=_cparams(("parallel", "parallel", "arbitrary")),
        name="conv_ffn",
    )(x, x, x, nw, shift, scale, gate, w_up, w_up, conv_w, conv_w, conv_b, conv_b, w_down, final_w)


def _rope_tables(n_tokens):
    rows = n_tokens // GRID_W
    row = jnp.repeat(jnp.arange(rows), GRID_W).astype(F32)
    col = jnp.tile(jnp.arange(GRID_W), rows).astype(F32)
    inv = ROPE_THETA ** (-jnp.arange(0, ROPE_AXIS_DIM, 2, dtype=F32) / ROPE_AXIS_DIM)
    ar, ac = row[:, None] * inv, col[:, None] * inv
    cos = jnp.concatenate([jnp.cos(ar), jnp.cos(ar), jnp.cos(ac), jnp.cos(ac)], axis=1)
    sin = jnp.concatenate([-jnp.sin(ar), jnp.sin(ar), -jnp.sin(ac), jnp.sin(ac)], axis=1)
    return cos, sin


def _expansion_matrix(direction):
    head_of_col = jnp.arange(SSM_WIDTH) // SSM_HEAD_DIM
    row = jnp.arange(LANES)[:, None]
    e = (row == head_of_col[None, :] + SSM_HEADS * direction).astype(BF16)
    return jnp.concatenate([e, e], axis=0)


def _pad_lanes(v, offset):
    return jnp.zeros((1, LANES), F32).at[0, offset:offset + v.shape[0]].set(v.astype(F32))


def kernel(x, c, ctx, c_ctx, w_mod, b_mod, norm_mix_w, norm_ffn_w, w_in, q_norm_w, k_norm_w, w_attn_o, conf_conv_w, conf_conv_b, conf_ln_w, conf_ln_b, w_conf_o, ssm_conv_w, ssm_conv_b, ssm_a_log, ssm_dt_bias, ssm_d, ssm_norm_w, w_ssm_o, w_out, ffn_w_up, ffn_conv_w, ffn_conv_b, ffn_w_down, final_norm_w):
    bsz, n_lat, d = x.shape
    n_ctx = ctx.shape[1]
    depth = w_mod.shape[0]
    assert bsz <= 7 and n_lat % 512 == 0 and n_ctx % 256 == 0

    q_end = ATTN_WIDTH
    v_end = q_end + 2 * KV_WIDTH
    conf_end = v_end + 2 * CONF_WIDTH
    z_end = conf_end + SSM_WIDTH
    xbc_end = z_end + SSM_WIDTH + SSM_BC
    dt_end = xbc_end + 2 * SSM_HEADS

    cc = jnp.zeros((8, d), F32).at[:bsz].set(c).at[bsz].set(c_ctx)
    mods = _mod_vectors(cc, w_mod, b_mod).reshape(depth, 8, N_MOD, d)
    rope = _rope_tables(n_lat)
    e_mats = [_expansion_matrix(0), _expansion_matrix(1)]
    zero_state = jnp.zeros((bsz, SSM_GROUPS, SSM_STATE, SSM_GROUP_WIDTH), F32)

    w_mix = jnp.concatenate([
        w_in[:, :, :q_end], w_in[:, :, v_end:conf_end], w_in[:, :, conf_end:z_end],
        w_in[:, :, z_end:z_end + SSM_WIDTH], w_in[:, :, q_end:v_end], w_in[:, :, z_end + SSM_WIDTH:xbc_end],
        w_in[:, :, xbc_end:dt_end], jnp.zeros((depth, d, MIX_WIDTH - dt_end), F32)], axis=2).astype(BF16)
    w_gate = w_in[:, :, dt_end:].astype(BF16)
    wa, wb, wc, wo = (w.astype(BF16) for w in (w_attn_o, w_conf_o, w_ssm_o, w_out))
    w_up, w_down = ffn_w_up.astype(BF16), ffn_w_down.astype(BF16)

    for l in range(depth):
        need_ctx = l < depth - 1
        row = lambda v: v.reshape(1, -1)
        nmix, nffn = row(norm_mix_w[l]), row(norm_ffn_w[l])
        cwx, cwb = ssm_conv_w[l][:, :SSM_WIDTH], ssm_conv_w[l][:, SSM_WIDTH:]
        cbx, cbb = row(ssm_conv_b[l][:SSM_WIDTH]), row(ssm_conv_b[l][SSM_WIDTH:])
        a_coef = -jnp.exp(ssm_a_log[l].astype(F32))
        a_rows = [_pad_lanes(a_coef[0], 0), _pad_lanes(a_coef[1], SSM_HEADS)]
        dtb = _pad_lanes(ssm_dt_bias[l].reshape(-1), 0)
        dsum_e = row(jnp.repeat(ssm_d[l][0] + ssm_d[l][1], SSM_HEAD_DIM))
        m_lat = [mods[l, :bsz, k].reshape(bsz, 1, d) for k in range(N_MOD)]
        m_ctx = [jnp.broadcast_to(mods[l, bsz, k].reshape(1, 1, d), (bsz, 1, d)) for k in range(N_MOD)]

        proj_l = _inproj(x, nmix, m_lat[0], m_lat[1], w_mix, l, 1024 if n_lat % 1024 == 0 else 512)
        proj_c = _inproj(ctx, nmix, m_ctx[0], m_ctx[1], w_mix, l, 256)

        q_l, k_l, v_l = _qkprep(proj_l, row(q_norm_w[l]), row(k_norm_w[l]), rope, 256)
        q_c, k_c, v_c = _qkprep(proj_c, row(q_norm_w[l]), row(k_norm_w[l]), None, 256)
        act_a_l = _attention(q_l, [(k_c, v_c), (k_l, v_l)], 512, 256)
        conf_args = (conf_conv_w[l], row(conf_conv_b[l]), row(conf_ln_w[l]), row(conf_ln_b[l]))
        act_b_l = _conformer(proj_l, *conf_args, 256)

        prep_l = _ssdprep(proj_l, cwx, cbx, cwb, cbb, dtb, 256)
        prep_c = _ssdprep(proj_c, cwx, cbx, cwb, cbb, dtb, 256)
        fin_args = (dsum_e, row(ssm_norm_w[l]))
        if need_ctx:
            y_cf, h_cf = _ssd_scan(*prep_c, a_rows[0], e_mats[0], zero_state, 0, True)
            h_cb, act_c_c = _ssd_scan(*prep_c, a_rows[1], e_mats[1], zero_state, 1, True, (y_cf, proj_c) + fin_args)
        else:
            (h_cf,) = _ssd_scan(*prep_c, a_rows[0], e_mats[0], zero_state, 0, False)
            (h_cb,) = _ssd_scan(*prep_c, a_rows[1], e_mats[1], zero_state, 1, False)
        y_lf, _ = _ssd_scan(*prep_l, a_rows[0], e_mats[0], h_cf, 0, True)
        _, act_c_l = _ssd_scan(*prep_l, a_rows[1], e_mats[1], h_cb, 1, True, (y_lf, proj_l) + fin_args)

        x = _merge(x, nmix, m_lat[0], m_lat[1], m_lat[2], act_a_l, act_b_l, act_c_l, w_gate, wa, wb, wc, wo, l, 512)
        ffn_args = (w_up, ffn_conv_w[l], row(ffn_conv_b[l]), w_down, l)
        x = _ffn(x, nffn, m_lat[3], m_lat[4], m_lat[5], *ffn_args, 1024 if n_lat % 1024 == 0 else 512,
                 final_w=None if need_ctx else final_norm_w.reshape(1, d))
        if need_ctx:
            act_a_c = _attention(q_c, [(k_c, v_c)], 256, 256)
            act_b_c = _conformer(proj_c, *conf_args, 256)
            ctx = _merge(ctx, nmix, m_ctx[0], m_ctx[1], m_ctx[2], act_a_c, act_b_c, act_c_c, w_gate, wa, wb, wc, wo, l, 256)
            ctx = _ffn(ctx, nffn, m_ctx[3], m_ctx[4], m_ctx[5], *ffn_args, 256)
    return x
```

```python
import functools
import math

import jax
import jax.numpy as jnp
from jax import lax
from jax.experimental import pallas as pl
from jax.experimental.pallas import tpu as pltpu

F32 = jnp.float32
BF16 = jnp.bfloat16

EPS = 1e-6
N_MOD = 6
GRID_W = 64
HEAD_DIM = 128
ATTN_HEADS = 8
ATTN_KV_HEADS = 2
ATTN_REP = ATTN_HEADS // ATTN_KV_HEADS
ATTN_WIDTH = ATTN_HEADS * HEAD_DIM
KV_WIDTH = ATTN_KV_HEADS * HEAD_DIM
ROPE_THETA = 10000.0
ROPE_AXIS_DIM = HEAD_DIM // 2
ROPE_FREQS = ROPE_AXIS_DIM // 2
CONF_WIDTH = 1024
CONF_KERNEL = 31
SSM_HEADS = 16
SSM_HEAD_DIM = 64
SSM_WIDTH = SSM_HEADS * SSM_HEAD_DIM
SSM_GROUPS = 2
SSM_STATE = 128
SSM_GROUP_WIDTH = SSM_WIDTH // SSM_GROUPS
SSM_BC = 2 * SSM_GROUPS * SSM_STATE
SSM_CONV = 5
SSM_CHUNK = 128
FFN_CONV = 3
N_BRANCHES = 3

LANES = 128
HALO = 16
VMEM_LIMIT = 56 * 1024 * 1024
LOG2E = 1.4426950408889634
ATTN_QSCALE = HEAD_DIM ** -0.5 * LOG2E
VT_PAD = 16
VT_ROWS = HEAD_DIM + VT_PAD
NEG_BIG = -1e30

COL_Q = 0
COL_A = 1024
COL_G = 2048
COL_Z = 3072
COL_XS = 4096
COL_KV = 5120
COL_BC = 5632
COL_DT = 6144
MIX_WIDTH = 6400


def _cparams(sem):
    return pltpu.CompilerParams(dimension_semantics=sem, vmem_limit_bytes=VMEM_LIMIT)


def _sigmoid(x):
    return 1.0 / (1.0 + jnp.exp(-x))


def _silu(x):
    return x * _sigmoid(x)


def _rms_mod(x, nw, shift, scale):
    var = jnp.mean(x * x, axis=-1, keepdims=True)
    y = x * lax.rsqrt(var + EPS) * nw
    return y * (1.0 + scale) + shift


def _mod_kernel(c_ref, w_ref, b_ref, o_ref):
    s = _silu(c_ref[...]).astype(BF16)
    o_ref[0] = jnp.dot(s, w_ref[0].astype(BF16), preferred_element_type=F32) + b_ref[0]


def _mod_vectors(cc, w_mod, b_mod):
    depth, d, nm = w_mod.shape
    tn = 2048
    return pl.pallas_call(
        _mod_kernel,
        grid=(depth, nm // tn),
        in_specs=[
            pl.BlockSpec((8, d), lambda l, j: (0, 0)),
            pl.BlockSpec((1, d, tn), lambda l, j: (l, 0, j)),
            pl.BlockSpec((1, 1, tn), lambda l, j: (l, 0, j)),
        ],
        out_specs=pl.BlockSpec((1, 8, tn), lambda l, j: (l, 0, j)),
        out_shape=jax.ShapeDtypeStruct((depth, 8, nm), F32),
        compiler_params=_cparams(("parallel", "parallel")),
        name="mod_vectors",
    )(cc, w_mod, b_mod.reshape(depth, 1, nm))


ROW_CHUNK = 256


MXU_COLS = 256
INPROJ_TILES = 5


def _source_tile(t):
    kv_tiles = 2 * KV_WIDTH // MXU_COLS
    q_tiles = ATTN_WIDTH // MXU_COLS
    moved = (COL_KV - COL_A) // MXU_COLS
    return jnp.where(t < q_tiles, t,
                     jnp.where(t < q_tiles + moved, t + kv_tiles,
                               jnp.where(t < q_tiles + moved + kv_tiles, t - moved, t)))


def _inproj_kernel(x_ref, nw_ref, sh_ref, sc_ref, *rest):
    w_refs, (o_ref, h_ref) = rest[:INPROJ_TILES], rest[INPROJ_TILES:]
    j = pl.program_id(2)

    def project(rows, h):
        for r, w_ref in enumerate(w_refs):
            o_ref[0, rows, r * MXU_COLS:(r + 1) * MXU_COLS] = jnp.dot(h, w_ref[...], preferred_element_type=F32)

    @pl.when(j == 0)
    def _():
        for r in range(0, h_ref.shape[0], ROW_CHUNK):
            rows = slice(r, r + ROW_CHUNK)
            hc = _rms_mod(x_ref[0, rows], nw_ref[...], sh_ref[0], sc_ref[0]).astype(BF16)
            h_ref[rows] = hc
            project(rows, hc)

    @pl.when(j > 0)
    def _():
        project(slice(None), h_ref[...])


def _inproj(x, nw, shift, scale, w, l, tm):
    b, t, d = x.shape
    tn = INPROJ_TILES * MXU_COLS
    w_specs = [pl.BlockSpec((None, d, MXU_COLS), lambda bb, i, j, r=r: (l, 0, _source_tile(j * INPROJ_TILES + r)))
               for r in range(INPROJ_TILES)]
    return pl.pallas_call(
        _inproj_kernel,
        grid=(b, t // tm, MIX_WIDTH // tn),
        in_specs=[
            pl.BlockSpec((1, tm, d), lambda bb, i, j: (bb, i, 0)),
            pl.BlockSpec((1, d), lambda bb, i, j: (0, 0)),
            pl.BlockSpec((1, 1, d), lambda bb, i, j: (bb, 0, 0)),
            pl.BlockSpec((1, 1, d), lambda bb, i, j: (bb, 0, 0)),
        ] + w_specs,
        out_specs=pl.BlockSpec((1, tm, tn), lambda bb, i, j: (bb, i, j)),
        out_shape=jax.ShapeDtypeStruct((b, t, MIX_WIDTH), F32),
        scratch_shapes=[pltpu.VMEM((tm, d), BF16)],
        compiler_params=_cparams(("parallel", "parallel", "arbitrary")),
        name="mixer_inproj",
    )(x, nw, shift, scale, *([w] * INPROJ_TILES))


def _head_norm(x, w):
    var = jnp.mean(x * x, axis=-1, keepdims=True)
    return x * lax.rsqrt(var + EPS) * w


def _rope(x, c, s):
    lane = lax.broadcasted_iota(jnp.int32, x.shape, 1)
    first = (lane % ROPE_AXIS_DIM) < ROPE_FREQS
    partner = jnp.where(first, pltpu.roll(x, HEAD_DIM - ROPE_FREQS, 1), pltpu.roll(x, ROPE_FREQS, 1))
    return x * c + partner * s


def _qkprep_kernel(*refs, use_rope):
    if use_rope:
        q_ref, kv_ref, qw_ref, kw_ref, c_ref, s_ref, qo_ref, ko_ref, vo_ref = refs
    else:
        q_ref, kv_ref, qw_ref, kw_ref, qo_ref, ko_ref, vo_ref = refs
    q = q_ref[0]
    kv = kv_ref[0]
    for h in range(ATTN_HEADS):
        y = _head_norm(q[:, h * HEAD_DIM:(h + 1) * HEAD_DIM], qw_ref[...])
        if use_rope:
            y = _rope(y, c_ref[...], s_ref[...])
        qo_ref[0, :, h * HEAD_DIM:(h + 1) * HEAD_DIM] = (y * ATTN_QSCALE).astype(BF16)
    for h in range(ATTN_KV_HEADS):
        y = _head_norm(kv[:, h * HEAD_DIM:(h + 1) * HEAD_DIM], kw_ref[...])
        if use_rope:
            y = _rope(y, c_ref[...], s_ref[...])
        ko_ref[0, :, h * HEAD_DIM:(h + 1) * HEAD_DIM] = y.astype(BF16)
    vt = kv[:, KV_WIDTH:].T.astype(BF16)
    ones = jnp.ones((VT_PAD, vt.shape[1]), BF16)
    for g in range(ATTN_KV_HEADS):
        vo_ref[0, g * VT_ROWS:g * VT_ROWS + HEAD_DIM] = vt[g * HEAD_DIM:(g + 1) * HEAD_DIM]
        vo_ref[0, g * VT_ROWS + HEAD_DIM:(g + 1) * VT_ROWS] = ones


def _qkprep(proj, qw, kw, rope, tm):
    b, t, _ = proj.shape
    use_rope = rope is not None
    in_specs = [
        pl.BlockSpec((1, tm, ATTN_WIDTH), lambda bb, i: (bb, i, COL_Q // ATTN_WIDTH)),
        pl.BlockSpec((1, tm, 2 * KV_WIDTH), lambda bb, i: (bb, i, COL_KV // (2 * KV_WIDTH))),
        pl.BlockSpec((1, HEAD_DIM), lambda bb, i: (0, 0)),
        pl.BlockSpec((1, HEAD_DIM), lambda bb, i: (0, 0)),
    ]
    args = [proj, proj, qw, kw]
    if use_rope:
        in_specs += [pl.BlockSpec((tm, HEAD_DIM), lambda bb, i: (i, 0))] * 2
        args += list(rope)
    return pl.pallas_call(
        functools.partial(_qkprep_kernel, use_rope=use_rope),
        grid=(b, t // tm),
        in_specs=in_specs,
        out_specs=[
            pl.BlockSpec((1, tm, ATTN_WIDTH), lambda bb, i: (bb, i, 0)),
            pl.BlockSpec((1, tm, KV_WIDTH), lambda bb, i: (bb, i, 0)),
            pl.BlockSpec((1, ATTN_KV_HEADS * VT_ROWS, tm), lambda bb, i: (bb, 0, i)),
        ],
        out_shape=[
            jax.ShapeDtypeStruct((b, t, ATTN_WIDTH), BF16),
            jax.ShapeDtypeStruct((b, t, KV_WIDTH), BF16),
            jax.ShapeDtypeStruct((b, ATTN_KV_HEADS * VT_ROWS, t), BF16),
        ],
        compiler_params=_cparams(("parallel", "parallel")),
        name="qk_prep",
    )(*args)


def _attn_kernel(*refs, tq, seg_tk, seg_chunks):
    n_seg = len(seg_chunks)
    q_ref = refs[0]
    kv_refs = refs[1:1 + 2 * n_seg]
    o_ref, s0_ref, s1_ref, acc_ref = refs[1 + 2 * n_seg:]
    q = q_ref[0]
    qs = jnp.concatenate([q[:, r * HEAD_DIM:(r + 1) * HEAD_DIM] for r in range(ATTN_REP)], axis=0)
    nq = ATTN_REP * tq
    chunks = [(kv_refs[2 * s], kv_refs[2 * s + 1], slice(c * seg_tk[s], (c + 1) * seg_tk[s]))
              for s in range(n_seg) for c in range(seg_chunks[s])]
    s_refs = (s0_ref, s1_ref)

    def scores(j):
        k_ref, _, rows = chunks[j]
        s_refs[j % 2][0:rows.stop - rows.start, :] = lax.dot_general(
            k_ref[0, rows, :], qs, (((1,), (1,)), ((), ())), preferred_element_type=F32)

    def update(j, m):
        _, vt_ref, cols = chunks[j]
        s = s_refs[j % 2][0:cols.stop - cols.start, :]
        m_new = jnp.maximum(m, jnp.max(s, axis=0, keepdims=True))
        alpha = jnp.exp2(m - m_new)
        p = jnp.exp2(s - m_new).astype(BF16)
        acc_ref[...] = alpha * acc_ref[...] + jnp.dot(vt_ref[0, :, cols], p, preferred_element_type=F32)
        return m_new

    scores(0)
    acc_ref[...] = jnp.zeros_like(acc_ref)
    m = jnp.full((1, nq), NEG_BIG, F32)
    for j in range(len(chunks)):
        if j + 1 < len(chunks):
            scores(j + 1)
        m = update(j, m)
    o = acc_ref[:HEAD_DIM] / acc_ref[HEAD_DIM:HEAD_DIM + 1]
    for r in range(ATTN_REP):
        o_ref[0, :, r * HEAD_DIM:(r + 1) * HEAD_DIM] = o[:, r * tq:(r + 1) * tq].T.astype(BF16)


def _attention(q, kv_segments, tq, tks):
    b, t, _ = q.shape
    gw = ATTN_REP * HEAD_DIM
    nq = ATTN_REP * tq
    in_specs = [pl.BlockSpec((1, tq, gw), lambda bb, g, i: (bb, i, g))]
    args = [q]
    for k, vt in kv_segments:
        tkv = k.shape[1]
        in_specs += [
            pl.BlockSpec((1, tkv, HEAD_DIM), lambda bb, g, i: (bb, 0, g)),
            pl.BlockSpec((1, VT_ROWS, tkv), lambda bb, g, i: (bb, g, 0)),
        ]
        args += [k, vt]
    return pl.pallas_call(
        functools.partial(_attn_kernel, tq=tq, seg_tk=tuple(tks),
                          seg_chunks=tuple(k.shape[1] // tk for (k, _), tk in zip(kv_segments, tks))),
        grid=(b, ATTN_KV_HEADS, t // tq),
        in_specs=in_specs,
        out_specs=pl.BlockSpec((1, tq, gw), lambda bb, g, i: (bb, i, g)),
        out_shape=jax.ShapeDtypeStruct((b, t, ATTN_WIDTH), BF16),
        scratch_shapes=[pltpu.VMEM((max(tks), nq), F32), pltpu.VMEM((max(tks), nq), F32),
                        pltpu.VMEM((VT_ROWS, nq), F32)],
        compiler_params=_cparams(("parallel", "parallel", "arbitrary")),
        name="attention",
    )(*args)


def _halo_specs(tm, t, width, col_block):
    per = tm // HALO
    last = t // HALO - 1
    return [
        pl.BlockSpec((1, tm, width), lambda bb, i: (bb, i, col_block)),
        pl.BlockSpec((1, HALO, width), lambda bb, i: (bb, jnp.maximum(i * per - 1, 0), col_block)),
        pl.BlockSpec((1, HALO, width), lambda bb, i: (bb, jnp.minimum((i + 1) * per, last), col_block)),
    ]


SUBLANES = 8


def _conv_rows(ext_ref, w_ref, taps, first_row, tm, width, out_fn, shift_ref=None, rows_per=64):
    if shift_ref is not None:
        n_shift = shift_ref.shape[1]
        for c0 in range(0, width, LANES):
            for r in range(1, SUBLANES):
                shift_ref[r - 1, :, c0:c0 + LANES] = ext_ref[r:r + n_shift, c0:c0 + LANES]
    for c0 in range(0, width, LANES):
        for r0 in range(0, tm, rows_per):
            acc = None
            for k in range(taps):
                lo = first_row + r0 + k
                res = lo % SUBLANES
                if shift_ref is None or res == 0:
                    rows = ext_ref[lo:lo + rows_per, c0:c0 + LANES]
                else:
                    rows = shift_ref[res - 1, lo - res:lo - res + rows_per, c0:c0 + LANES]
                term = rows * w_ref[k:k + 1, c0:c0 + LANES]
                acc = term if acc is None else acc + term
            out_fn(r0, rows_per, c0, acc)


def _conf_kernel(a_ref, ap_ref, an_ref, g_ref, gp_ref, gn_ref, cw_ref, cb_ref, lw_ref, lb_ref,
                 o_ref, u_ref, v_ref, us_ref, *, tm):
    i = pl.program_id(1)
    nt = pl.num_programs(1)
    u_ref[HALO:HALO + tm] = a_ref[0] * _sigmoid(g_ref[0])
    u_ref[0:HALO] = jnp.where(i > 0, ap_ref[0] * _sigmoid(gp_ref[0]), 0.0)
    u_ref[HALO + tm:] = jnp.where(i < nt - 1, an_ref[0] * _sigmoid(gn_ref[0]), 0.0)

    def put(r0, nr, c0, acc):
        v_ref[r0:r0 + nr, c0:c0 + LANES] = acc + cb_ref[:, c0:c0 + LANES]

    pad = (CONF_KERNEL - 1) // 2
    _conv_rows(u_ref, cw_ref, CONF_KERNEL, HALO - pad, tm, CONF_WIDTH, put, shift_ref=us_ref)
    v = v_ref[...]
    mu = jnp.mean(v, axis=-1, keepdims=True)
    d = v - mu
    var = jnp.mean(d * d, axis=-1, keepdims=True)
    y = d * lax.rsqrt(var + EPS) * lw_ref[...] + lb_ref[...]
    o_ref[0] = _silu(y).astype(BF16)


def _conformer(proj, cw, cb, lw, lb, tm):
    b, t, _ = proj.shape
    vec = pl.BlockSpec((1, CONF_WIDTH), lambda bb, i: (0, 0))
    return pl.pallas_call(
        functools.partial(_conf_kernel, tm=tm),
        grid=(b, t // tm),
        in_specs=_halo_specs(tm, t, CONF_WIDTH, COL_A // CONF_WIDTH)
        + _halo_specs(tm, t, CONF_WIDTH, COL_G // CONF_WIDTH)
        + [pl.BlockSpec((CONF_KERNEL, CONF_WIDTH), lambda bb, i: (0, 0)), vec, vec, vec],
        out_specs=pl.BlockSpec((1, tm, CONF_WIDTH), lambda bb, i: (bb, i, 0)),
        out_shape=jax.ShapeDtypeStruct((b, t, CONF_WIDTH), BF16),
        scratch_shapes=[
            pltpu.VMEM((tm + 2 * HALO, CONF_WIDTH), F32),
            pltpu.VMEM((tm, CONF_WIDTH), F32),
            pltpu.VMEM((SUBLANES - 1, tm + 2 * HALO - SUBLANES, CONF_WIDTH), F32),
        ],
        compiler_params=_cparams(("parallel", "parallel")),
        name="conformer",
    )(proj, proj, proj, proj, proj, proj, cw, cb, lw, lb)


def _ssdprep_kernel(x_ref, xp_ref, xn_ref, bc_ref, bcp_ref, bcn_ref, dt_ref, cwx_ref, cbx_ref,
                    cwb_ref, cbb_ref, dtb_ref, xo_ref, bco_ref, bmt_ref, dto_ref, xe_ref, be_ref, bv_ref, *, tm):
    i = pl.program_id(1)
    nt = pl.num_programs(1)
    xe_ref[HALO:HALO + tm] = x_ref[0]
    xe_ref[0:HALO] = jnp.where(i > 0, xp_ref[0], 0.0)
    xe_ref[HALO + tm:] = jnp.where(i < nt - 1, xn_ref[0], 0.0)
    be_ref[HALO:HALO + tm] = bc_ref[0]
    be_ref[0:HALO] = jnp.where(i > 0, bcp_ref[0], 0.0)
    be_ref[HALO + tm:] = jnp.where(i < nt - 1, bcn_ref[0], 0.0)
    pad = (SSM_CONV - 1) // 2

    def put_x(r0, nr, c0, acc):
        xo_ref[0, r0:r0 + nr, c0:c0 + LANES] = _silu(acc + cbx_ref[:, c0:c0 + LANES])

    def put_b(r0, nr, c0, acc):
        bv_ref[r0:r0 + nr, c0:c0 + LANES] = _silu(acc + cbb_ref[:, c0:c0 + LANES])

    _conv_rows(xe_ref, cwx_ref, SSM_CONV, HALO - pad, tm, SSM_WIDTH, put_x)
    _conv_rows(be_ref, cwb_ref, SSM_CONV, HALO - pad, tm, SSM_BC, put_b)
    bv = bv_ref[...]
    bco_ref[0] = bv.astype(BF16)
    bmt_ref[0] = bv[:, :SSM_GROUPS * SSM_STATE].T.astype(BF16)
    x = dt_ref[0] + dtb_ref[...]
    softplus = jnp.maximum(x, 0.0) + jnp.log(1.0 + jnp.exp(-jnp.abs(x)))
    lane = lax.broadcasted_iota(jnp.int32, x.shape, 1)
    dto_ref[0] = jnp.where(lane < 2 * SSM_HEADS, softplus, 0.0)


def _ssdprep(proj, cwx, cbx, cwb, cbb, dtb, tm):
    b, t, _ = proj.shape
    gs = SSM_GROUPS * SSM_STATE
    return pl.pallas_call(
        functools.partial(_ssdprep_kernel, tm=tm),
        grid=(b, t // tm),
        in_specs=_halo_specs(tm, t, SSM_WIDTH, COL_XS // SSM_WIDTH)
        + _halo_specs(tm, t, SSM_BC, COL_BC // SSM_BC)
        + [
            pl.BlockSpec((1, tm, LANES), lambda bb, i: (bb, i, COL_DT // LANES)),
            pl.BlockSpec((SSM_CONV, SSM_WIDTH), lambda bb, i: (0, 0)),
            pl.BlockSpec((1, SSM_WIDTH), lambda bb, i: (0, 0)),
            pl.BlockSpec((SSM_CONV, SSM_BC), lambda bb, i: (0, 0)),
            pl.BlockSpec((1, SSM_BC), lambda bb, i: (0, 0)),
            pl.BlockSpec((1, LANES), lambda bb, i: (0, 0)),
        ],
        out_specs=[
            pl.BlockSpec((1, tm, SSM_WIDTH), lambda bb, i: (bb, i, 0)),
            pl.BlockSpec((1, tm, SSM_BC), lambda bb, i: (bb, i, 0)),
            pl.BlockSpec((1, gs, tm), lambda bb, i: (bb, 0, i)),
            pl.BlockSpec((1, tm, LANES), lambda bb, i: (bb, i, 0)),
        ],
        out_shape=[
            jax.ShapeDtypeStruct((b, t, SSM_WIDTH), F32),
            jax.ShapeDtypeStruct((b, t, SSM_BC), BF16),
            jax.ShapeDtypeStruct((b, gs, t), BF16),
            jax.ShapeDtypeStruct((b, t, LANES), F32),
        ],
        scratch_shapes=[
            pltpu.VMEM((tm + 2 * HALO, SSM_WIDTH), F32),
            pltpu.VMEM((tm + 2 * HALO, SSM_BC), F32),
            pltpu.VMEM((tm, SSM_BC), F32),
        ],
        compiler_params=_cparams(("parallel", "parallel")),
        name="ssd_prep",
    )(proj, proj, proj, proj, proj, proj, proj, cwx, cbx, cwb, cbb, dtb)


def _split3(v, axis):
    hi = v.astype(BF16)
    r1 = v - hi.astype(F32)
    mid = r1.astype(BF16)
    lo = (r1 - mid.astype(F32)).astype(BF16)
    return jnp.concatenate([hi, mid, lo], axis=axis)


def _split_hi_lo(v):
    hi = v.astype(BF16)
    lo = (v - hi.astype(F32)).astype(BF16)
    return jnp.concatenate([hi, lo], axis=1)


def _ssd_kernel(*refs, direction, with_y, finish, nb):
    it = iter(refs)
    xs_ref, bc_ref, bmt_ref, dt_ref, a_ref, e_ref, h0_ref = (next(it) for _ in range(7))
    if finish:
        yf_ref, z_ref, dsum_ref, nw_ref = (next(it) for _ in range(4))
    if with_y:
        y_ref = next(it)
    hfin_ref = next(it)
    st_ref = next(it)
    if finish:
        out_ref = next(it)
    q = SSM_CHUNK
    c = pl.program_id(1)

    @pl.when(c == 0)
    def _():
        st_ref[...] = h0_ref[...]

    ii = lax.broadcasted_iota(jnp.int32, (q, q), 0)
    jj = lax.broadcasted_iota(jnp.int32, (q, q), 1)
    if direction == 0:
        valid, valid_t, end_row = jj <= ii, ii <= jj, q - 1
    else:
        valid, valid_t, end_row = jj >= ii, ii >= jj, 0
    tri = jnp.where(valid, 1.0, 0.0).astype(BF16)
    tri_t = jnp.where(valid_t, 1.0, 0.0).astype(BF16)
    tri3 = jnp.concatenate([tri, tri, tri], axis=1)
    tri3_t = jnp.concatenate([tri_t, tri_t, tri_t], axis=0)
    e = e_ref[...]

    def expand(v):
        return jnp.dot(_split_hi_lo(v), e, preferred_element_type=F32)

    gw = SSM_GROUP_WIDTH
    hpg = SSM_HEADS // SSM_GROUPS
    for bi in range(nb):
        dt = dt_ref[bi]
        da = dt * a_ref[...]
        acum_c = jnp.dot(tri3, _split3(da, 0), preferred_element_type=F32)
        acum_r = jnp.dot(_split3(da.T, 1), tri3_t, preferred_element_type=F32)
        tot = acum_c[end_row:end_row + 1, :]
        to_end = jnp.exp(tot - acum_c)
        ea = jnp.exp(acum_c)
        ea_e = expand(ea)
        xs = xs_ref[bi]
        xw = (xs * expand(dt * to_end)).astype(BF16)
        cdec_e = ea_e[end_row:end_row + 1, :]
        bc = bc_ref[bi]
        bmt = bmt_ref[bi]
        if with_y:
            xdt = (xs * expand(dt)).astype(BF16)
            lane = lax.broadcasted_iota(jnp.int32, xdt.shape, 1)
            left = (lane % LANES) < SSM_HEAD_DIM
            zero = jnp.zeros_like(xdt)
            xdt_l = jnp.where(left, xdt, zero)
            xdt_r = jnp.where(left, zero, xdt)
        for g in range(SSM_GROUPS):
            bm_t = bmt[g * SSM_STATE:(g + 1) * SSM_STATE, :]
            st = st_ref[bi, g]
            if with_y:
                cm = bc[:, (SSM_GROUPS + g) * SSM_STATE:(SSM_GROUPS + g + 1) * SSM_STATE]
                bm = bc[:, g * SSM_STATE:(g + 1) * SSM_STATE]
                cb = lax.dot_general(cm, bm, (((1,), (1,)), ((), ())), preferred_element_type=F32)
                y_inter = jnp.dot(cm, st.astype(BF16), preferred_element_type=F32) * ea_e[:, g * gw:(g + 1) * gw]
                parts = []
                for k in range(hpg // 2):
                    ms = []
                    for r in (2 * k, 2 * k + 1):
                        hl = SSM_HEADS * direction + g * hpg + r
                        seg = acum_c[:, hl:hl + 1] - acum_r[hl:hl + 1, :]
                        ms.append((cb * jnp.where(valid, jnp.exp(seg), 0.0)).astype(BF16))
                    c0 = g * gw + k * LANES
                    rhs = jnp.concatenate([xdt_l[:, c0:c0 + LANES], xdt_r[:, c0:c0 + LANES]], axis=0)
                    parts.append(jnp.dot(jnp.concatenate(ms, axis=1), rhs, preferred_element_type=F32))
                y_g = y_inter + jnp.concatenate(parts, axis=1)
                sl = slice(g * gw, (g + 1) * gw)
                if finish:
                    y_ref[bi, :, sl] = yf_ref[bi, :, sl] + y_g + dsum_ref[:, sl] * xs[:, sl]
                else:
                    y_ref[bi, :, sl] = y_g
            st_ref[bi, g] = st * cdec_e[:, g * gw:(g + 1) * gw] + jnp.dot(
                bm_t, xw[:, g * gw:(g + 1) * gw], preferred_element_type=F32)
        if finish:
            gz = y_ref[bi] * _silu(z_ref[bi])
            var = jnp.mean(gz * gz, axis=-1, keepdims=True)
            out_ref[bi] = (gz * lax.rsqrt(var + EPS) * nw_ref[...]).astype(BF16)

    @pl.when(c == pl.num_programs(1) - 1)
    def _():
        hfin_ref[...] = st_ref[...]


def _ssd_scan(xs, bc, bmt, dt, a_row, e_mat, h0, direction, with_y, fin=None):
    b, t, _ = xs.shape
    nb = 2 if b % 2 == 0 else 1
    nc = t // SSM_CHUNK
    q = SSM_CHUNK
    gs = SSM_GROUPS * SSM_STATE
    finish = fin is not None
    if direction == 0:
        tmap = lambda bb, c: (bb, c, 0)
        tmap_t = lambda bb, c: (bb, 0, c)
        zmap = lambda bb, c: (bb, c, COL_Z // SSM_WIDTH)
    else:
        tmap = lambda bb, c: (bb, nc - 1 - c, 0)
        tmap_t = lambda bb, c: (bb, 0, nc - 1 - c)
        zmap = lambda bb, c: (bb, nc - 1 - c, COL_Z // SSM_WIDTH)
    st_spec = pl.BlockSpec((nb, SSM_GROUPS, SSM_STATE, SSM_GROUP_WIDTH), lambda bb, c: (bb, 0, 0, 0))
    in_specs = [
        pl.BlockSpec((nb, q, SSM_WIDTH), tmap),
        pl.BlockSpec((nb, q, SSM_BC), tmap),
        pl.BlockSpec((nb, gs, q), tmap_t),
        pl.BlockSpec((nb, q, LANES), tmap),
        pl.BlockSpec((1, LANES), lambda bb, c: (0, 0)),
        pl.BlockSpec((2 * LANES, SSM_WIDTH), lambda bb, c: (0, 0)),
        st_spec,
    ]
    args = [xs, bc, bmt, dt, a_row, e_mat, h0]
    out_specs, out_shape, scratch = [], [], []
    if finish:
        y_fwd, proj, dsum_e, nw = fin
        in_specs += [
            pl.BlockSpec((nb, q, SSM_WIDTH), tmap),
            pl.BlockSpec((nb, q, SSM_WIDTH), zmap),
            pl.BlockSpec((1, SSM_WIDTH), lambda bb, c: (0, 0)),
            pl.BlockSpec((1, SSM_WIDTH), lambda bb, c: (0, 0)),
        ]
        args += [y_fwd, proj, dsum_e, nw]
    elif with_y:
        out_specs.append(pl.BlockSpec((nb, q, SSM_WIDTH), tmap))
        out_shape.append(jax.ShapeDtypeStruct((b, t, SSM_WIDTH), F32))
    out_specs.append(st_spec)
    out_shape.append(jax.ShapeDtypeStruct((b, SSM_GROUPS, SSM_STATE, SSM_GROUP_WIDTH), F32))
    scratch.append(pltpu.VMEM((nb, SSM_GROUPS, SSM_STATE, SSM_GROUP_WIDTH), F32))
    if finish:
        out_specs.append(pl.BlockSpec((nb, q, SSM_WIDTH), tmap))
        out_shape.append(jax.ShapeDtypeStruct((b, t, SSM_WIDTH), BF16))
    return pl.pallas_call(
        functools.partial(_ssd_kernel_ordered, direction=direction, with_y=with_y, finish=finish, nb=nb),
        grid=(b // nb, nc),
        in_specs=in_specs,
        out_specs=out_specs,
        out_shape=out_shape,
        scratch_shapes=scratch + ([pltpu.VMEM((nb, q, SSM_WIDTH), F32)] if finish else []),
        compiler_params=_cparams(("parallel", "arbitrary")),
        name="ssd_scan",
    )(*args)


def _ssd_kernel_ordered(*refs, direction, with_y, finish, nb):
    n_in = 11 if finish else 7
    ins = list(refs[:n_in])
    rest = list(refs[n_in:])
    if finish:
        hfin, out, st, ybuf = rest
        ordered = ins + [ybuf, hfin, st, out]
    elif with_y:
        y, hfin, st = rest
        ordered = ins + [y, hfin, st]
    else:
        hfin, st = rest
        ordered = ins + [hfin, st]
    _ssd_kernel(*ordered, direction=direction, with_y=with_y, finish=finish, nb=nb)


def _merge_kernel(x_ref, nw_ref, sh_ref, sc_ref, gt_ref, a_ref, b_ref, c_ref, wg0_ref, wg1_ref, wg2_ref,
                  wa_ref, wb_ref, wc_ref, wo_ref, o_ref, h_ref):
    j = pl.program_id(2)

    def contribution(rows, h):
        s = None
        for act_ref, wp_ref, wg_ref in ((a_ref, wa_ref, wg0_ref), (b_ref, wb_ref, wg1_ref), (c_ref, wc_ref, wg2_ref)):
            y = jnp.dot(act_ref[0, rows], wp_ref[...], preferred_element_type=F32)
            gate = _sigmoid(jnp.dot(h, wg_ref[...], preferred_element_type=F32))
            s = gate * y if s is None else s + gate * y
        return jnp.dot(s.astype(BF16), wo_ref[...], preferred_element_type=F32)

    @pl.when(j == 0)
    def _():
        for r in range(0, h_ref.shape[0], ROW_CHUNK):
            rows = slice(r, r + ROW_CHUNK)
            hc = _rms_mod(x_ref[0, rows], nw_ref[...], sh_ref[0], sc_ref[0]).astype(BF16)
            h_ref[rows] = hc
            o_ref[0, rows] = contribution(rows, hc)

    @pl.when(j > 0)
    def _():
        o_ref[0] += contribution(slice(None), h_ref[...])

    @pl.when(j == pl.num_programs(2) - 1)
    def _():
        o_ref[0] = x_ref[0] + gt_ref[0] * o_ref[0]


def _merge(x, nw, shift, scale, gate, act_a, act_b, act_c, wg, wa, wb, wc, wo, l, tm):
    b, t, d = x.shape
    tn = 512
    nj = d // tn
    vec = pl.BlockSpec((1, 1, d), lambda bb, i, j: (bb, 0, 0))
    act = pl.BlockSpec((1, tm, ATTN_WIDTH), lambda bb, i, j: (bb, i, 0))
    wproj = pl.BlockSpec((None, ATTN_WIDTH, tn), lambda bb, i, j: (l, 0, j))
    return pl.pallas_call(
        _merge_kernel,
        grid=(b, t // tm, nj),
        in_specs=[
            pl.BlockSpec((1, tm, d), lambda bb, i, j: (bb, i, 0)),
            pl.BlockSpec((1, d), lambda bb, i, j: (0, 0)),
            vec, vec, vec, act, act, act,
            pl.BlockSpec((None, d, tn), lambda bb, i, j: (l, 0, j)),
            pl.BlockSpec((None, d, tn), lambda bb, i, j: (l, 0, nj + j)),
            pl.BlockSpec((None, d, tn), lambda bb, i, j: (l, 0, 2 * nj + j)),
            wproj, wproj, wproj,
            pl.BlockSpec((None, tn, d), lambda bb, i, j: (l, j, 0)),
        ],
        out_specs=pl.BlockSpec((1, tm, d), lambda bb, i, j: (bb, i, 0)),
        out_shape=jax.ShapeDtypeStruct((b, t, d), F32),
        scratch_shapes=[pltpu.VMEM((tm, d), BF16)],
        compiler_params=_cparams(("parallel", "parallel", "arbitrary")),
        name="branch_merge",
    )(x, nw, shift, scale, gate, act_a, act_b, act_c, wg, wg, wg, wa, wb, wc, wo)


def _ffn_kernel(x_ref, xp_ref, xn_ref, nw_ref, sh_ref, sc_ref, gt_ref, wug_ref, wuv_ref, cwg_ref, cwv_ref,
                cbg_ref, cbv_ref, wd_ref, fw_ref, o_ref, h_ref, pg_ref, pv_ref, *, tm, final):
    i = pl.program_id(1)
    j = pl.program_id(2)
    rows = tm + 2 * HALO

    def conv(p, cw_ref, cb_ref):
        prev = pltpu.roll(p, 1, 0)[HALO:HALO + tm]
        nxt = pltpu.roll(p, rows - 1, 0)[HALO:HALO + tm]
        return prev * cw_ref[0:1, :] + p[HALO:HALO + tm] * cw_ref[1:2, :] + nxt * cw_ref[2:3, :] + cb_ref[...]

    def down(pg, pv):
        s = (_silu(conv(pg, cwg_ref, cbg_ref)) * conv(pv, cwv_ref, cbv_ref)).astype(BF16)
        return jnp.dot(s, wd_ref[...], preferred_element_type=F32)

    @pl.when(j == 0)
    def _():
        nw, sh, sc = nw_ref[...], sh_ref[0], sc_ref[0]
        for r in range(0, tm, ROW_CHUNK):
            hc = _rms_mod(x_ref[0, r:r + ROW_CHUNK], nw, sh, sc).astype(BF16)
            lo, hi = HALO + r, HALO + r + ROW_CHUNK
            if r == 0:
                hp = jnp.where(i > 0, _rms_mod(xp_ref[0], nw, sh, sc), 0.0).astype(BF16)
                hc, lo = jnp.concatenate([hp, hc], axis=0), 0
            if r + ROW_CHUNK == tm:
                hn = jnp.where(i < pl.num_programs(1) - 1, _rms_mod(xn_ref[0], nw, sh, sc), 0.0).astype(BF16)
                hc, hi = jnp.concatenate([hc, hn], axis=0), rows
            h_ref[lo:hi] = hc
            pg_ref[lo:hi] = jnp.dot(hc, wug_ref[...], preferred_element_type=F32)
            pv_ref[lo:hi] = jnp.dot(hc, wuv_ref[...], preferred_element_type=F32)
        o_ref[0] = down(pg_ref[...], pv_ref[...])

    @pl.when(j > 0)
    def _():
        h = h_ref[...]
        o_ref[0] += down(jnp.dot(h, wug_ref[...], preferred_element_type=F32),
                         jnp.dot(h, wuv_ref[...], preferred_element_type=F32))

    @pl.when(j == pl.num_programs(2) - 1)
    def _():
        y = x_ref[0] + gt_ref[0] * o_ref[0]
        if final:
            y = y * lax.rsqrt(jnp.mean(y * y, axis=-1, keepdims=True) + EPS) * fw_ref[...]
        o_ref[0] = y


def _ffn(x, nw, shift, scale, gate, w_up, conv_w, conv_b, w_down, l, tm, final_w=None):
    b, t, d = x.shape
    final = final_w is not None
    if not final:
        final_w = nw
    dff = w_down.shape[1]
    tf = 512
    nj = dff // tf
    per = tm // HALO
    last = t // HALO - 1
    vec = pl.BlockSpec((1, 1, d), lambda bb, i, j: (bb, 0, 0))
    return pl.pallas_call(
        functools.partial(_ffn_kernel, tm=tm, final=final),
        grid=(b, t // tm, nj),
        in_specs=[
            pl.BlockSpec((1, tm, d), lambda bb, i, j: (bb, i, 0)),
            pl.BlockSpec((1, HALO, d), lambda bb, i, j: (bb, jnp.maximum(i * per - 1, 0), 0)),
            pl.BlockSpec((1, HALO, d), lambda bb, i, j: (bb, jnp.minimum((i + 1) * per, last), 0)),
            pl.BlockSpec((1, d), lambda bb, i, j: (0, 0)),
            vec, vec, vec,
            pl.BlockSpec((None, d, tf), lambda bb, i, j: (l, 0, j)),
            pl.BlockSpec((None, d, tf), lambda bb, i, j: (l, 0, nj + j)),
            pl.BlockSpec((FFN_CONV, tf), lambda bb, i, j: (0, j)),
            pl.BlockSpec((FFN_CONV, tf), lambda bb, i, j: (0, nj + j)),
            pl.BlockSpec((1, tf), lambda bb, i, j: (0, j)),
            pl.BlockSpec((1, tf), lambda bb, i, j: (0, nj + j)),
            pl.BlockSpec((None, tf, d), lambda bb, i, j: (l, j, 0)),
            pl.BlockSpec((1, d), lambda bb, i, j: (0, 0)),
        ],
        out_specs=pl.BlockSpec((1, tm, d), lambda bb, i, j: (bb, i, 0), pipeline_mode=pl.Buffered(1)),
        out_shape=jax.ShapeDtypeStruct((b, t, d), F32),
        scratch_shapes=[
            pltpu.VMEM((tm + 2 * HALO, d), BF16),
            pltpu.VMEM((tm + 2 * HALO, tf), F32),
            pltpu.VMEM((tm + 2 * HALO, tf), F32),
        ],
        compiler_params=_cparams(("parallel", "parallel", "arbitrary")),
        name="conv_ffn",
    )(x, x, x, nw, shift, scale, gate, w_up, w_up, conv_w, conv_w, conv_b, conv_b, w_down, final_w)


def _rope_tables(n_tokens):
    rows = n_tokens // GRID_W
    row = jnp.repeat(jnp.arange(rows), GRID_W).astype(F32)
    col = jnp.tile(jnp.arange(GRID_W), rows).astype(F32)
    inv = ROPE_THETA ** (-jnp.arange(0, ROPE_AXIS_DIM, 2, dtype=F32) / ROPE_AXIS_DIM)
    ar, ac = row[:, None] * inv, col[:, None] * inv
    cos = jnp.concatenate([jnp.cos(ar), jnp.cos(ar), jnp.cos(ac), jnp.cos(ac)], axis=1)
    sin = jnp.concatenate([-jnp.sin(ar), jnp.sin(ar), -jnp.sin(ac), jnp.sin(ac)], axis=1)
    return cos, sin


def _expansion_matrix(direction):
    head_of_col = jnp.arange(SSM_WIDTH) // SSM_HEAD_DIM
    row = jnp.arange(LANES)[:, None]
    e = (row == head_of_col[None, :] + SSM_HEADS * direction).astype(BF16)
    return jnp.concatenate([e, e], axis=0)


def _pad_lanes(v, offset):
    return jnp.zeros((1, LANES), F32).at[0, offset:offset + v.shape[0]].set(v.astype(F32))


def kernel(x, c, ctx, c_ctx, w_mod, b_mod, norm_mix_w, norm_ffn_w, w_in, q_norm_w, k_norm_w, w_attn_o, conf_conv_w, conf_conv_b, conf_ln_w, conf_ln_b, w_conf_o, ssm_conv_w, ssm_conv_b, ssm_a_log, ssm_dt_bias, ssm_d, ssm_norm_w, w_ssm_o, w_out, ffn_w_up, ffn_conv_w, ffn_conv_b, ffn_w_down, final_norm_w):
    bsz, n_lat, d = x.shape
    n_ctx = ctx.shape[1]
    depth = w_mod.shape[0]
    assert bsz <= 7 and n_lat % 512 == 0 and n_ctx % 256 == 0

    q_end = ATTN_WIDTH
    v_end = q_end + 2 * KV_WIDTH
    conf_end = v_end + 2 * CONF_WIDTH
    z_end = conf_end + SSM_WIDTH
    xbc_end = z_end + SSM_WIDTH + SSM_BC
    dt_end = xbc_end + 2 * SSM_HEADS

    cc = jnp.zeros((8, d), F32).at[:bsz].set(c).at[bsz].set(c_ctx)
    mods = _mod_vectors(cc, w_mod, b_mod).reshape(depth, 8, N_MOD, d)
    rope = _rope_tables(n_lat)
    e_mats = [_expansion_matrix(0), _expansion_matrix(1)]
    zero_state = jnp.zeros((bsz, SSM_GROUPS, SSM_STATE, SSM_GROUP_WIDTH), F32)

    w_mix = w_in.astype(BF16)
    w_gate = w_mix[:, :, dt_end:]
    wa, wb, wc, wo = (w.astype(BF16) for w in (w_attn_o, w_conf_o, w_ssm_o, w_out))
    w_up, w_down = ffn_w_up.astype(BF16), ffn_w_down.astype(BF16)

    for l in range(depth):
        need_ctx = l < depth - 1
        row = lambda v: v.reshape(1, -1)
        nmix, nffn = row(norm_mix_w[l]), row(norm_ffn_w[l])
        cwx, cwb = ssm_conv_w[l][:, :SSM_WIDTH], ssm_conv_w[l][:, SSM_WIDTH:]
        cbx, cbb = row(ssm_conv_b[l][:SSM_WIDTH]), row(ssm_conv_b[l][SSM_WIDTH:])
        a_coef = -jnp.exp(ssm_a_log[l].astype(F32))
        a_rows = [_pad_lanes(a_coef[0], 0), _pad_lanes(a_coef[1], SSM_HEADS)]
        dtb = _pad_lanes(ssm_dt_bias[l].reshape(-1), 0)
        dsum_e = row(jnp.repeat(ssm_d[l][0] + ssm_d[l][1], SSM_HEAD_DIM))
        m_lat = [mods[l, :bsz, k].reshape(bsz, 1, d) for k in range(N_MOD)]
        m_ctx = [jnp.broadcast_to(mods[l, bsz, k].reshape(1, 1, d), (bsz, 1, d)) for k in range(N_MOD)]

        proj_l = _inproj(x, nmix, m_lat[0], m_lat[1], w_mix, l, 1024 if n_lat % 1024 == 0 else 512)
        proj_c = _inproj(ctx, nmix, m_ctx[0], m_ctx[1], w_mix, l, 256)

        q_l, k_l, v_l = _qkprep(proj_l, row(q_norm_w[l]), row(k_norm_w[l]), rope, 256)
        q_c, k_c, v_c = _qkprep(proj_c, row(q_norm_w[l]), row(k_norm_w[l]), None, 256)
        act_a_l = _attention(q_l, [(k_c, v_c), (k_l, v_l)], 512, (256, 512))
        conf_args = (conf_conv_w[l], row(conf_conv_b[l]), row(conf_ln_w[l]), row(conf_ln_b[l]))
        act_b_l = _conformer(proj_l, *conf_args, 256)

        prep_l = _ssdprep(proj_l, cwx, cbx, cwb, cbb, dtb, 256)
        prep_c = _ssdprep(proj_c, cwx, cbx, cwb, cbb, dtb, 256)
        fin_args = (dsum_e, row(ssm_norm_w[l]))
        if need_ctx:
            y_cf, h_cf = _ssd_scan(*prep_c, a_rows[0], e_mats[0], zero_state, 0, True)
            h_cb, act_c_c = _ssd_scan(*prep_c, a_rows[1], e_mats[1], zero_state, 1, True, (y_cf, proj_c) + fin_args)
        else:
            (h_cf,) = _ssd_scan(*prep_c, a_rows[0], e_mats[0], zero_state, 0, False)
            (h_cb,) = _ssd_scan(*prep_c, a_rows[1], e_mats[1], zero_state, 1, False)
        y_lf, _ = _ssd_scan(*prep_l, a_rows[0], e_mats[0], h_cf, 0, True)
        _, act_c_l = _ssd_scan(*prep_l, a_rows[1], e_mats[1], h_cb, 1, True, (y_lf, proj_l) + fin_args)

        x = _merge(x, nmix, m_lat[0], m_lat[1], m_lat[2], act_a_l, act_b_l, act_c_l, w_gate, wa, wb, wc, wo, l, 512)
        ffn_args = (w_up, ffn_conv_w[l], row(ffn_conv_b[l]), w_down, l)
        x = _ffn(x, nffn, m_lat[3], m_lat[4], m_lat[5], *ffn_args, 1024 if n_lat % 1024 == 0 else 512,
                 final_w=None if need_ctx else final_norm_w.reshape(1, d))
        if need_ctx:
            act_a_c = _attention(q_c, [(k_c, v_c)], 256, (256,))
            act_b_c = _conformer(proj_c, *conf_args, 256)
            ctx = _merge(ctx, nmix, m_ctx[0], m_ctx[1], m_ctx[2], act_a_c, act_b_c, act_c_c, w_gate, wa, wb, wc, wo, l, 256)
            ctx = _ffn(ctx, nffn, m_ctx[3], m_ctx[4], m_ctx[5], *ffn_args, 256)
    return x
```

```python
import functools
import math

import jax
import jax.numpy as jnp
from jax import lax
from jax.experimental import pallas as pl
from jax.experimental.pallas import tpu as pltpu

F32 = jnp.float32
BF16 = jnp.bfloat16

EPS = 1e-6
N_MOD = 6
GRID_W = 64
HEAD_DIM = 128
ATTN_HEADS = 8
ATTN_KV_HEADS = 2
ATTN_REP = ATTN_HEADS // ATTN_KV_HEADS
ATTN_WIDTH = ATTN_HEADS * HEAD_DIM
KV_WIDTH = ATTN_KV_HEADS * HEAD_DIM
ROPE_THETA = 10000.0
ROPE_AXIS_DIM = HEAD_DIM // 2
ROPE_FREQS = ROPE_AXIS_DIM // 2
CONF_WIDTH = 1024
CONF_KERNEL = 31
SSM_HEADS = 16
SSM_HEAD_DIM = 64
SSM_WIDTH = SSM_HEADS * SSM_HEAD_DIM
SSM_GROUPS = 2
SSM_STATE = 128
SSM_GROUP_WIDTH = SSM_WIDTH // SSM_GROUPS
SSM_BC = 2 * SSM_GROUPS * SSM_STATE
SSM_CONV = 5
SSM_CHUNK = 128
FFN_CONV = 3
N_BRANCHES = 3

LANES = 128
HALO = 16
VMEM_LIMIT = 56 * 1024 * 1024
LOG2E = 1.4426950408889634
ATTN_QSCALE = HEAD_DIM ** -0.5 * LOG2E
VT_PAD = 16
VT_ROWS = HEAD_DIM + VT_PAD
NEG_BIG = -1e30

COL_Q = 0
COL_A = 1024
COL_G = 2048
COL_Z = 3072
COL_XS = 4096
COL_KV = 5120
COL_BC = 5632
COL_DT = 6144
MIX_WIDTH = 6400


def _cparams(sem):
    return pltpu.CompilerParams(dimension_semantics=sem, vmem_limit_bytes=VMEM_LIMIT)


def _sigmoid(x):
    return 1.0 / (1.0 + jnp.exp(-x))


def _silu(x):
    return x * _sigmoid(x)


def _rms_mod(x, nw, shift, scale):
    var = jnp.mean(x * x, axis=-1, keepdims=True)
    y = x * lax.rsqrt(var + EPS) * nw
    return y * (1.0 + scale) + shift


def _mod_kernel(c_ref, w_ref, b_ref, o_ref):
    s = _silu(c_ref[...]).astype(BF16)
    o_ref[0] = jnp.dot(s, w_ref[0].astype(BF16), preferred_element_type=F32) + b_ref[0]


def _mod_vectors(cc, w_mod, b_mod):
    depth, d, nm = w_mod.shape
    tn = 2048
    return pl.pallas_call(
        _mod_kernel,
        grid=(depth, nm // tn),
        in_specs=[
            pl.BlockSpec((8, d), lambda l, j: (0, 0)),
            pl.BlockSpec((1, d, tn), lambda l, j: (l, 0, j)),
            pl.BlockSpec((1, 1, tn), lambda l, j: (l, 0, j)),
        ],
        out_specs=pl.BlockSpec((1, 8, tn), lambda l, j: (l, 0, j)),
        out_shape=jax.ShapeDtypeStruct((depth, 8, nm), F32),
        compiler_params=_cparams(("parallel", "parallel")),
        name="mod_vectors",
    )(cc, w_mod, b_mod.reshape(depth, 1, nm))


ROW_CHUNK = 256


def _winprep_kernel(w_ref, mix_ref, gate_ref, *, q_end, v_end, xs_end, bc_end, dt_end):
    mix_ref[:, COL_Q:COL_A] = w_ref[:, :q_end].astype(BF16)
    mix_ref[:, COL_A:COL_KV] = w_ref[:, v_end:xs_end].astype(BF16)
    mix_ref[:, COL_KV:COL_BC] = w_ref[:, q_end:v_end].astype(BF16)
    mix_ref[:, COL_BC:COL_DT] = w_ref[:, xs_end:bc_end].astype(BF16)
    tail = w_ref[:, bc_end:bc_end + MIX_WIDTH - COL_DT]
    lane = lax.broadcasted_iota(jnp.int32, tail.shape, 1)
    mix_ref[:, COL_DT:] = jnp.where(lane < dt_end - bc_end, tail, 0.0).astype(BF16)
    gate_ref[...] = w_ref[:, dt_end:].astype(BF16)


def _winprep(w_in, offsets):
    depth, d, n = w_in.shape
    dt_end = offsets["dt_end"]
    tr = 256
    return pl.pallas_call(
        functools.partial(_winprep_kernel, **offsets),
        grid=(depth, d // tr),
        in_specs=[pl.BlockSpec((None, tr, n), lambda l, i: (l, i, 0))],
        out_specs=[
            pl.BlockSpec((None, tr, MIX_WIDTH), lambda l, i: (l, i, 0)),
            pl.BlockSpec((None, tr, n - dt_end), lambda l, i: (l, i, 0)),
        ],
        out_shape=[
            jax.ShapeDtypeStruct((depth, d, MIX_WIDTH), BF16),
            jax.ShapeDtypeStruct((depth, d, n - dt_end), BF16),
        ],
        compiler_params=_cparams(("parallel", "parallel")),
        name="w_in_prep",
    )(w_in)


def _inproj_kernel(x_ref, nw_ref, sh_ref, sc_ref, w_ref, o_ref, h_ref):
    j = pl.program_id(2)

    @pl.when(j == 0)
    def _():
        for r in range(0, h_ref.shape[0], ROW_CHUNK):
            rows = slice(r, r + ROW_CHUNK)
            hc = _rms_mod(x_ref[0, rows], nw_ref[...], sh_ref[0], sc_ref[0]).astype(BF16)
            h_ref[rows] = hc
            o_ref[0, rows] = jnp.dot(hc, w_ref[...], preferred_element_type=F32)

    @pl.when(j > 0)
    def _():
        o_ref[0] = jnp.dot(h_ref[...], w_ref[...], preferred_element_type=F32)


def _inproj(x, nw, shift, scale, w, l, tm):
    b, t, d = x.shape
    n = w.shape[2]
    tn = 1280
    return pl.pallas_call(
        _inproj_kernel,
        grid=(b, t // tm, n // tn),
        in_specs=[
            pl.BlockSpec((1, tm, d), lambda bb, i, j: (bb, i, 0)),
            pl.BlockSpec((1, d), lambda bb, i, j: (0, 0)),
            pl.BlockSpec((1, 1, d), lambda bb, i, j: (bb, 0, 0)),
            pl.BlockSpec((1, 1, d), lambda bb, i, j: (bb, 0, 0)),
            pl.BlockSpec((None, d, tn), lambda bb, i, j: (l, 0, j)),
        ],
        out_specs=pl.BlockSpec((1, tm, tn), lambda bb, i, j: (bb, i, j)),
        out_shape=jax.ShapeDtypeStruct((b, t, n), F32),
        scratch_shapes=[pltpu.VMEM((tm, d), BF16)],
        compiler_params=_cparams(("parallel", "parallel", "arbitrary")),
        name="mixer_inproj",
    )(x, nw, shift, scale, w)


def _head_norm(x, w):
    var = jnp.mean(x * x, axis=-1, keepdims=True)
    return x * lax.rsqrt(var + EPS) * w


def _rope(x, c, s):
    lane = lax.broadcasted_iota(jnp.int32, x.shape, 1)
    first = (lane % ROPE_AXIS_DIM) < ROPE_FREQS
    partner = jnp.where(first, pltpu.roll(x, HEAD_DIM - ROPE_FREQS, 1), pltpu.roll(x, ROPE_FREQS, 1))
    return x * c + partner * s


def _qkprep_kernel(*refs, use_rope):
    if use_rope:
        q_ref, kv_ref, qw_ref, kw_ref, c_ref, s_ref, qo_ref, ko_ref, vo_ref = refs
    else:
        q_ref, kv_ref, qw_ref, kw_ref, qo_ref, ko_ref, vo_ref = refs
    q = q_ref[0]
    kv = kv_ref[0]
    for h in range(ATTN_HEADS):
        y = _head_norm(q[:, h * HEAD_DIM:(h + 1) * HEAD_DIM], qw_ref[...])
        if use_rope:
            y = _rope(y, c_ref[...], s_ref[...])
        qo_ref[0, :, h * HEAD_DIM:(h + 1) * HEAD_DIM] = (y * ATTN_QSCALE).astype(BF16)
    for h in range(ATTN_KV_HEADS):
        y = _head_norm(kv[:, h * HEAD_DIM:(h + 1) * HEAD_DIM], kw_ref[...])
        if use_rope:
            y = _rope(y, c_ref[...], s_ref[...])
        ko_ref[0, :, h * HEAD_DIM:(h + 1) * HEAD_DIM] = y.astype(BF16)
    vt = kv[:, KV_WIDTH:].T.astype(BF16)
    ones = jnp.ones((VT_PAD, vt.shape[1]), BF16)
    for g in range(ATTN_KV_HEADS):
        vo_ref[0, g * VT_ROWS:g * VT_ROWS + HEAD_DIM] = vt[g * HEAD_DIM:(g + 1) * HEAD_DIM]
        vo_ref[0, g * VT_ROWS + HEAD_DIM:(g + 1) * VT_ROWS] = ones


def _qkprep(proj, qw, kw, rope, tm):
    b, t, _ = proj.shape
    use_rope = rope is not None
    in_specs = [
        pl.BlockSpec((1, tm, ATTN_WIDTH), lambda bb, i: (bb, i, COL_Q // ATTN_WIDTH)),
        pl.BlockSpec((1, tm, 2 * KV_WIDTH), lambda bb, i: (bb, i, COL_KV // (2 * KV_WIDTH))),
        pl.BlockSpec((1, HEAD_DIM), lambda bb, i: (0, 0)),
        pl.BlockSpec((1, HEAD_DIM), lambda bb, i: (0, 0)),
    ]
    args = [proj, proj, qw, kw]
    if use_rope:
        in_specs += [pl.BlockSpec((tm, HEAD_DIM), lambda bb, i: (i, 0))] * 2
        args += list(rope)
    return pl.pallas_call(
        functools.partial(_qkprep_kernel, use_rope=use_rope),
        grid=(b, t // tm),
        in_specs=in_specs,
        out_specs=[
            pl.BlockSpec((1, tm, ATTN_WIDTH), lambda bb, i: (bb, i, 0)),
            pl.BlockSpec((1, tm, KV_WIDTH), lambda bb, i: (bb, i, 0)),
            pl.BlockSpec((1, ATTN_KV_HEADS * VT_ROWS, tm), lambda bb, i: (bb, 0, i)),
        ],
        out_shape=[
            jax.ShapeDtypeStruct((b, t, ATTN_WIDTH), BF16),
            jax.ShapeDtypeStruct((b, t, KV_WIDTH), BF16),
            jax.ShapeDtypeStruct((b, ATTN_KV_HEADS * VT_ROWS, t), BF16),
        ],
        compiler_params=_cparams(("parallel", "parallel")),
        name="qk_prep",
    )(*args)


def _attn_kernel(*refs, tq, seg_tk, seg_chunks):
    n_seg = len(seg_chunks)
    q_ref = refs[0]
    kv_refs = refs[1:1 + 2 * n_seg]
    o_ref, s0_ref, s1_ref, acc_ref = refs[1 + 2 * n_seg:]
    q = q_ref[0]
    qs = jnp.concatenate([q[:, r * HEAD_DIM:(r + 1) * HEAD_DIM] for r in range(ATTN_REP)], axis=0)
    nq = ATTN_REP * tq
    chunks = [(kv_refs[2 * s], kv_refs[2 * s + 1], slice(c * seg_tk[s], (c + 1) * seg_tk[s]))
              for s in range(n_seg) for c in range(seg_chunks[s])]
    s_refs = (s0_ref, s1_ref)

    def scores(j):
        k_ref, _, rows = chunks[j]
        s_refs[j % 2][0:rows.stop - rows.start, :] = lax.dot_general(
            k_ref[0, rows, :], qs, (((1,), (1,)), ((), ())), preferred_element_type=F32)

    def update(j, m):
        _, vt_ref, cols = chunks[j]
        s = s_refs[j % 2][0:cols.stop - cols.start, :]
        m_new = jnp.maximum(m, jnp.max(s, axis=0, keepdims=True))
        alpha = jnp.exp2(m - m_new)
        p = jnp.exp2(s - m_new).astype(BF16)
        acc_ref[...] = alpha * acc_ref[...] + jnp.dot(vt_ref[0, :, cols], p, preferred_element_type=F32)
        return m_new

    scores(0)
    acc_ref[...] = jnp.zeros_like(acc_ref)
    m = jnp.full((1, nq), NEG_BIG, F32)
    for j in range(len(chunks)):
        if j + 1 < len(chunks):
            scores(j + 1)
        m = update(j, m)
    o = acc_ref[:HEAD_DIM] / acc_ref[HEAD_DIM:HEAD_DIM + 1]
    for r in range(ATTN_REP):
        o_ref[0, :, r * HEAD_DIM:(r + 1) * HEAD_DIM] = o[:, r * tq:(r + 1) * tq].T.astype(BF16)


def _attention(q, kv_segments, tq, tks):
    b, t, _ = q.shape
    gw = ATTN_REP * HEAD_DIM
    nq = ATTN_REP * tq
    in_specs = [pl.BlockSpec((1, tq, gw), lambda bb, g, i: (bb, i, g))]
    args = [q]
    for k, vt in kv_segments:
        tkv = k.shape[1]
        in_specs += [
            pl.BlockSpec((1, tkv, HEAD_DIM), lambda bb, g, i: (bb, 0, g)),
            pl.BlockSpec((1, VT_ROWS, tkv), lambda bb, g, i: (bb, g, 0)),
        ]
        args += [k, vt]
    return pl.pallas_call(
        functools.partial(_attn_kernel, tq=tq, seg_tk=tuple(tks),
                          seg_chunks=tuple(k.shape[1] // tk for (k, _), tk in zip(kv_segments, tks))),
        grid=(b, ATTN_KV_HEADS, t // tq),
        in_specs=in_specs,
        out_specs=pl.BlockSpec((1, tq, gw), lambda bb, g, i: (bb, i, g)),
        out_shape=jax.ShapeDtypeStruct((b, t, ATTN_WIDTH), BF16),
        scratch_shapes=[pltpu.VMEM((max(tks), nq), F32), pltpu.VMEM((max(tks), nq), F32),
                        pltpu.VMEM((VT_ROWS, nq), F32)],
        compiler_params=_cparams(("parallel", "parallel", "arbitrary")),
        name="attention",
    )(*args)


def _halo_specs(tm, t, width, col_block):
    per = tm // HALO
    last = t // HALO - 1
    return [
        pl.BlockSpec((1, tm, width), lambda bb, i: (bb, i, col_block)),
        pl.BlockSpec((1, HALO, width), lambda bb, i: (bb, jnp.maximum(i * per - 1, 0), col_block)),
        pl.BlockSpec((1, HALO, width), lambda bb, i: (bb, jnp.minimum((i + 1) * per, last), col_block)),
    ]


SUBLANES = 8


def _conv_rows(ext_ref, w_ref, taps, first_row, tm, width, out_fn, shift_ref=None, rows_per=64):
    if shift_ref is not None:
        n_shift = shift_ref.shape[1]
        for c0 in range(0, width, LANES):
            for r in range(1, SUBLANES):
                shift_ref[r - 1, :, c0:c0 + LANES] = ext_ref[r:r + n_shift, c0:c0 + LANES]
    for c0 in range(0, width, LANES):
        for r0 in range(0, tm, rows_per):
            acc = None
            for k in range(taps):
                lo = first_row + r0 + k
                res = lo % SUBLANES
                if shift_ref is None or res == 0:
                    rows = ext_ref[lo:lo + rows_per, c0:c0 + LANES]
                else:
                    rows = shift_ref[res - 1, lo - res:lo - res + rows_per, c0:c0 + LANES]
                term = rows * w_ref[k:k + 1, c0:c0 + LANES]
                acc = term if acc is None else acc + term
            out_fn(r0, rows_per, c0, acc)


def _conf_kernel(a_ref, ap_ref, an_ref, g_ref, gp_ref, gn_ref, cw_ref, cb_ref, lw_ref, lb_ref,
                 o_ref, u_ref, v_ref, us_ref, *, tm):
    i = pl.program_id(1)
    nt = pl.num_programs(1)
    u_ref[HALO:HALO + tm] = a_ref[0] * _sigmoid(g_ref[0])
    u_ref[0:HALO] = jnp.where(i > 0, ap_ref[0] * _sigmoid(gp_ref[0]), 0.0)
    u_ref[HALO + tm:] = jnp.where(i < nt - 1, an_ref[0] * _sigmoid(gn_ref[0]), 0.0)

    def put(r0, nr, c0, acc):
        v_ref[r0:r0 + nr, c0:c0 + LANES] = acc + cb_ref[:, c0:c0 + LANES]

    pad = (CONF_KERNEL - 1) // 2
    _conv_rows(u_ref, cw_ref, CONF_KERNEL, HALO - pad, tm, CONF_WIDTH, put, shift_ref=us_ref)
    v = v_ref[...]
    mu = jnp.mean(v, axis=-1, keepdims=True)
    d = v - mu
    var = jnp.mean(d * d, axis=-1, keepdims=True)
    y = d * lax.rsqrt(var + EPS) * lw_ref[...] + lb_ref[...]
    o_ref[0] = _silu(y).astype(BF16)


def _conformer(proj, cw, cb, lw, lb, tm):
    b, t, _ = proj.shape
    vec = pl.BlockSpec((1, CONF_WIDTH), lambda bb, i: (0, 0))
    return pl.pallas_call(
        functools.partial(_conf_kernel, tm=tm),
        grid=(b, t // tm),
        in_specs=_halo_specs(tm, t, CONF_WIDTH, COL_A // CONF_WIDTH)
        + _halo_specs(tm, t, CONF_WIDTH, COL_G // CONF_WIDTH)
        + [pl.BlockSpec((CONF_KERNEL, CONF_WIDTH), lambda bb, i: (0, 0)), vec, vec, vec],
        out_specs=pl.BlockSpec((1, tm, CONF_WIDTH), lambda bb, i: (bb, i, 0)),
        out_shape=jax.ShapeDtypeStruct((b, t, CONF_WIDTH), BF16),
        scratch_shapes=[
            pltpu.VMEM((tm + 2 * HALO, CONF_WIDTH), F32),
            pltpu.VMEM((tm, CONF_WIDTH), F32),
            pltpu.VMEM((SUBLANES - 1, tm + 2 * HALO - SUBLANES, CONF_WIDTH), F32),
        ],
        compiler_params=_cparams(("parallel", "parallel")),
        name="conformer",
    )(proj, proj, proj, proj, proj, proj, cw, cb, lw, lb)


def _ssdprep_kernel(x_ref, xp_ref, xn_ref, bc_ref, bcp_ref, bcn_ref, dt_ref, cwx_ref, cbx_ref,
                    cwb_ref, cbb_ref, dtb_ref, xo_ref, bco_ref, bmt_ref, dto_ref, xe_ref, be_ref, bv_ref, *, tm):
    i = pl.program_id(1)
    nt = pl.num_programs(1)
    xe_ref[HALO:HALO + tm] = x_ref[0]
    xe_ref[0:HALO] = jnp.where(i > 0, xp_ref[0], 0.0)
    xe_ref[HALO + tm:] = jnp.where(i < nt - 1, xn_ref[0], 0.0)
    be_ref[HALO:HALO + tm] = bc_ref[0]
    be_ref[0:HALO] = jnp.where(i > 0, bcp_ref[0], 0.0)
    be_ref[HALO + tm:] = jnp.where(i < nt - 1, bcn_ref[0], 0.0)
    pad = (SSM_CONV - 1) // 2

    def put_x(r0, nr, c0, acc):
        xo_ref[0, r0:r0 + nr, c0:c0 + LANES] = _silu(acc + cbx_ref[:, c0:c0 + LANES])

    def put_b(r0, nr, c0, acc):
        bv_ref[r0:r0 + nr, c0:c0 + LANES] = _silu(acc + cbb_ref[:, c0:c0 + LANES])

    _conv_rows(xe_ref, cwx_ref, SSM_CONV, HALO - pad, tm, SSM_WIDTH, put_x)
    _conv_rows(be_ref, cwb_ref, SSM_CONV, HALO - pad, tm, SSM_BC, put_b)
    bv = bv_ref[...]
    bco_ref[0] = bv.astype(BF16)
    bmt_ref[0] = bv[:, :SSM_GROUPS * SSM_STATE].T.astype(BF16)
    x = dt_ref[0] + dtb_ref[...]
    softplus = jnp.maximum(x, 0.0) + jnp.log(1.0 + jnp.exp(-jnp.abs(x)))
    lane = lax.broadcasted_iota(jnp.int32, x.shape, 1)
    dto_ref[0] = jnp.where(lane < 2 * SSM_HEADS, softplus, 0.0)


def _ssdprep(proj, cwx, cbx, cwb, cbb, dtb, tm):
    b, t, _ = proj.shape
    gs = SSM_GROUPS * SSM_STATE
    return pl.pallas_call(
        functools.partial(_ssdprep_kernel, tm=tm),
        grid=(b, t // tm),
        in_specs=_halo_specs(tm, t, SSM_WIDTH, COL_XS // SSM_WIDTH)
        + _halo_specs(tm, t, SSM_BC, COL_BC // SSM_BC)
        + [
            pl.BlockSpec((1, tm, LANES), lambda bb, i: (bb, i, COL_DT // LANES)),
            pl.BlockSpec((SSM_CONV, SSM_WIDTH), lambda bb, i: (0, 0)),
            pl.BlockSpec((1, SSM_WIDTH), lambda bb, i: (0, 0)),
            pl.BlockSpec((SSM_CONV, SSM_BC), lambda bb, i: (0, 0)),
            pl.BlockSpec((1, SSM_BC), lambda bb, i: (0, 0)),
            pl.BlockSpec((1, LANES), lambda bb, i: (0, 0)),
        ],
        out_specs=[
            pl.BlockSpec((1, tm, SSM_WIDTH), lambda bb, i: (bb, i, 0)),
            pl.BlockSpec((1, tm, SSM_BC), lambda bb, i: (bb, i, 0)),
            pl.BlockSpec((1, gs, tm), lambda bb, i: (bb, 0, i)),
            pl.BlockSpec((1, tm, LANES), lambda bb, i: (bb, i, 0)),
        ],
        out_shape=[
            jax.ShapeDtypeStruct((b, t, SSM_WIDTH), F32),
            jax.ShapeDtypeStruct((b, t, SSM_BC), BF16),
            jax.ShapeDtypeStruct((b, gs, t), BF16),
            jax.ShapeDtypeStruct((b, t, LANES), F32),
        ],
        scratch_shapes=[
            pltpu.VMEM((tm + 2 * HALO, SSM_WIDTH), F32),
            pltpu.VMEM((tm + 2 * HALO, SSM_BC), F32),
            pltpu.VMEM((tm, SSM_BC), F32),
        ],
        compiler_params=_cparams(("parallel", "parallel")),
        name="ssd_prep",
    )(proj, proj, proj, proj, proj, proj, proj, cwx, cbx, cwb, cbb, dtb)


def _split3(v, axis):
    hi = v.astype(BF16)
    r1 = v - hi.astype(F32)
    mid = r1.astype(BF16)
    lo = (r1 - mid.astype(F32)).astype(BF16)
    return jnp.concatenate([hi, mid, lo], axis=axis)


def _split_hi_lo(v):
    hi = v.astype(BF16)
    lo = (v - hi.astype(F32)).astype(BF16)
    return jnp.concatenate([hi, lo], axis=1)


def _ssd_kernel(*refs, direction, with_y, finish, nb):
    it = iter(refs)
    xs_ref, bc_ref, bmt_ref, dt_ref, a_ref, e_ref, h0_ref = (next(it) for _ in range(7))
    if finish:
        yf_ref, z_ref, dsum_ref, nw_ref = (next(it) for _ in range(4))
    if with_y:
        y_ref = next(it)
    hfin_ref = next(it)
    st_ref = next(it)
    if finish:
        out_ref = next(it)
    q = SSM_CHUNK
    c = pl.program_id(1)

    @pl.when(c == 0)
    def _():
        st_ref[...] = h0_ref[...]

    ii = lax.broadcasted_iota(jnp.int32, (q, q), 0)
    jj = lax.broadcasted_iota(jnp.int32, (q, q), 1)
    if direction == 0:
        valid, valid_t, end_row = jj <= ii, ii <= jj, q - 1
    else:
        valid, valid_t, end_row = jj >= ii, ii >= jj, 0
    tri = jnp.where(valid, 1.0, 0.0).astype(BF16)
    tri_t = jnp.where(valid_t, 1.0, 0.0).astype(BF16)
    tri3 = jnp.concatenate([tri, tri, tri], axis=1)
    tri3_t = jnp.concatenate([tri_t, tri_t, tri_t], axis=0)
    e = e_ref[...]

    def expand(v):
        return jnp.dot(_split_hi_lo(v), e, preferred_element_type=F32)

    gw = SSM_GROUP_WIDTH
    hpg = SSM_HEADS // SSM_GROUPS
    for bi in range(nb):
        dt = dt_ref[bi]
        da = dt * a_ref[...]
        acum_c = jnp.dot(tri3, _split3(da, 0), preferred_element_type=F32)
        acum_r = jnp.dot(_split3(da.T, 1), tri3_t, preferred_element_type=F32)
        tot = acum_c[end_row:end_row + 1, :]
        to_end = jnp.exp(tot - acum_c)
        ea = jnp.exp(acum_c)
        ea_e = expand(ea)
        xs = xs_ref[bi]
        xw = (xs * expand(dt * to_end)).astype(BF16)
        cdec_e = ea_e[end_row:end_row + 1, :]
        bc = bc_ref[bi]
        bmt = bmt_ref[bi]
        if with_y:
            xdt = (xs * expand(dt)).astype(BF16)
            lane = lax.broadcasted_iota(jnp.int32, xdt.shape, 1)
            left = (lane % LANES) < SSM_HEAD_DIM
            zero = jnp.zeros_like(xdt)
            xdt_l = jnp.where(left, xdt, zero)
            xdt_r = jnp.where(left, zero, xdt)
        for g in range(SSM_GROUPS):
            bm_t = bmt[g * SSM_STATE:(g + 1) * SSM_STATE, :]
            st = st_ref[bi, g]
            if with_y:
                cm = bc[:, (SSM_GROUPS + g) * SSM_STATE:(SSM_GROUPS + g + 1) * SSM_STATE]
                bm = bc[:, g * SSM_STATE:(g + 1) * SSM_STATE]
                cb = lax.dot_general(cm, bm, (((1,), (1,)), ((), ())), preferred_element_type=F32)
                y_inter = jnp.dot(cm, st.astype(BF16), preferred_element_type=F32) * ea_e[:, g * gw:(g + 1) * gw]
                parts = []
                for k in range(hpg // 2):
                    ms = []
                    for r in (2 * k, 2 * k + 1):
                        hl = SSM_HEADS * direction + g * hpg + r
                        seg = acum_c[:, hl:hl + 1] - acum_r[hl:hl + 1, :]
                        ms.append((cb * jnp.where(valid, jnp.exp(seg), 0.0)).astype(BF16))
                    c0 = g * gw + k * LANES
                    rhs = jnp.concatenate([xdt_l[:, c0:c0 + LANES], xdt_r[:, c0:c0 + LANES]], axis=0)
                    parts.append(jnp.dot(jnp.concatenate(ms, axis=1), rhs, preferred_element_type=F32))
                y_g = y_inter + jnp.concatenate(parts, axis=1)
                sl = slice(g * gw, (g + 1) * gw)
                if finish:
                    y_ref[bi, :, sl] = yf_ref[bi, :, sl] + y_g + dsum_ref[:, sl] * xs[:, sl]
                else:
                    y_ref[bi, :, sl] = y_g
            st_ref[bi, g] = st * cdec_e[:, g * gw:(g + 1) * gw] + jnp.dot(
                bm_t, xw[:, g * gw:(g + 1) * gw], preferred_element_type=F32)
        if finish:
            gz = y_ref[bi] * _silu(z_ref[bi])
            var = jnp.mean(gz * gz, axis=-1, keepdims=True)
            out_ref[bi] = (gz * lax.rsqrt(var + EPS) * nw_ref[...]).astype(BF16)

    @pl.when(c == pl.num_programs(1) - 1)
    def _():
        hfin_ref[...] = st_ref[...]


def _ssd_scan(xs, bc, bmt, dt, a_row, e_mat, h0, direction, with_y, fin=None):
    b, t, _ = xs.shape
    nb = 2 if b % 2 == 0 else 1
    nc = t // SSM_CHUNK
    q = SSM_CHUNK
    gs = SSM_GROUPS * SSM_STATE
    finish = fin is not None
    if direction == 0:
        tmap = lambda bb, c: (bb, c, 0)
        tmap_t = lambda bb, c: (bb, 0, c)
        zmap = lambda bb, c: (bb, c, COL_Z // SSM_WIDTH)
    else:
        tmap = lambda bb, c: (bb, nc - 1 - c, 0)
        tmap_t = lambda bb, c: (bb, 0, nc - 1 - c)
        zmap = lambda bb, c: (bb, nc - 1 - c, COL_Z // SSM_WIDTH)
    st_spec = pl.BlockSpec((nb, SSM_GROUPS, SSM_STATE, SSM_GROUP_WIDTH), lambda bb, c: (bb, 0, 0, 0))
    in_specs = [
        pl.BlockSpec((nb, q, SSM_WIDTH), tmap),
        pl.BlockSpec((nb, q, SSM_BC), tmap),
        pl.BlockSpec((nb, gs, q), tmap_t),
        pl.BlockSpec((nb, q, LANES), tmap),
        pl.BlockSpec((1, LANES), lambda bb, c: (0, 0)),
        pl.BlockSpec((2 * LANES, SSM_WIDTH), lambda bb, c: (0, 0)),
        st_spec,
    ]
    args = [xs, bc, bmt, dt, a_row, e_mat, h0]
    out_specs, out_shape, scratch = [], [], []
    if finish:
        y_fwd, proj, dsum_e, nw = fin
        in_specs += [
            pl.BlockSpec((nb, q, SSM_WIDTH), tmap),
            pl.BlockSpec((nb, q, SSM_WIDTH), zmap),
            pl.BlockSpec((1, SSM_WIDTH), lambda bb, c: (0, 0)),
            pl.BlockSpec((1, SSM_WIDTH), lambda bb, c: (0, 0)),
        ]
        args += [y_fwd, proj, dsum_e, nw]
    elif with_y:
        out_specs.append(pl.BlockSpec((nb, q, SSM_WIDTH), tmap))
        out_shape.append(jax.ShapeDtypeStruct((b, t, SSM_WIDTH), F32))
    out_specs.append(st_spec)
    out_shape.append(jax.ShapeDtypeStruct((b, SSM_GROUPS, SSM_STATE, SSM_GROUP_WIDTH), F32))
    scratch.append(pltpu.VMEM((nb, SSM_GROUPS, SSM_STATE, SSM_GROUP_WIDTH), F32))
    if finish:
        out_specs.append(pl.BlockSpec((nb, q, SSM_WIDTH), tmap))
        out_shape.append(jax.ShapeDtypeStruct((b, t, SSM_WIDTH), BF16))
    return pl.pallas_call(
        functools.partial(_ssd_kernel_ordered, direction=direction, with_y=with_y, finish=finish, nb=nb),
        grid=(b // nb, nc),
        in_specs=in_specs,
        out_specs=out_specs,
        out_shape=out_shape,
        scratch_shapes=scratch + ([pltpu.VMEM((nb, q, SSM_WIDTH), F32)] if finish else []),
        compiler_params=_cparams(("parallel", "arbitrary")),
        name="ssd_scan",
    )(*args)


def _ssd_kernel_ordered(*refs, direction, with_y, finish, nb):
    n_in = 11 if finish else 7
    ins = list(refs[:n_in])
    rest = list(refs[n_in:])
    if finish:
        hfin, out, st, ybuf = rest
        ordered = ins + [ybuf, hfin, st, out]
    elif with_y:
        y, hfin, st = rest
        ordered = ins + [y, hfin, st]
    else:
        hfin, st = rest
        ordered = ins + [hfin, st]
    _ssd_kernel(*ordered, direction=direction, with_y=with_y, finish=finish, nb=nb)


def _merge_kernel(x_ref, nw_ref, sh_ref, sc_ref, gt_ref, a_ref, b_ref, c_ref, wg0_ref, wg1_ref, wg2_ref,
                  wa_ref, wb_ref, wc_ref, wo_ref, o_ref, h_ref):
    j = pl.program_id(2)

    def contribution(rows, h):
        s = None
        for act_ref, wp_ref, wg_ref in ((a_ref, wa_ref, wg0_ref), (b_ref, wb_ref, wg1_ref), (c_ref, wc_ref, wg2_ref)):
            y = jnp.dot(act_ref[0, rows], wp_ref[...], preferred_element_type=F32)
            gate = _sigmoid(jnp.dot(h, wg_ref[...], preferred_element_type=F32))
            s = gate * y if s is None else s + gate * y
        return jnp.dot(s.astype(BF16), wo_ref[...], preferred_element_type=F32)

    @pl.when(j == 0)
    def _():
        for r in range(0, h_ref.shape[0], ROW_CHUNK):
            rows = slice(r, r + ROW_CHUNK)
            hc = _rms_mod(x_ref[0, rows], nw_ref[...], sh_ref[0], sc_ref[0]).astype(BF16)
            h_ref[rows] = hc
            o_ref[0, rows] = contribution(rows, hc)

    @pl.when(j > 0)
    def _():
        o_ref[0] += contribution(slice(None), h_ref[...])

    @pl.when(j == pl.num_programs(2) - 1)
    def _():
        o_ref[0] = x_ref[0] + gt_ref[0] * o_ref[0]


def _merge(x, nw, shift, scale, gate, act_a, act_b, act_c, wg, wa, wb, wc, wo, l, tm):
    b, t, d = x.shape
    tn = 512
    nj = d // tn
    vec = pl.BlockSpec((1, 1, d), lambda bb, i, j: (bb, 0, 0))
    act = pl.BlockSpec((1, tm, ATTN_WIDTH), lambda bb, i, j: (bb, i, 0))
    wproj = pl.BlockSpec((None, ATTN_WIDTH, tn), lambda bb, i, j: (l, 0, j))
    return pl.pallas_call(
        _merge_kernel,
        grid=(b, t // tm, nj),
        in_specs=[
            pl.BlockSpec((1, tm, d), lambda bb, i, j: (bb, i, 0)),
            pl.BlockSpec((1, d), lambda bb, i, j: (0, 0)),
            vec, vec, vec, act, act, act,
            pl.BlockSpec((None, d, tn), lambda bb, i, j: (l, 0, j)),
            pl.BlockSpec((None, d, tn), lambda bb, i, j: (l, 0, nj + j)),
            pl.BlockSpec((None, d, tn), lambda bb, i, j: (l, 0, 2 * nj + j)),
            wproj, wproj, wproj,
            pl.BlockSpec((None, tn, d), lambda bb, i, j: (l, j, 0)),
        ],
        out_specs=pl.BlockSpec((1, tm, d), lambda bb, i, j: (bb, i, 0)),
        out_shape=jax.ShapeDtypeStruct((b, t, d), F32),
        scratch_shapes=[pltpu.VMEM((tm, d), BF16)],
        compiler_params=_cparams(("parallel", "parallel", "arbitrary")),
        name="branch_merge",
    )(x, nw, shift, scale, gate, act_a, act_b, act_c, wg, wg, wg, wa, wb, wc, wo)


def _ffn_kernel(x_ref, xp_ref, xn_ref, nw_ref, sh_ref, sc_ref, gt_ref, wug_ref, wuv_ref, cwg_ref, cwv_ref,
                cbg_ref, cbv_ref, wd_ref, fw_ref, o_ref, h_ref, pg_ref, pv_ref, *, tm, final):
    i = pl.program_id(1)
    j = pl.program_id(2)
    rows = tm + 2 * HALO

    def conv(p, cw_ref, cb_ref):
        prev = pltpu.roll(p, 1, 0)[HALO:HALO + tm]
        nxt = pltpu.roll(p, rows - 1, 0)[HALO:HALO + tm]
        return prev * cw_ref[0:1, :] + p[HALO:HALO + tm] * cw_ref[1:2, :] + nxt * cw_ref[2:3, :] + cb_ref[...]

    def down(pg, pv):
        s = (_silu(conv(pg, cwg_ref, cbg_ref)) * conv(pv, cwv_ref, cbv_ref)).astype(BF16)
        return jnp.dot(s, wd_ref[...], preferred_element_type=F32)

    @pl.when(j == 0)
    def _():
        nw, sh, sc = nw_ref[...], sh_ref[0], sc_ref[0]
        for r in range(0, tm, ROW_CHUNK):
            hc = _rms_mod(x_ref[0, r:r + ROW_CHUNK], nw, sh, sc).astype(BF16)
            lo, hi = HALO + r, HALO + r + ROW_CHUNK
            if r == 0:
                hp = jnp.where(i > 0, _rms_mod(xp_ref[0], nw, sh, sc), 0.0).astype(BF16)
                hc, lo = jnp.concatenate([hp, hc], axis=0), 0
            if r + ROW_CHUNK == tm:
                hn = jnp.where(i < pl.num_programs(1) - 1, _rms_mod(xn_ref[0], nw, sh, sc), 0.0).astype(BF16)
                hc, hi = jnp.concatenate([hc, hn], axis=0), rows
            h_ref[lo:hi] = hc
            pg_ref[lo:hi] = jnp.dot(hc, wug_ref[...], preferred_element_type=F32)
            pv_ref[lo:hi] = jnp.dot(hc, wuv_ref[...], preferred_element_type=F32)
        o_ref[0] = down(pg_ref[...], pv_ref[...])

    @pl.when(j > 0)
    def _():
        h = h_ref[...]
        o_ref[0] += down(jnp.dot(h, wug_ref[...], preferred_element_type=F32),
                         jnp.dot(h, wuv_ref[...], preferred_element_type=F32))

    @pl.when(j == pl.num_programs(2) - 1)
    def _():
        y = x_ref[0] + gt_ref[0] * o_ref[0]
        if final:
            y = y * lax.rsqrt(jnp.mean(y * y, axis=-1, keepdims=True) + EPS) * fw_ref[...]
        o_ref[0] = y


def _ffn(x, nw, shift, scale, gate, w_up, conv_w, conv_b, w_down, l, tm, final_w=None):
    b, t, d = x.shape
    final = final_w is not None
    if not final:
        final_w = nw
    dff = w_down.shape[1]
    tf = 512
    nj = dff // tf
    per = tm // HALO
    last = t // HALO - 1
    vec = pl.BlockSpec((1, 1, d), lambda bb, i, j: (bb, 0, 0))
    return pl.pallas_call(
        functools.partial(_ffn_kernel, tm=tm, final=final),
        grid=(b, t // tm, nj),
        in_specs=[
            pl.BlockSpec((1, tm, d), lambda bb, i, j: (bb, i, 0)),
            pl.BlockSpec((1, HALO, d), lambda bb, i, j: (bb, jnp.maximum(i * per - 1, 0), 0)),
            pl.BlockSpec((1, HALO, d), lambda bb, i, j: (bb, jnp.minimum((i + 1) * per, last), 0)),
            pl.BlockSpec((1, d), lambda bb, i, j: (0, 0)),
            vec, vec, vec,
            pl.BlockSpec((None, d, tf), lambda bb, i, j: (l, 0, j)),
            pl.BlockSpec((None, d, tf), lambda bb, i, j: (l, 0, nj + j)),
            pl.BlockSpec((FFN_CONV, tf), lambda bb, i, j: (0, j)),
            pl.BlockSpec((FFN_CONV, tf), lambda bb, i, j: (0, nj + j)),
            pl.BlockSpec((1, tf), lambda bb, i, j: (0, j)),
            pl.BlockSpec((1, tf), lambda bb, i, j: (0, nj + j)),
            pl.BlockSpec((None, tf, d), lambda bb, i, j: (l, j, 0)),
            pl.BlockSpec((1, d), lambda bb, i, j: (0, 0)),
        ],
        out_specs=pl.BlockSpec((1, tm, d), lambda bb, i, j: (bb, i, 0), pipeline_mode=pl.Buffered(1)),
        out_shape=jax.ShapeDtypeStruct((b, t, d), F32),
        scratch_shapes=[
            pltpu.VMEM((tm + 2 * HALO, d), BF16),
            pltpu.VMEM((tm + 2 * HALO, tf), F32),
            pltpu.VMEM((tm + 2 * HALO, tf), F32),
        ],
        compiler_params=_cparams(("parallel", "parallel", "arbitrary")),
        name="conv_ffn",
    )(x, x, x, nw, shift, scale, gate, w_up, w_up, conv_w, conv_w, conv_b, conv_b, w_down, final_w)


def _rope_tables(n_tokens):
    rows = n_tokens // GRID_W
    row = jnp.repeat(jnp.arange(rows), GRID_W).astype(F32)
    col = jnp.tile(jnp.arange(GRID_W), rows).astype(F32)
    inv = ROPE_THETA ** (-jnp.arange(0, ROPE_AXIS_DIM, 2, dtype=F32) / ROPE_AXIS_DIM)
    ar, ac = row[:, None] * inv, col[:, None] * inv
    cos = jnp.concatenate([jnp.cos(ar), jnp.cos(ar), jnp.cos(ac), jnp.cos(ac)], axis=1)
    sin = jnp.concatenate([-jnp.sin(ar), jnp.sin(ar), -jnp.sin(ac), jnp.sin(ac)], axis=1)
    return cos, sin


def _expansion_matrix(direction):
    head_of_col = jnp.arange(SSM_WIDTH) // SSM_HEAD_DIM
    row = jnp.arange(LANES)[:, None]
    e = (row == head_of_col[None, :] + SSM_HEADS * direction).astype(BF16)
    return jnp.concatenate([e, e], axis=0)


def _pad_lanes(v, offset):
    return jnp.zeros((1, LANES), F32).at[0, offset:offset + v.shape[0]].set(v.astype(F32))


def kernel(x, c, ctx, c_ctx, w_mod, b_mod, norm_mix_w, norm_ffn_w, w_in, q_norm_w, k_norm_w, w_attn_o, conf_conv_w, conf_conv_b, conf_ln_w, conf_ln_b, w_conf_o, ssm_conv_w, ssm_conv_b, ssm_a_log, ssm_dt_bias, ssm_d, ssm_norm_w, w_ssm_o, w_out, ffn_w_up, ffn_conv_w, ffn_conv_b, ffn_w_down, final_norm_w):
    bsz, n_lat, d = x.shape
    n_ctx = ctx.shape[1]
    depth = w_mod.shape[0]
    assert bsz <= 7 and n_lat % 512 == 0 and n_ctx % 256 == 0

    q_end = ATTN_WIDTH
    v_end = q_end + 2 * KV_WIDTH
    conf_end = v_end + 2 * CONF_WIDTH
    z_end = conf_end + SSM_WIDTH
    xbc_end = z_end + SSM_WIDTH + SSM_BC
    dt_end = xbc_end + 2 * SSM_HEADS

    cc = jnp.zeros((8, d), F32).at[:bsz].set(c).at[bsz].set(c_ctx)
    mods = _mod_vectors(cc, w_mod, b_mod).reshape(depth, 8, N_MOD, d)
    rope = _rope_tables(n_lat)
    e_mats = [_expansion_matrix(0), _expansion_matrix(1)]
    zero_state = jnp.zeros((bsz, SSM_GROUPS, SSM_STATE, SSM_GROUP_WIDTH), F32)

    w_mix, w_gate = _winprep(w_in, dict(q_end=q_end, v_end=v_end, xs_end=z_end + SSM_WIDTH, bc_end=xbc_end,
                                        dt_end=dt_end))
    wa, wb, wc, wo = (w.astype(BF16) for w in (w_attn_o, w_conf_o, w_ssm_o, w_out))
    w_up, w_down = ffn_w_up.astype(BF16), ffn_w_down.astype(BF16)

    for l in range(depth):
        need_ctx = l < depth - 1
        row = lambda v: v.reshape(1, -1)
        nmix, nffn = row(norm_mix_w[l]), row(norm_ffn_w[l])
        cwx, cwb = ssm_conv_w[l][:, :SSM_WIDTH], ssm_conv_w[l][:, SSM_WIDTH:]
        cbx, cbb = row(ssm_conv_b[l][:SSM_WIDTH]), row(ssm_conv_b[l][SSM_WIDTH:])
        a_coef = -jnp.exp(ssm_a_log[l].astype(F32))
        a_rows = [_pad_lanes(a_coef[0], 0), _pad_lanes(a_coef[1], SSM_HEADS)]
        dtb = _pad_lanes(ssm_dt_bias[l].reshape(-1), 0)
        dsum_e = row(jnp.repeat(ssm_d[l][0] + ssm_d[l][1], SSM_HEAD_DIM))
        m_lat = [mods[l, :bsz, k].reshape(bsz, 1, d) for k in range(N_MOD)]
        m_ctx = [jnp.broadcast_to(mods[l, bsz, k].reshape(1, 1, d), (bsz, 1, d)) for k in range(N_MOD)]

        proj_l = _inproj(x, nmix, m_lat[0], m_lat[1], w_mix, l, 1024 if n_lat % 1024 == 0 else 512)
        proj_c = _inproj(ctx, nmix, m_ctx[0], m_ctx[1], w_mix, l, 256)

        q_l, k_l, v_l = _qkprep(proj_l, row(q_norm_w[l]), row(k_norm_w[l]), rope, 512)
        q_c, k_c, v_c = _qkprep(proj_c, row(q_norm_w[l]), row(k_norm_w[l]), None, 256)
        act_a_l = _attention(q_l, [(k_c, v_c), (k_l, v_l)], 512, (256, 512))
        conf_args = (conf_conv_w[l], row(conf_conv_b[l]), row(conf_ln_w[l]), row(conf_ln_b[l]))
        act_b_l = _conformer(proj_l, *conf_args, 256)

        prep_l = _ssdprep(proj_l, cwx, cbx, cwb, cbb, dtb, 512)
        prep_c = _ssdprep(proj_c, cwx, cbx, cwb, cbb, dtb, 256)
        fin_args = (dsum_e, row(ssm_norm_w[l]))
        if need_ctx:
            y_cf, h_cf = _ssd_scan(*prep_c, a_rows[0], e_mats[0], zero_state, 0, True)
            h_cb, act_c_c = _ssd_scan(*prep_c, a_rows[1], e_mats[1], zero_state, 1, True, (y_cf, proj_c) + fin_args)
        else:
            (h_cf,) = _ssd_scan(*prep_c, a_rows[0], e_mats[0], zero_state, 0, False)
            (h_cb,) = _ssd_scan(*prep_c, a_rows[1], e_mats[1], zero_state, 1, False)
        y_lf, _ = _ssd_scan(*prep_l, a_rows[0], e_mats[0], h_cf, 0, True)
        _, act_c_l = _ssd_scan(*prep_l, a_rows[1], e_mats[1], h_cb, 1, True, (y_lf, proj_l) + fin_args)

        x = _merge(x, nmix, m_lat[0], m_lat[1], m_lat[2], act_a_l, act_b_l, act_c_l, w_gate, wa, wb, wc, wo, l, 512)
        ffn_args = (w_up, ffn_conv_w[l], row(ffn_conv_b[l]), w_down, l)
        x = _ffn(x, nffn, m_lat[3], m_lat[4], m_lat[5], *ffn_args, 1024 if n_lat % 1024 == 0 else 512,
                 final_w=None if need_ctx else final_norm_w.reshape(1, d))
        if need_ctx:
            act_a_c = _attention(q_c, [(k_c, v_c)], 256, (256,))
            act_b_c = _conformer(proj_c, *conf_args, 256)
            ctx = _merge(ctx, nmix, m_ctx[0], m_ctx[1], m_ctx[2], act_a_c, act_b_c, act_c_c, w_gate, wa, wb, wc, wo, l, 256)
            ctx = _ffn(ctx, nffn, m_ctx[3], m_ctx[4], m_ctx[5], *ffn_args, 256)
    return x
```

```python
import functools
import math

import jax
import jax.numpy as jnp
from jax import lax
from jax.experimental import pallas as pl
from jax.experimental.pallas import tpu as pltpu

F32 = jnp.float32
BF16 = jnp.bfloat16

EPS = 1e-6
N_MOD = 6
GRID_W = 64
HEAD_DIM = 128
ATTN_HEADS = 8
ATTN_KV_HEADS = 2
ATTN_REP = ATTN_HEADS // ATTN_KV_HEADS
ATTN_WIDTH = ATTN_HEADS * HEAD_DIM
KV_WIDTH = ATTN_KV_HEADS * HEAD_DIM
ROPE_THETA = 10000.0
ROPE_AXIS_DIM = HEAD_DIM // 2
ROPE_FREQS = ROPE_AXIS_DIM // 2
CONF_WIDTH = 1024
CONF_KERNEL = 31
SSM_HEADS = 16
SSM_HEAD_DIM = 64
SSM_WIDTH = SSM_HEADS * SSM_HEAD_DIM
SSM_GROUPS = 2
SSM_STATE = 128
SSM_GROUP_WIDTH = SSM_WIDTH // SSM_GROUPS
SSM_BC = 2 * SSM_GROUPS * SSM_STATE
SSM_CONV = 5
SSM_CHUNK = 128
FFN_CONV = 3
N_BRANCHES = 3

LANES = 128
HALO = 16
VMEM_LIMIT = 56 * 1024 * 1024
LOG2E = 1.4426950408889634
ATTN_QSCALE = HEAD_DIM ** -0.5 * LOG2E
VT_PAD = 16
VT_ROWS = HEAD_DIM + VT_PAD
NEG_BIG = -1e30

COL_Q = 0
COL_A = 1024
COL_G = 2048
COL_Z = 3072
COL_XS = 4096
COL_KV = 5120
COL_BC = 5632
COL_DT = 6144
MIX_WIDTH = 6400


def _cparams(sem):
    return pltpu.CompilerParams(dimension_semantics=sem, vmem_limit_bytes=VMEM_LIMIT)


def _sigmoid(x):
    return 1.0 / (1.0 + jnp.exp(-x))


def _silu(x):
    return x * _sigmoid(x)


def _rms_mod(x, nw, shift, scale):
    var = jnp.mean(x * x, axis=-1, keepdims=True)
    y = x * lax.rsqrt(var + EPS) * nw
    return y * (1.0 + scale) + shift


def _mod_kernel(c_ref, w_ref, b_ref, o_ref):
    s = _silu(c_ref[...]).astype(BF16)
    o_ref[0] = jnp.dot(s, w_ref[0].astype(BF16), preferred_element_type=F32) + b_ref[0]


def _mod_vectors(cc, w_mod, b_mod):
    depth, d, nm = w_mod.shape
    tn = 2048
    return pl.pallas_call(
        _mod_kernel,
        grid=(depth, nm // tn),
        in_specs=[
            pl.BlockSpec((8, d), lambda l, j: (0, 0)),
            pl.BlockSpec((1, d, tn), lambda l, j: (l, 0, j)),
            pl.BlockSpec((1, 1, tn), lambda l, j: (l, 0, j)),
        ],
        out_specs=pl.BlockSpec((1, 8, tn), lambda l, j: (l, 0, j)),
        out_shape=jax.ShapeDtypeStruct((depth, 8, nm), F32),
        compiler_params=_cparams(("parallel", "parallel")),
        name="mod_vectors",
    )(cc, w_mod, b_mod.reshape(depth, 1, nm))


ROW_CHUNK = 256


def _winprep_kernel(wt_ref, mix_ref, gate_ref, *, q_end, v_end, xs_end, bc_end, dt_end):
    def cols(lo, hi):
        return wt_ref[lo:hi, :].T.astype(BF16)

    mix_ref[:, COL_Q:COL_A] = cols(0, q_end)
    mix_ref[:, COL_A:COL_KV] = cols(v_end, xs_end)
    mix_ref[:, COL_KV:COL_BC] = cols(q_end, v_end)
    mix_ref[:, COL_BC:COL_DT] = cols(xs_end, bc_end)
    tail = wt_ref[bc_end:bc_end + LANES, :].T
    lane = lax.broadcasted_iota(jnp.int32, tail.shape, 1)
    mix_ref[:, COL_DT:COL_DT + LANES] = jnp.where(lane < dt_end - bc_end, tail, 0.0).astype(BF16)
    mix_ref[:, COL_DT + LANES:] = jnp.zeros((tail.shape[0], MIX_WIDTH - COL_DT - LANES), BF16)
    gate_ref[...] = cols(dt_end, wt_ref.shape[0])


def _winprep(w_in, offsets):
    depth, d, n = w_in.shape
    dt_end = offsets["dt_end"]
    tr = 256
    return pl.pallas_call(
        functools.partial(_winprep_kernel, **offsets),
        grid=(depth, d // tr),
        in_specs=[pl.BlockSpec((None, n, tr), lambda l, i: (l, 0, i))],
        out_specs=[
            pl.BlockSpec((None, tr, MIX_WIDTH), lambda l, i: (l, i, 0)),
            pl.BlockSpec((None, tr, n - dt_end), lambda l, i: (l, i, 0)),
        ],
        out_shape=[
            jax.ShapeDtypeStruct((depth, d, MIX_WIDTH), BF16),
            jax.ShapeDtypeStruct((depth, d, n - dt_end), BF16),
        ],
        compiler_params=_cparams(("parallel", "parallel")),
        name="w_in_prep",
    )(jnp.swapaxes(w_in, 1, 2))


def _inproj_kernel(x_ref, nw_ref, sh_ref, sc_ref, w_ref, o_ref, h_ref):
    j = pl.program_id(2)

    @pl.when(j == 0)
    def _():
        for r in range(0, h_ref.shape[0], ROW_CHUNK):
            rows = slice(r, r + ROW_CHUNK)
            hc = _rms_mod(x_ref[0, rows], nw_ref[...], sh_ref[0], sc_ref[0]).astype(BF16)
            h_ref[rows] = hc
            o_ref[0, rows] = jnp.dot(hc, w_ref[...], preferred_element_type=F32)

    @pl.when(j > 0)
    def _():
        o_ref[0] = jnp.dot(h_ref[...], w_ref[...], preferred_element_type=F32)


def _inproj(x, nw, shift, scale, w, l, tm):
    b, t, d = x.shape
    n = w.shape[2]
    tn = 1280
    return pl.pallas_call(
        _inproj_kernel,
        grid=(b, t // tm, n // tn),
        in_specs=[
            pl.BlockSpec((1, tm, d), lambda bb, i, j: (bb, i, 0)),
            pl.BlockSpec((1, d), lambda bb, i, j: (0, 0)),
            pl.BlockSpec((1, 1, d), lambda bb, i, j: (bb, 0, 0)),
            pl.BlockSpec((1, 1, d), lambda bb, i, j: (bb, 0, 0)),
            pl.BlockSpec((None, d, tn), lambda bb, i, j: (l, 0, j)),
        ],
        out_specs=pl.BlockSpec((1, tm, tn), lambda bb, i, j: (bb, i, j)),
        out_shape=jax.ShapeDtypeStruct((b, t, n), F32),
        scratch_shapes=[pltpu.VMEM((tm, d), BF16)],
        compiler_params=_cparams(("parallel", "parallel", "arbitrary")),
        name="mixer_inproj",
    )(x, nw, shift, scale, w)


def _head_norm(x, w):
    var = jnp.mean(x * x, axis=-1, keepdims=True)
    return x * lax.rsqrt(var + EPS) * w


def _rope(x, c, s):
    lane = lax.broadcasted_iota(jnp.int32, x.shape, 1)
    first = (lane % ROPE_AXIS_DIM) < ROPE_FREQS
    partner = jnp.where(first, pltpu.roll(x, HEAD_DIM - ROPE_FREQS, 1), pltpu.roll(x, ROPE_FREQS, 1))
    return x * c + partner * s


def _qkprep_kernel(*refs, use_rope):
    if use_rope:
        q_ref, kv_ref, qw_ref, kw_ref, c_ref, s_ref, qo_ref, ko_ref, vo_ref = refs
    else:
        q_ref, kv_ref, qw_ref, kw_ref, qo_ref, ko_ref, vo_ref = refs
    q = q_ref[0]
    kv = kv_ref[0]
    for h in range(ATTN_HEADS):
        y = _head_norm(q[:, h * HEAD_DIM:(h + 1) * HEAD_DIM], qw_ref[...])
        if use_rope:
            y = _rope(y, c_ref[...], s_ref[...])
        qo_ref[0, :, h * HEAD_DIM:(h + 1) * HEAD_DIM] = (y * ATTN_QSCALE).astype(BF16)
    for h in range(ATTN_KV_HEADS):
        y = _head_norm(kv[:, h * HEAD_DIM:(h + 1) * HEAD_DIM], kw_ref[...])
        if use_rope:
            y = _rope(y, c_ref[...], s_ref[...])
        ko_ref[0, :, h * HEAD_DIM:(h + 1) * HEAD_DIM] = y.astype(BF16)
    vt = kv[:, KV_WIDTH:].T.astype(BF16)
    ones = jnp.ones((VT_PAD, vt.shape[1]), BF16)
    for g in range(ATTN_KV_HEADS):
        vo_ref[0, g * VT_ROWS:g * VT_ROWS + HEAD_DIM] = vt[g * HEAD_DIM:(g + 1) * HEAD_DIM]
        vo_ref[0, g * VT_ROWS + HEAD_DIM:(g + 1) * VT_ROWS] = ones


def _qkprep(proj, qw, kw, rope, tm):
    b, t, _ = proj.shape
    use_rope = rope is not None
    in_specs = [
        pl.BlockSpec((1, tm, ATTN_WIDTH), lambda bb, i: (bb, i, COL_Q // ATTN_WIDTH)),
        pl.BlockSpec((1, tm, 2 * KV_WIDTH), lambda bb, i: (bb, i, COL_KV // (2 * KV_WIDTH))),
        pl.BlockSpec((1, HEAD_DIM), lambda bb, i: (0, 0)),
        pl.BlockSpec((1, HEAD_DIM), lambda bb, i: (0, 0)),
    ]
    args = [proj, proj, qw, kw]
    if use_rope:
        in_specs += [pl.BlockSpec((tm, HEAD_DIM), lambda bb, i: (i, 0))] * 2
        args += list(rope)
    return pl.pallas_call(
        functools.partial(_qkprep_kernel, use_rope=use_rope),
        grid=(b, t // tm),
        in_specs=in_specs,
        out_specs=[
            pl.BlockSpec((1, tm, ATTN_WIDTH), lambda bb, i: (bb, i, 0)),
            pl.BlockSpec((1, tm, KV_WIDTH), lambda bb, i: (bb, i, 0)),
            pl.BlockSpec((1, ATTN_KV_HEADS * VT_ROWS, tm), lambda bb, i: (bb, 0, i)),
        ],
        out_shape=[
            jax.ShapeDtypeStruct((b, t, ATTN_WIDTH), BF16),
            jax.ShapeDtypeStruct((b, t, KV_WIDTH), BF16),
            jax.ShapeDtypeStruct((b, ATTN_KV_HEADS * VT_ROWS, t), BF16),
        ],
        compiler_params=_cparams(("parallel", "parallel")),
        name="qk_prep",
    )(*args)


def _attn_kernel(*refs, tq, seg_tk, seg_chunks):
    n_seg = len(seg_chunks)
    q_ref = refs[0]
    kv_refs = refs[1:1 + 2 * n_seg]
    o_ref, s0_ref, s1_ref, acc_ref = refs[1 + 2 * n_seg:]
    q = q_ref[0]
    qs = jnp.concatenate([q[:, r * HEAD_DIM:(r + 1) * HEAD_DIM] for r in range(ATTN_REP)], axis=0)
    nq = ATTN_REP * tq
    chunks = [(kv_refs[2 * s], kv_refs[2 * s + 1], slice(c * seg_tk[s], (c + 1) * seg_tk[s]))
              for s in range(n_seg) for c in range(seg_chunks[s])]
    s_refs = (s0_ref, s1_ref)

    def scores(j):
        k_ref, _, rows = chunks[j]
        s_refs[j % 2][0:rows.stop - rows.start, :] = lax.dot_general(
            k_ref[0, rows, :], qs, (((1,), (1,)), ((), ())), preferred_element_type=F32)

    def update(j, m):
        _, vt_ref, cols = chunks[j]
        s = s_refs[j % 2][0:cols.stop - cols.start, :]
        m_new = jnp.maximum(m, jnp.max(s, axis=0, keepdims=True))
        alpha = jnp.exp2(m - m_new)
        p = jnp.exp2(s - m_new).astype(BF16)
        acc_ref[...] = alpha * acc_ref[...] + jnp.dot(vt_ref[0, :, cols], p, preferred_element_type=F32)
        return m_new

    scores(0)
    acc_ref[...] = jnp.zeros_like(acc_ref)
    m = jnp.full((1, nq), NEG_BIG, F32)
    for j in range(len(chunks)):
        if j + 1 < len(chunks):
            scores(j + 1)
        m = update(j, m)
    o = acc_ref[:HEAD_DIM] / acc_ref[HEAD_DIM:HEAD_DIM + 1]
    for r in range(ATTN_REP):
        o_ref[0, :, r * HEAD_DIM:(r + 1) * HEAD_DIM] = o[:, r * tq:(r + 1) * tq].T.astype(BF16)


def _attention(q, kv_segments, tq, tks):
    b, t, _ = q.shape
    gw = ATTN_REP * HEAD_DIM
    nq = ATTN_REP * tq
    in_specs = [pl.BlockSpec((1, tq, gw), lambda bb, g, i: (bb, i, g))]
    args = [q]
    for k, vt in kv_segments:
        tkv = k.shape[1]
        in_specs += [
            pl.BlockSpec((1, tkv, HEAD_DIM), lambda bb, g, i: (bb, 0, g)),
            pl.BlockSpec((1, VT_ROWS, tkv), lambda bb, g, i: (bb, g, 0)),
        ]
        args += [k, vt]
    return pl.pallas_call(
        functools.partial(_attn_kernel, tq=tq, seg_tk=tuple(tks),
                          seg_chunks=tuple(k.shape[1] // tk for (k, _), tk in zip(kv_segments, tks))),
        grid=(b, ATTN_KV_HEADS, t // tq),
        in_specs=in_specs,
        out_specs=pl.BlockSpec((1, tq, gw), lambda bb, g, i: (bb, i, g)),
        out_shape=jax.ShapeDtypeStruct((b, t, ATTN_WIDTH), BF16),
        scratch_shapes=[pltpu.VMEM((max(tks), nq), F32), pltpu.VMEM((max(tks), nq), F32),
                        pltpu.VMEM((VT_ROWS, nq), F32)],
        compiler_params=_cparams(("parallel", "parallel", "arbitrary")),
        name="attention",
    )(*args)


def _halo_specs(tm, t, width, col_block):
    per = tm // HALO
    last = t // HALO - 1
    return [
        pl.BlockSpec((1, tm, width), lambda bb, i: (bb, i, col_block)),
        pl.BlockSpec((1, HALO, width), lambda bb, i: (bb, jnp.maximum(i * per - 1, 0), col_block)),
        pl.BlockSpec((1, HALO, width), lambda bb, i: (bb, jnp.minimum((i + 1) * per, last), col_block)),
    ]


SUBLANES = 8


def _conv_rows(ext_ref, w_ref, taps, first_row, tm, width, out_fn, shift_ref=None, rows_per=64):
    if shift_ref is not None:
        n_shift = shift_ref.shape[1]
        for c0 in range(0, width, LANES):
            for r in range(1, SUBLANES):
                shift_ref[r - 1, :, c0:c0 + LANES] = ext_ref[r:r + n_shift, c0:c0 + LANES]
    for c0 in range(0, width, LANES):
        for r0 in range(0, tm, rows_per):
            acc = None
            for k in range(taps):
                lo = first_row + r0 + k
                res = lo % SUBLANES
                if shift_ref is None or res == 0:
                    rows = ext_ref[lo:lo + rows_per, c0:c0 + LANES]
                else:
                    rows = shift_ref[res - 1, lo - res:lo - res + rows_per, c0:c0 + LANES]
                term = rows * w_ref[k:k + 1, c0:c0 + LANES]
                acc = term if acc is None else acc + term
            out_fn(r0, rows_per, c0, acc)


def _conf_kernel(a_ref, ap_ref, an_ref, g_ref, gp_ref, gn_ref, cw_ref, cb_ref, lw_ref, lb_ref,
                 o_ref, u_ref, v_ref, us_ref, *, tm):
    i = pl.program_id(1)
    nt = pl.num_programs(1)
    u_ref[HALO:HALO + tm] = a_ref[0] * _sigmoid(g_ref[0])
    u_ref[0:HALO] = jnp.where(i > 0, ap_ref[0] * _sigmoid(gp_ref[0]), 0.0)
    u_ref[HALO + tm:] = jnp.where(i < nt - 1, an_ref[0] * _sigmoid(gn_ref[0]), 0.0)

    def put(r0, nr, c0, acc):
        v_ref[r0:r0 + nr, c0:c0 + LANES] = acc + cb_ref[:, c0:c0 + LANES]

    pad = (CONF_KERNEL - 1) // 2
    _conv_rows(u_ref, cw_ref, CONF_KERNEL, HALO - pad, tm, CONF_WIDTH, put, shift_ref=us_ref)
    v = v_ref[...]
    mu = jnp.mean(v, axis=-1, keepdims=True)
    d = v - mu
    var = jnp.mean(d * d, axis=-1, keepdims=True)
    y = d * lax.rsqrt(var + EPS) * lw_ref[...] + lb_ref[...]
    o_ref[0] = _silu(y).astype(BF16)


def _conformer(proj, cw, cb, lw, lb, tm):
    b, t, _ = proj.shape
    vec = pl.BlockSpec((1, CONF_WIDTH), lambda bb, i: (0, 0))
    return pl.pallas_call(
        functools.partial(_conf_kernel, tm=tm),
        grid=(b, t // tm),
        in_specs=_halo_specs(tm, t, CONF_WIDTH, COL_A // CONF_WIDTH)
        + _halo_specs(tm, t, CONF_WIDTH, COL_G // CONF_WIDTH)
        + [pl.BlockSpec((CONF_KERNEL, CONF_WIDTH), lambda bb, i: (0, 0)), vec, vec, vec],
        out_specs=pl.BlockSpec((1, tm, CONF_WIDTH), lambda bb, i: (bb, i, 0)),
        out_shape=jax.ShapeDtypeStruct((b, t, CONF_WIDTH), BF16),
        scratch_shapes=[
            pltpu.VMEM((tm + 2 * HALO, CONF_WIDTH), F32),
            pltpu.VMEM((tm, CONF_WIDTH), F32),
            pltpu.VMEM((SUBLANES - 1, tm + 2 * HALO - SUBLANES, CONF_WIDTH), F32),
        ],
        compiler_params=_cparams(("parallel", "parallel")),
        name="conformer",
    )(proj, proj, proj, proj, proj, proj, cw, cb, lw, lb)


def _ssdprep_kernel(x_ref, xp_ref, xn_ref, bc_ref, bcp_ref, bcn_ref, dt_ref, cwx_ref, cbx_ref,
                    cwb_ref, cbb_ref, dtb_ref, xo_ref, bco_ref, bmt_ref, dto_ref, xe_ref, be_ref, bv_ref, *, tm):
    i = pl.program_id(1)
    nt = pl.num_programs(1)
    xe_ref[HALO:HALO + tm] = x_ref[0]
    xe_ref[0:HALO] = jnp.where(i > 0, xp_ref[0], 0.0)
    xe_ref[HALO + tm:] = jnp.where(i < nt - 1, xn_ref[0], 0.0)
    be_ref[HALO:HALO + tm] = bc_ref[0]
    be_ref[0:HALO] = jnp.where(i > 0, bcp_ref[0], 0.0)
    be_ref[HALO + tm:] = jnp.where(i < nt - 1, bcn_ref[0], 0.0)
    pad = (SSM_CONV - 1) // 2

    def put_x(r0, nr, c0, acc):
        xo_ref[0, r0:r0 + nr, c0:c0 + LANES] = _silu(acc + cbx_ref[:, c0:c0 + LANES])

    def put_b(r0, nr, c0, acc):
        bv_ref[r0:r0 + nr, c0:c0 + LANES] = _silu(acc + cbb_ref[:, c0:c0 + LANES])

    _conv_rows(xe_ref, cwx_ref, SSM_CONV, HALO - pad, tm, SSM_WIDTH, put_x)
    _conv_rows(be_ref, cwb_ref, SSM_CONV, HALO - pad, tm, SSM_BC, put_b)
    bv = bv_ref[...]
    bco_ref[0] = bv.astype(BF16)
    bmt_ref[0] = bv[:, :SSM_GROUPS * SSM_STATE].T.astype(BF16)
    x = dt_ref[0] + dtb_ref[...]
    softplus = jnp.maximum(x, 0.0) + jnp.log(1.0 + jnp.exp(-jnp.abs(x)))
    lane = lax.broadcasted_iota(jnp.int32, x.shape, 1)
    dto_ref[0] = jnp.where(lane < 2 * SSM_HEADS, softplus, 0.0)


def _ssdprep(proj, cwx, cbx, cwb, cbb, dtb, tm):
    b, t, _ = proj.shape
    gs = SSM_GROUPS * SSM_STATE
    return pl.pallas_call(
        functools.partial(_ssdprep_kernel, tm=tm),
        grid=(b, t // tm),
        in_specs=_halo_specs(tm, t, SSM_WIDTH, COL_XS // SSM_WIDTH)
        + _halo_specs(tm, t, SSM_BC, COL_BC // SSM_BC)
        + [
            pl.BlockSpec((1, tm, LANES), lambda bb, i: (bb, i, COL_DT // LANES)),
            pl.BlockSpec((SSM_CONV, SSM_WIDTH), lambda bb, i: (0, 0)),
            pl.BlockSpec((1, SSM_WIDTH), lambda bb, i: (0, 0)),
            pl.BlockSpec((SSM_CONV, SSM_BC), lambda bb, i: (0, 0)),
            pl.BlockSpec((1, SSM_BC), lambda bb, i: (0, 0)),
            pl.BlockSpec((1, LANES), lambda bb, i: (0, 0)),
        ],
        out_specs=[
            pl.BlockSpec((1, tm, SSM_WIDTH), lambda bb, i: (bb, i, 0)),
            pl.BlockSpec((1, tm, SSM_BC), lambda bb, i: (bb, i, 0)),
            pl.BlockSpec((1, gs, tm), lambda bb, i: (bb, 0, i)),
            pl.BlockSpec((1, tm, LANES), lambda bb, i: (bb, i, 0)),
        ],
        out_shape=[
            jax.ShapeDtypeStruct((b, t, SSM_WIDTH), F32),
            jax.ShapeDtypeStruct((b, t, SSM_BC), BF16),
            jax.ShapeDtypeStruct((b, gs, t), BF16),
            jax.ShapeDtypeStruct((b, t, LANES), F32),
        ],
        scratch_shapes=[
            pltpu.VMEM((tm + 2 * HALO, SSM_WIDTH), F32),
            pltpu.VMEM((tm + 2 * HALO, SSM_BC), F32),
            pltpu.VMEM((tm, SSM_BC), F32),
        ],
        compiler_params=_cparams(("parallel", "parallel")),
        name="ssd_prep",
    )(proj, proj, proj, proj, proj, proj, proj, cwx, cbx, cwb, cbb, dtb)


def _split3(v, axis):
    hi = v.astype(BF16)
    r1 = v - hi.astype(F32)
    mid = r1.astype(BF16)
    lo = (r1 - mid.astype(F32)).astype(BF16)
    return jnp.concatenate([hi, mid, lo], axis=axis)


def _split_hi_lo(v):
    hi = v.astype(BF16)
    lo = (v - hi.astype(F32)).astype(BF16)
    return jnp.concatenate([hi, lo], axis=1)


def _ssd_kernel(*refs, direction, with_y, finish, nb):
    it = iter(refs)
    xs_ref, bc_ref, bmt_ref, dt_ref, a_ref, e_ref, h0_ref = (next(it) for _ in range(7))
    if finish:
        yf_ref, z_ref, dsum_ref, nw_ref = (next(it) for _ in range(4))
    if with_y:
        y_ref = next(it)
    hfin_ref = next(it)
    st_ref = next(it)
    if finish:
        out_ref = next(it)
    q = SSM_CHUNK
    c = pl.program_id(1)

    @pl.when(c == 0)
    def _():
        st_ref[...] = h0_ref[...]

    ii = lax.broadcasted_iota(jnp.int32, (q, q), 0)
    jj = lax.broadcasted_iota(jnp.int32, (q, q), 1)
    if direction == 0:
        valid, valid_t, end_row = jj <= ii, ii <= jj, q - 1
    else:
        valid, valid_t, end_row = jj >= ii, ii >= jj, 0
    tri = jnp.where(valid, 1.0, 0.0).astype(BF16)
    tri_t = jnp.where(valid_t, 1.0, 0.0).astype(BF16)
    tri3 = jnp.concatenate([tri, tri, tri], axis=1)
    tri3_t = jnp.concatenate([tri_t, tri_t, tri_t], axis=0)
    e = e_ref[...]

    def expand(v):
        return jnp.dot(_split_hi_lo(v), e, preferred_element_type=F32)

    gw = SSM_GROUP_WIDTH
    hpg = SSM_HEADS // SSM_GROUPS
    for bi in range(nb):
        dt = dt_ref[bi]
        da = dt * a_ref[...]
        acum_c = jnp.dot(tri3, _split3(da, 0), preferred_element_type=F32)
        acum_r = jnp.dot(_split3(da.T, 1), tri3_t, preferred_element_type=F32)
        tot = acum_c[end_row:end_row + 1, :]
        to_end = jnp.exp(tot - acum_c)
        ea = jnp.exp(acum_c)
        ea_e = expand(ea)
        xs = xs_ref[bi]
        xw = (xs * expand(dt * to_end)).astype(BF16)
        cdec_e = ea_e[end_row:end_row + 1, :]
        bc = bc_ref[bi]
        bmt = bmt_ref[bi]
        if with_y:
            xdt = (xs * expand(dt)).astype(BF16)
            lane = lax.broadcasted_iota(jnp.int32, xdt.shape, 1)
            left = (lane % LANES) < SSM_HEAD_DIM
            zero = jnp.zeros_like(xdt)
            xdt_l = jnp.where(left, xdt, zero)
            xdt_r = jnp.where(left, zero, xdt)
        for g in range(SSM_GROUPS):
            bm_t = bmt[g * SSM_STATE:(g + 1) * SSM_STATE, :]
            st = st_ref[bi, g]
            if with_y:
                cm = bc[:, (SSM_GROUPS + g) * SSM_STATE:(SSM_GROUPS + g + 1) * SSM_STATE]
                bm = bc[:, g * SSM_STATE:(g + 1) * SSM_STATE]
                cb = lax.dot_general(cm, bm, (((1,), (1,)), ((), ())), preferred_element_type=F32)
                y_inter = jnp.dot(cm, st.astype(BF16), preferred_element_type=F32) * ea_e[:, g * gw:(g + 1) * gw]
                parts = []
                for k in range(hpg // 2):
                    ms = []
                    for r in (2 * k, 2 * k + 1):
                        hl = SSM_HEADS * direction + g * hpg + r
                        seg = acum_c[:, hl:hl + 1] - acum_r[hl:hl + 1, :]
                        ms.append((cb * jnp.where(valid, jnp.exp(seg), 0.0)).astype(BF16))
                    c0 = g * gw + k * LANES
                    rhs = jnp.concatenate([xdt_l[:, c0:c0 + LANES], xdt_r[:, c0:c0 + LANES]], axis=0)
                    parts.append(jnp.dot(jnp.concatenate(ms, axis=1), rhs, preferred_element_type=F32))
                y_g = y_inter + jnp.concatenate(parts, axis=1)
                sl = slice(g * gw, (g + 1) * gw)
                if finish:
                    y_ref[bi, :, sl] = yf_ref[bi, :, sl] + y_g + dsum_ref[:, sl] * xs[:, sl]
                else:
                    y_ref[bi, :, sl] = y_g
            st_ref[bi, g] = st * cdec_e[:, g * gw:(g + 1) * gw] + jnp.dot(
                bm_t, xw[:, g * gw:(g + 1) * gw], preferred_element_type=F32)
        if finish:
            gz = y_ref[bi] * _silu(z_ref[bi])
            var = jnp.mean(gz * gz, axis=-1, keepdims=True)
            out_ref[bi] = (gz * lax.rsqrt(var + EPS) * nw_ref[...]).astype(BF16)

    @pl.when(c == pl.num_programs(1) - 1)
    def _():
        hfin_ref[...] = st_ref[...]


def _ssd_scan(xs, bc, bmt, dt, a_row, e_mat, h0, direction, with_y, fin=None):
    b, t, _ = xs.shape
    nb = 4 if b % 4 == 0 else (2 if b % 2 == 0 else 1)
    nc = t // SSM_CHUNK
    q = SSM_CHUNK
    gs = SSM_GROUPS * SSM_STATE
    finish = fin is not None
    if direction == 0:
        tmap = lambda bb, c: (bb, c, 0)
        tmap_t = lambda bb, c: (bb, 0, c)
        zmap = lambda bb, c: (bb, c, COL_Z // SSM_WIDTH)
    else:
        tmap = lambda bb, c: (bb, nc - 1 - c, 0)
        tmap_t = lambda bb, c: (bb, 0, nc - 1 - c)
        zmap = lambda bb, c: (bb, nc - 1 - c, COL_Z // SSM_WIDTH)
    st_spec = pl.BlockSpec((nb, SSM_GROUPS, SSM_STATE, SSM_GROUP_WIDTH), lambda bb, c: (bb, 0, 0, 0))
    in_specs = [
        pl.BlockSpec((nb, q, SSM_WIDTH), tmap),
        pl.BlockSpec((nb, q, SSM_BC), tmap),
        pl.BlockSpec((nb, gs, q), tmap_t),
        pl.BlockSpec((nb, q, LANES), tmap),
        pl.BlockSpec((1, LANES), lambda bb, c: (0, 0)),
        pl.BlockSpec((2 * LANES, SSM_WIDTH), lambda bb, c: (0, 0)),
        st_spec,
    ]
    args = [xs, bc, bmt, dt, a_row, e_mat, h0]
    out_specs, out_shape, scratch = [], [], []
    if finish:
        y_fwd, proj, dsum_e, nw = fin
        in_specs += [
            pl.BlockSpec((nb, q, SSM_WIDTH), tmap),
            pl.BlockSpec((nb, q, SSM_WIDTH), zmap),
            pl.BlockSpec((1, SSM_WIDTH), lambda bb, c: (0, 0)),
            pl.BlockSpec((1, SSM_WIDTH), lambda bb, c: (0, 0)),
        ]
        args += [y_fwd, proj, dsum_e, nw]
    elif with_y:
        out_specs.append(pl.BlockSpec((nb, q, SSM_WIDTH), tmap))
        out_shape.append(jax.ShapeDtypeStruct((b, t, SSM_WIDTH), F32))
    out_specs.append(st_spec)
    out_shape.append(jax.ShapeDtypeStruct((b, SSM_GROUPS, SSM_STATE, SSM_GROUP_WIDTH), F32))
    scratch.append(pltpu.VMEM((nb, SSM_GROUPS, SSM_STATE, SSM_GROUP_WIDTH), F32))
    if finish:
        out_specs.append(pl.BlockSpec((nb, q, SSM_WIDTH), tmap))
        out_shape.append(jax.ShapeDtypeStruct((b, t, SSM_WIDTH), BF16))
    return pl.pallas_call(
        functools.partial(_ssd_kernel_ordered, direction=direction, with_y=with_y, finish=finish, nb=nb),
        grid=(b // nb, nc),
        in_specs=in_specs,
        out_specs=out_specs,
        out_shape=out_shape,
        scratch_shapes=scratch + ([pltpu.VMEM((nb, q, SSM_WIDTH), F32)] if finish else []),
        compiler_params=_cparams(("parallel", "arbitrary")),
        name="ssd_scan",
    )(*args)


def _ssd_kernel_ordered(*refs, direction, with_y, finish, nb):
    n_in = 11 if finish else 7
    ins = list(refs[:n_in])
    rest = list(refs[n_in:])
    if finish:
        hfin, out, st, ybuf = rest
        ordered = ins + [ybuf, hfin, st, out]
    elif with_y:
        y, hfin, st = rest
        ordered = ins + [y, hfin, st]
    else:
        hfin, st = rest
        ordered = ins + [hfin, st]
    _ssd_kernel(*ordered, direction=direction, with_y=with_y, finish=finish, nb=nb)


def _merge_kernel(x_ref, nw_ref, sh_ref, sc_ref, gt_ref, a_ref, b_ref, c_ref, wg0_ref, wg1_ref, wg2_ref,
                  wa_ref, wb_ref, wc_ref, wo_ref, o_ref, h_ref):
    j = pl.program_id(2)

    def contribution(rows, h):
        s = None
        for act_ref, wp_ref, wg_ref in ((a_ref, wa_ref, wg0_ref), (b_ref, wb_ref, wg1_ref), (c_ref, wc_ref, wg2_ref)):
            y = jnp.dot(act_ref[0, rows], wp_ref[...], preferred_element_type=F32)
            gate = _sigmoid(jnp.dot(h, wg_ref[...], preferred_element_type=F32))
            s = gate * y if s is None else s + gate * y
        return jnp.dot(s.astype(BF16), wo_ref[...], preferred_element_type=F32)

    @pl.when(j == 0)
    def _():
        for r in range(0, h_ref.shape[0], ROW_CHUNK):
            rows = slice(r, r + ROW_CHUNK)
            hc = _rms_mod(x_ref[0, rows], nw_ref[...], sh_ref[0], sc_ref[0]).astype(BF16)
            h_ref[rows] = hc
            o_ref[0, rows] = contribution(rows, hc)

    @pl.when(j > 0)
    def _():
        o_ref[0] += contribution(slice(None), h_ref[...])

    @pl.when(j == pl.num_programs(2) - 1)
    def _():
        o_ref[0] = x_ref[0] + gt_ref[0] * o_ref[0]


def _merge(x, nw, shift, scale, gate, act_a, act_b, act_c, wg, wa, wb, wc, wo, l, tm):
    b, t, d = x.shape
    tn = 512
    nj = d // tn
    vec = pl.BlockSpec((1, 1, d), lambda bb, i, j: (bb, 0, 0))
    act = pl.BlockSpec((1, tm, ATTN_WIDTH), lambda bb, i, j: (bb, i, 0))
    wproj = pl.BlockSpec((None, ATTN_WIDTH, tn), lambda bb, i, j: (l, 0, j))
    return pl.pallas_call(
        _merge_kernel,
        grid=(b, t // tm, nj),
        in_specs=[
            pl.BlockSpec((1, tm, d), lambda bb, i, j: (bb, i, 0)),
            pl.BlockSpec((1, d), lambda bb, i, j: (0, 0)),
            vec, vec, vec, act, act, act,
            pl.BlockSpec((None, d, tn), lambda bb, i, j: (l, 0, j)),
            pl.BlockSpec((None, d, tn), lambda bb, i, j: (l, 0, nj + j)),
            pl.BlockSpec((None, d, tn), lambda bb, i, j: (l, 0, 2 * nj + j)),
            wproj, wproj, wproj,
            pl.BlockSpec((None, tn, d), lambda bb, i, j: (l, j, 0)),
        ],
        out_specs=pl.BlockSpec((1, tm, d), lambda bb, i, j: (bb, i, 0)),
        out_shape=jax.ShapeDtypeStruct((b, t, d), F32),
        scratch_shapes=[pltpu.VMEM((tm, d), BF16)],
        compiler_params=_cparams(("parallel", "parallel", "arbitrary")),
        name="branch_merge",
    )(x, nw, shift, scale, gate, act_a, act_b, act_c, wg, wg, wg, wa, wb, wc, wo)


def _ffn_kernel(x_ref, xp_ref, xn_ref, nw_ref, sh_ref, sc_ref, gt_ref, wug_ref, wuv_ref, cwg_ref, cwv_ref,
                cbg_ref, cbv_ref, wd_ref, fw_ref, o_ref, h_ref, pg_ref, pv_ref, *, tm, final):
    i = pl.program_id(1)
    j = pl.program_id(2)
    rows = tm + 2 * HALO

    def conv(p, cw_ref, cb_ref):
        prev = pltpu.roll(p, 1, 0)[HALO:HALO + tm]
        nxt = pltpu.roll(p, rows - 1, 0)[HALO:HALO + tm]
        return prev * cw_ref[0:1, :] + p[HALO:HALO + tm] * cw_ref[1:2, :] + nxt * cw_ref[2:3, :] + cb_ref[...]

    def down(pg, pv):
        s = (_silu(conv(pg, cwg_ref, cbg_ref)) * conv(pv, cwv_ref, cbv_ref)).astype(BF16)
        return jnp.dot(s, wd_ref[...], preferred_element_type=F32)

    @pl.when(j == 0)
    def _():
        nw, sh, sc = nw_ref[...], sh_ref[0], sc_ref[0]
        for r in range(0, tm, ROW_CHUNK):
            hc = _rms_mod(x_ref[0, r:r + ROW_CHUNK], nw, sh, sc).astype(BF16)
            lo, hi = HALO + r, HALO + r + ROW_CHUNK
            if r == 0:
                hp = jnp.where(i > 0, _rms_mod(xp_ref[0], nw, sh, sc), 0.0).astype(BF16)
                hc, lo = jnp.concatenate([hp, hc], axis=0), 0
            if r + ROW_CHUNK == tm:
                hn = jnp.where(i < pl.num_programs(1) - 1, _rms_mod(xn_ref[0], nw, sh, sc), 0.0).astype(BF16)
                hc, hi = jnp.concatenate([hc, hn], axis=0), rows
            h_ref[lo:hi] = hc
            pg_ref[lo:hi] = jnp.dot(hc, wug_ref[...], preferred_element_type=F32)
            pv_ref[lo:hi] = jnp.dot(hc, wuv_ref[...], preferred_element_type=F32)
        o_ref[0] = down(pg_ref[...], pv_ref[...])

    @pl.when(j > 0)
    def _():
        h = h_ref[...]
        o_ref[0] += down(jnp.dot(h, wug_ref[...], preferred_element_type=F32),
                         jnp.dot(h, wuv_ref[...], preferred_element_type=F32))

    @pl.when(j == pl.num_programs(2) - 1)
    def _():
        y = x_ref[0] + gt_ref[0] * o_ref[0]
        if final:
            y = y * lax.rsqrt(jnp.mean(y * y, axis=-1, keepdims=True) + EPS) * fw_ref[...]
        o_ref[0] = y


def _ffn(x, nw, shift, scale, gate, w_up, conv_w, conv_b, w_down, l, tm, final_w=None):
    b, t, d = x.shape
    final = final_w is not None
    if not final:
        final_w = nw
    dff = w_down.shape[1]
    tf = 512
    nj = dff // tf
    per = tm // HALO
    last = t // HALO - 1
    vec = pl.BlockSpec((1, 1, d), lambda bb, i, j: (bb, 0, 0))
    return pl.pallas_call(
        functools.partial(_ffn_kernel, tm=tm, final=final),
        grid=(b, t // tm, nj),
        in_specs=[
            pl.BlockSpec((1, tm, d), lambda bb, i, j: (bb, i, 0)),
            pl.BlockSpec((1, HALO, d), lambda bb, i, j: (bb, jnp.maximum(i * per - 1, 0), 0)),
            pl.BlockSpec((1, HALO, d), lambda bb, i, j: (bb, jnp.minimum((i + 1) * per, last), 0)),
            pl.BlockSpec((1, d), lambda bb, i, j: (0, 0)),
            vec, vec, vec,
            pl.BlockSpec((None, d, tf), lambda bb, i, j: (l, 0, j)),
            pl.BlockSpec((None, d, tf), lambda bb, i, j: (l, 0, nj + j)),
            pl.BlockSpec((FFN_CONV, tf), lambda bb, i, j: (0, j)),
            pl.BlockSpec((FFN_CONV, tf), lambda bb, i, j: (0, nj + j)),
            pl.BlockSpec((1, tf), lambda bb, i, j: (0, j)),
            pl.BlockSpec((1, tf), lambda bb, i, j: (0, nj + j)),
            pl.BlockSpec((None, tf, d), lambda bb, i, j: (l, j, 0)),
            pl.BlockSpec((1, d), lambda bb, i, j: (0, 0)),
        ],
        out_specs=pl.BlockSpec((1, tm, d), lambda bb, i, j: (bb, i, 0), pipeline_mode=pl.Buffered(1)),
        out_shape=jax.ShapeDtypeStruct((b, t, d), F32),
        scratch_shapes=[
            pltpu.VMEM((tm + 2 * HALO, d), BF16),
            pltpu.VMEM((tm + 2 * HALO, tf), F32),
            pltpu.VMEM((tm + 2 * HALO, tf), F32),
        ],
        compiler_params=_cparams(("parallel", "parallel", "arbitrary")),
        name="conv_ffn",
    )(x, x, x, nw, shift, scale, gate, w_up, w_up, conv_w, conv_w, conv_b, conv_b, w_down, final_w)


def _rope_tables(n_tokens):
    rows = n_tokens // GRID_W
    row = jnp.repeat(jnp.arange(rows), GRID_W).astype(F32)
    col = jnp.tile(jnp.arange(GRID_W), rows).astype(F32)
    inv = ROPE_THETA ** (-jnp.arange(0, ROPE_AXIS_DIM, 2, dtype=F32) / ROPE_AXIS_DIM)
    ar, ac = row[:, None] * inv, col[:, None] * inv
    cos = jnp.concatenate([jnp.cos(ar), jnp.cos(ar), jnp.cos(ac), jnp.cos(ac)], axis=1)
    sin = jnp.concatenate([-jnp.sin(ar), jnp.sin(ar), -jnp.sin(ac), jnp.sin(ac)], axis=1)
    return cos, sin


def _expansion_matrix(direction):
    head_of_col = jnp.arange(SSM_WIDTH) // SSM_HEAD_DIM
    row = jnp.arange(LANES)[:, None]
    e = (row == head_of_col[None, :] + SSM_HEADS * direction).astype(BF16)
    return jnp.concatenate([e, e], axis=0)


def _pad_lanes(v, offset):
    return jnp.zeros((1, LANES), F32).at[0, offset:offset + v.shape[0]].set(v.astype(F32))


def kernel(x, c, ctx, c_ctx, w_mod, b_mod, norm_mix_w, norm_ffn_w, w_in, q_norm_w, k_norm_w, w_attn_o, conf_conv_w, conf_conv_b, conf_ln_w, conf_ln_b, w_conf_o, ssm_conv_w, ssm_conv_b, ssm_a_log, ssm_dt_bias, ssm_d, ssm_norm_w, w_ssm_o, w_out, ffn_w_up, ffn_conv_w, ffn_conv_b, ffn_w_down, final_norm_w):
    bsz, n_lat, d = x.shape
    n_ctx = ctx.shape[1]
    depth = w_mod.shape[0]
    assert bsz <= 7 and n_lat % 512 == 0 and n_ctx % 256 == 0

    q_end = ATTN_WIDTH
    v_end = q_end + 2 * KV_WIDTH
    conf_end = v_end + 2 * CONF_WIDTH
    z_end = conf_end + SSM_WIDTH
    xbc_end = z_end + SSM_WIDTH + SSM_BC
    dt_end = xbc_end + 2 * SSM_HEADS

    cc = jnp.zeros((8, d), F32).at[:bsz].set(c).at[bsz].set(c_ctx)
    mods = _mod_vectors(cc, w_mod, b_mod).reshape(depth, 8, N_MOD, d)
    rope = _rope_tables(n_lat)
    e_mats = [_expansion_matrix(0), _expansion_matrix(1)]
    zero_state = jnp.zeros((bsz, SSM_GROUPS, SSM_STATE, SSM_GROUP_WIDTH), F32)

    w_mix, w_gate = _winprep(w_in, dict(q_end=q_end, v_end=v_end, xs_end=z_end + SSM_WIDTH, bc_end=xbc_end,
                                        dt_end=dt_end))
    wa, wb, wc, wo = (w.astype(BF16) for w in (w_attn_o, w_conf_o, w_ssm_o, w_out))
    w_up, w_down = ffn_w_up.astype(BF16), ffn_w_down.astype(BF16)

    for l in range(depth):
        need_ctx = l < depth - 1
        row = lambda v: v.reshape(1, -1)
        nmix, nffn = row(norm_mix_w[l]), row(norm_ffn_w[l])
        cwx, cwb = ssm_conv_w[l][:, :SSM_WIDTH], ssm_conv_w[l][:, SSM_WIDTH:]
        cbx, cbb = row(ssm_conv_b[l][:SSM_WIDTH]), row(ssm_conv_b[l][SSM_WIDTH:])
        a_coef = -jnp.exp(ssm_a_log[l].astype(F32))
        a_rows = [_pad_lanes(a_coef[0], 0), _pad_lanes(a_coef[1], SSM_HEADS)]
        dtb = _pad_lanes(ssm_dt_bias[l].reshape(-1), 0)
        dsum_e = row(jnp.repeat(ssm_d[l][0] + ssm_d[l][1], SSM_HEAD_DIM))
        m_lat = [mods[l, :bsz, k].reshape(bsz, 1, d) for k in range(N_MOD)]
        m_ctx = [jnp.broadcast_to(mods[l, bsz, k].reshape(1, 1, d), (bsz, 1, d)) for k in range(N_MOD)]

        proj_l = _inproj(x, nmix, m_lat[0], m_lat[1], w_mix, l, 1024 if n_lat % 1024 == 0 else 512)
        proj_c = _inproj(ctx, nmix, m_ctx[0], m_ctx[1], w_mix, l, 256)

        q_l, k_l, v_l = _qkprep(proj_l, row(q_norm_w[l]), row(k_norm_w[l]), rope, 512)
        q_c, k_c, v_c = _qkprep(proj_c, row(q_norm_w[l]), row(k_norm_w[l]), None, 256)
        act_a_l = _attention(q_l, [(k_c, v_c), (k_l, v_l)], 512, (256, 512))
        conf_args = (conf_conv_w[l], row(conf_conv_b[l]), row(conf_ln_w[l]), row(conf_ln_b[l]))
        act_b_l = _conformer(proj_l, *conf_args, 256)

        prep_l = _ssdprep(proj_l, cwx, cbx, cwb, cbb, dtb, 512)
        prep_c = _ssdprep(proj_c, cwx, cbx, cwb, cbb, dtb, 256)
        fin_args = (dsum_e, row(ssm_norm_w[l]))
        if need_ctx:
            y_cf, h_cf = _ssd_scan(*prep_c, a_rows[0], e_mats[0], zero_state, 0, True)
            h_cb, act_c_c = _ssd_scan(*prep_c, a_rows[1], e_mats[1], zero_state, 1, True, (y_cf, proj_c) + fin_args)
        else:
            (h_cf,) = _ssd_scan(*prep_c, a_rows[0], e_mats[0], zero_state, 0, False)
            (h_cb,) = _ssd_scan(*prep_c, a_rows[1], e_mats[1], zero_state, 1, False)
        y_lf, _ = _ssd_scan(*prep_l, a_rows[0], e_mats[0], h_cf, 0, True)
        _, act_c_l = _ssd_scan(*prep_l, a_rows[1], e_mats[1], h_cb, 1, True, (y_lf, proj_l) + fin_args)

        x = _merge(x, nmix, m_lat[0], m_lat[1], m_lat[2], act_a_l, act_b_l, act_c_l, w_gate, wa, wb, wc, wo, l, 512)
        ffn_args = (w_up, ffn_conv_w[l], row(ffn_conv_b[l]), w_down, l)
        x = _ffn(x, nffn, m_lat[3], m_lat[4], m_lat[5], *ffn_args, 1024 if n_lat % 1024 == 0 else 512,
                 final_w=None if need_ctx else final_norm_w.reshape(1, d))
        if need_ctx:
            act_a_c = _attention(q_c, [(k_c, v_c)], 256, (256,))
            act_b_c = _conformer(proj_c, *conf_args, 256)
            ctx = _merge(ctx, nmix, m_ctx[0], m_ctx[1], m_ctx[2], act_a_c, act_b_c, act_c_c, w_gate, wa, wb, wc, wo, l, 256)
            ctx = _ffn(ctx, nffn, m_ctx[3], m_ctx[4], m_ctx[5], *ffn_args, 256)
    return x
```

```python
import functools
import math

import jax
import jax.numpy as jnp
from jax import lax
from jax.experimental import pallas as pl
from jax.experimental.pallas import tpu as pltpu

F32 = jnp.float32
BF16 = jnp.bfloat16

EPS = 1e-6
N_MOD = 6
GRID_W = 64
HEAD_DIM = 128
ATTN_HEADS = 8
ATTN_KV_HEADS = 2
ATTN_REP = ATTN_HEADS // ATTN_KV_HEADS
ATTN_WIDTH = ATTN_HEADS * HEAD_DIM
KV_WIDTH = ATTN_KV_HEADS * HEAD_DIM
ROPE_THETA = 10000.0
ROPE_AXIS_DIM = HEAD_DIM // 2
ROPE_FREQS = ROPE_AXIS_DIM // 2
CONF_WIDTH = 1024
CONF_KERNEL = 31
SSM_HEADS = 16
SSM_HEAD_DIM = 64
SSM_WIDTH = SSM_HEADS * SSM_HEAD_DIM
SSM_GROUPS = 2
SSM_STATE = 128
SSM_GROUP_WIDTH = SSM_WIDTH // SSM_GROUPS
SSM_BC = 2 * SSM_GROUPS * SSM_STATE
SSM_CONV = 5
SSM_CHUNK = 128
FFN_CONV = 3
N_BRANCHES = 3

LANES = 128
HALO = 16
VMEM_LIMIT = 56 * 1024 * 1024
LOG2E = 1.4426950408889634
ATTN_QSCALE = HEAD_DIM ** -0.5 * LOG2E
VT_PAD = 16
VT_ROWS = HEAD_DIM + VT_PAD
NEG_BIG = -1e30

COL_Q = 0
COL_A = 1024
COL_G = 2048
COL_Z = 3072
COL_XS = 4096
COL_KV = 5120
COL_BC = 5632
COL_DT = 6144
MIX_WIDTH = 6400


def _cparams(sem):
    return pltpu.CompilerParams(dimension_semantics=sem, vmem_limit_bytes=VMEM_LIMIT)


def _sigmoid(x):
    return 1.0 / (1.0 + jnp.exp(-x))


def _silu(x):
    return x * _sigmoid(x)


def _rms_mod(x, nw, shift, scale):
    var = jnp.mean(x * x, axis=-1, keepdims=True)
    y = x * lax.rsqrt(var + EPS) * nw
    return y * (1.0 + scale) + shift


def _mod_kernel(c_ref, w_ref, b_ref, o_ref):
    s = _silu(c_ref[...]).astype(BF16)
    o_ref[0] = jnp.dot(s, w_ref[0].astype(BF16), preferred_element_type=F32) + b_ref[0]


def _mod_vectors(cc, w_mod, b_mod):
    depth, d, nm = w_mod.shape
    tn = 2048
    return pl.pallas_call(
        _mod_kernel,
        grid=(depth, nm // tn),
        in_specs=[
            pl.BlockSpec((8, d), lambda l, j: (0, 0)),
            pl.BlockSpec((1, d, tn), lambda l, j: (l, 0, j)),
            pl.BlockSpec((1, 1, tn), lambda l, j: (l, 0, j)),
        ],
        out_specs=pl.BlockSpec((1, 8, tn), lambda l, j: (l, 0, j)),
        out_shape=jax.ShapeDtypeStruct((depth, 8, nm), F32),
        compiler_params=_cparams(("parallel", "parallel")),
        name="mod_vectors",
    )(cc, w_mod, b_mod.reshape(depth, 1, nm))


ROW_CHUNK = 256


def _winprep_kernel(wt_ref, mix_ref, gate_ref, *, q_end, v_end, xs_end, bc_end, dt_end):
    def cols(lo, hi):
        return wt_ref[lo:hi, :].T.astype(BF16)

    mix_ref[:, COL_Q:COL_A] = cols(0, q_end)
    mix_ref[:, COL_A:COL_KV] = cols(v_end, xs_end)
    mix_ref[:, COL_KV:COL_BC] = cols(q_end, v_end)
    mix_ref[:, COL_BC:COL_DT] = cols(xs_end, bc_end)
    tail = wt_ref[bc_end:bc_end + LANES, :].T
    lane = lax.broadcasted_iota(jnp.int32, tail.shape, 1)
    mix_ref[:, COL_DT:COL_DT + LANES] = jnp.where(lane < dt_end - bc_end, tail, 0.0).astype(BF16)
    mix_ref[:, COL_DT + LANES:] = jnp.zeros((tail.shape[0], MIX_WIDTH - COL_DT - LANES), BF16)
    gate_ref[...] = cols(dt_end, wt_ref.shape[0])


def _winprep(w_in, offsets):
    depth, d, n = w_in.shape
    dt_end = offsets["dt_end"]
    tr = 256
    return pl.pallas_call(
        functools.partial(_winprep_kernel, **offsets),
        grid=(depth, d // tr),
        in_specs=[pl.BlockSpec((None, n, tr), lambda l, i: (l, 0, i))],
        out_specs=[
            pl.BlockSpec((None, tr, MIX_WIDTH), lambda l, i: (l, i, 0)),
            pl.BlockSpec((None, tr, n - dt_end), lambda l, i: (l, i, 0)),
        ],
        out_shape=[
            jax.ShapeDtypeStruct((depth, d, MIX_WIDTH), BF16),
            jax.ShapeDtypeStruct((depth, d, n - dt_end), BF16),
        ],
        compiler_params=_cparams(("parallel", "parallel")),
        name="w_in_prep",
    )(jnp.swapaxes(w_in, 1, 2))


def _inproj_kernel(x_ref, nw_ref, sh_ref, sc_ref, w_ref, o_ref, h_ref):
    j = pl.program_id(2)

    @pl.when(j == 0)
    def _():
        for r in range(0, h_ref.shape[0], ROW_CHUNK):
            rows = slice(r, r + ROW_CHUNK)
            hc = _rms_mod(x_ref[0, rows], nw_ref[...], sh_ref[0], sc_ref[0]).astype(BF16)
            h_ref[rows] = hc
            o_ref[0, rows] = jnp.dot(hc, w_ref[...], preferred_element_type=F32)

    @pl.when(j > 0)
    def _():
        o_ref[0] = jnp.dot(h_ref[...], w_ref[...], preferred_element_type=F32)


def _inproj(x, nw, shift, scale, w, l, tm):
    b, t, d = x.shape
    n = w.shape[2]
    tn = 1280
    return pl.pallas_call(
        _inproj_kernel,
        grid=(b, t // tm, n // tn),
        in_specs=[
            pl.BlockSpec((1, tm, d), lambda bb, i, j: (bb, i, 0)),
            pl.BlockSpec((1, d), lambda bb, i, j: (0, 0)),
            pl.BlockSpec((1, 1, d), lambda bb, i, j: (bb, 0, 0)),
            pl.BlockSpec((1, 1, d), lambda bb, i, j: (bb, 0, 0)),
            pl.BlockSpec((None, d, tn), lambda bb, i, j: (l, 0, j)),
        ],
        out_specs=pl.BlockSpec((1, tm, tn), lambda bb, i, j: (bb, i, j)),
        out_shape=jax.ShapeDtypeStruct((b, t, n), F32),
        scratch_shapes=[pltpu.VMEM((tm, d), BF16)],
        compiler_params=_cparams(("parallel", "parallel", "arbitrary")),
        name="mixer_inproj",
    )(x, nw, shift, scale, w)


def _head_norm(x, w):
    var = jnp.mean(x * x, axis=-1, keepdims=True)
    return x * lax.rsqrt(var + EPS) * w


def _rope(x, c, s):
    lane = lax.broadcasted_iota(jnp.int32, x.shape, 1)
    first = (lane % ROPE_AXIS_DIM) < ROPE_FREQS
    partner = jnp.where(first, pltpu.roll(x, HEAD_DIM - ROPE_FREQS, 1), pltpu.roll(x, ROPE_FREQS, 1))
    return x * c + partner * s


def _qkprep_kernel(*refs, use_rope):
    if use_rope:
        q_ref, kv_ref, qw_ref, kw_ref, c_ref, s_ref, qo_ref, ko_ref, vo_ref = refs
    else:
        q_ref, kv_ref, qw_ref, kw_ref, qo_ref, ko_ref, vo_ref = refs
    q = q_ref[0]
    kv = kv_ref[0]
    for h in range(ATTN_HEADS):
        y = _head_norm(q[:, h * HEAD_DIM:(h + 1) * HEAD_DIM], qw_ref[...])
        if use_rope:
            y = _rope(y, c_ref[...], s_ref[...])
        qo_ref[0, :, h * HEAD_DIM:(h + 1) * HEAD_DIM] = (y * ATTN_QSCALE).astype(BF16)
    for h in range(ATTN_KV_HEADS):
        y = _head_norm(kv[:, h * HEAD_DIM:(h + 1) * HEAD_DIM], kw_ref[...])
        if use_rope:
            y = _rope(y, c_ref[...], s_ref[...])
        ko_ref[0, :, h * HEAD_DIM:(h + 1) * HEAD_DIM] = y.astype(BF16)
    vt = kv[:, KV_WIDTH:].T.astype(BF16)
    ones = jnp.ones((VT_PAD, vt.shape[1]), BF16)
    for g in range(ATTN_KV_HEADS):
        vo_ref[0, g * VT_ROWS:g * VT_ROWS + HEAD_DIM] = vt[g * HEAD_DIM:(g + 1) * HEAD_DIM]
        vo_ref[0, g * VT_ROWS + HEAD_DIM:(g + 1) * VT_ROWS] = ones


def _qkprep(proj, qw, kw, rope, tm):
    b, t, _ = proj.shape
    use_rope = rope is not None
    in_specs = [
        pl.BlockSpec((1, tm, ATTN_WIDTH), lambda bb, i: (bb, i, COL_Q // ATTN_WIDTH)),
        pl.BlockSpec((1, tm, 2 * KV_WIDTH), lambda bb, i: (bb, i, COL_KV // (2 * KV_WIDTH))),
        pl.BlockSpec((1, HEAD_DIM), lambda bb, i: (0, 0)),
        pl.BlockSpec((1, HEAD_DIM), lambda bb, i: (0, 0)),
    ]
    args = [proj, proj, qw, kw]
    if use_rope:
        in_specs += [pl.BlockSpec((tm, HEAD_DIM), lambda bb, i: (i, 0))] * 2
        args += list(rope)
    return pl.pallas_call(
        functools.partial(_qkprep_kernel, use_rope=use_rope),
        grid=(b, t // tm),
        in_specs=in_specs,
        out_specs=[
            pl.BlockSpec((1, tm, ATTN_WIDTH), lambda bb, i: (bb, i, 0)),
            pl.BlockSpec((1, tm, KV_WIDTH), lambda bb, i: (bb, i, 0)),
            pl.BlockSpec((1, ATTN_KV_HEADS * VT_ROWS, tm), lambda bb, i: (bb, 0, i)),
        ],
        out_shape=[
            jax.ShapeDtypeStruct((b, t, ATTN_WIDTH), BF16),
            jax.ShapeDtypeStruct((b, t, KV_WIDTH), BF16),
            jax.ShapeDtypeStruct((b, ATTN_KV_HEADS * VT_ROWS, t), BF16),
        ],
        compiler_params=_cparams(("parallel", "parallel")),
        name="qk_prep",
    )(*args)


def _attn_kernel(*refs, tq, seg_tk, seg_chunks):
    n_seg = len(seg_chunks)
    q_ref = refs[0]
    kv_refs = refs[1:1 + 2 * n_seg]
    o_ref, s0_ref, s1_ref, acc_ref = refs[1 + 2 * n_seg:]
    q = q_ref[0]
    qs = jnp.concatenate([q[:, r * HEAD_DIM:(r + 1) * HEAD_DIM] for r in range(ATTN_REP)], axis=0)
    nq = ATTN_REP * tq
    chunks = [(kv_refs[2 * s], kv_refs[2 * s + 1], slice(c * seg_tk[s], (c + 1) * seg_tk[s]))
              for s in range(n_seg) for c in range(seg_chunks[s])]
    s_refs = (s0_ref, s1_ref)

    def scores(j):
        k_ref, _, rows = chunks[j]
        s_refs[j % 2][0:rows.stop - rows.start, :] = lax.dot_general(
            k_ref[0, rows, :], qs, (((1,), (1,)), ((), ())), preferred_element_type=F32)

    def update(j, m):
        _, vt_ref, cols = chunks[j]
        s = s_refs[j % 2][0:cols.stop - cols.start, :]
        m_new = jnp.maximum(m, jnp.max(s, axis=0, keepdims=True))
        alpha = jnp.exp2(m - m_new)
        p = jnp.exp2(s - m_new).astype(BF16)
        acc_ref[...] = alpha * acc_ref[...] + jnp.dot(vt_ref[0, :, cols], p, preferred_element_type=F32)
        return m_new

    scores(0)
    acc_ref[...] = jnp.zeros_like(acc_ref)
    m = jnp.full((1, nq), NEG_BIG, F32)
    for j in range(len(chunks)):
        if j + 1 < len(chunks):
            scores(j + 1)
        m = update(j, m)
    o = acc_ref[:HEAD_DIM] / acc_ref[HEAD_DIM:HEAD_DIM + 1]
    for r in range(ATTN_REP):
        o_ref[0, :, r * HEAD_DIM:(r + 1) * HEAD_DIM] = o[:, r * tq:(r + 1) * tq].T.astype(BF16)


def _attention(q, kv_segments, tq, tks):
    b, t, _ = q.shape
    gw = ATTN_REP * HEAD_DIM
    nq = ATTN_REP * tq
    in_specs = [pl.BlockSpec((1, tq, gw), lambda bb, g, i: (bb, i, g))]
    args = [q]
    for k, vt in kv_segments:
        tkv = k.shape[1]
        in_specs += [
            pl.BlockSpec((1, tkv, HEAD_DIM), lambda bb, g, i: (bb, 0, g)),
            pl.BlockSpec((1, VT_ROWS, tkv), lambda bb, g, i: (bb, g, 0)),
        ]
        args += [k, vt]
    return pl.pallas_call(
        functools.partial(_attn_kernel, tq=tq, seg_tk=tuple(tks),
                          seg_chunks=tuple(k.shape[1] // tk for (k, _), tk in zip(kv_segments, tks))),
        grid=(b, ATTN_KV_HEADS, t // tq),
        in_specs=in_specs,
        out_specs=pl.BlockSpec((1, tq, gw), lambda bb, g, i: (bb, i, g)),
        out_shape=jax.ShapeDtypeStruct((b, t, ATTN_WIDTH), BF16),
        scratch_shapes=[pltpu.VMEM((max(tks), nq), F32), pltpu.VMEM((max(tks), nq), F32),
                        pltpu.VMEM((VT_ROWS, nq), F32)],
        compiler_params=_cparams(("parallel", "parallel", "arbitrary")),
        name="attention",
    )(*args)


def _halo_specs(tm, t, width, col_block):
    per = tm // HALO
    last = t // HALO - 1
    return [
        pl.BlockSpec((1, tm, width), lambda bb, i: (bb, i, col_block)),
        pl.BlockSpec((1, HALO, width), lambda bb, i: (bb, jnp.maximum(i * per - 1, 0), col_block)),
        pl.BlockSpec((1, HALO, width), lambda bb, i: (bb, jnp.minimum((i + 1) * per, last), col_block)),
    ]


SUBLANES = 8


def _conv_rows(ext_ref, w_ref, taps, first_row, tm, width, out_fn, shift_ref=None, rows_per=64):
    if shift_ref is not None:
        n_shift = shift_ref.shape[1]
        for c0 in range(0, width, LANES):
            for r in range(1, SUBLANES):
                shift_ref[r - 1, :, c0:c0 + LANES] = ext_ref[r:r + n_shift, c0:c0 + LANES]
    for c0 in range(0, width, LANES):
        for r0 in range(0, tm, rows_per):
            acc = None
            for k in range(taps):
                lo = first_row + r0 + k
                res = lo % SUBLANES
                if shift_ref is None or res == 0:
                    rows = ext_ref[lo:lo + rows_per, c0:c0 + LANES]
                else:
                    rows = shift_ref[res - 1, lo - res:lo - res + rows_per, c0:c0 + LANES]
                term = rows * w_ref[k:k + 1, c0:c0 + LANES]
                acc = term if acc is None else acc + term
            out_fn(r0, rows_per, c0, acc)


def _conf_kernel(a_ref, ap_ref, an_ref, g_ref, gp_ref, gn_ref, cw_ref, cb_ref, lw_ref, lb_ref,
                 o_ref, u_ref, v_ref, us_ref, *, tm):
    i = pl.program_id(1)
    nt = pl.num_programs(1)
    u_ref[HALO:HALO + tm] = a_ref[0] * _sigmoid(g_ref[0])
    u_ref[0:HALO] = jnp.where(i > 0, ap_ref[0] * _sigmoid(gp_ref[0]), 0.0)
    u_ref[HALO + tm:] = jnp.where(i < nt - 1, an_ref[0] * _sigmoid(gn_ref[0]), 0.0)

    def put(r0, nr, c0, acc):
        v_ref[r0:r0 + nr, c0:c0 + LANES] = acc + cb_ref[:, c0:c0 + LANES]

    pad = (CONF_KERNEL - 1) // 2
    _conv_rows(u_ref, cw_ref, CONF_KERNEL, HALO - pad, tm, CONF_WIDTH, put, shift_ref=us_ref)
    v = v_ref[...]
    mu = jnp.mean(v, axis=-1, keepdims=True)
    d = v - mu
    var = jnp.mean(d * d, axis=-1, keepdims=True)
    y = d * lax.rsqrt(var + EPS) * lw_ref[...] + lb_ref[...]
    o_ref[0] = _silu(y).astype(BF16)


def _conformer(proj, cw, cb, lw, lb, tm):
    b, t, _ = proj.shape
    vec = pl.BlockSpec((1, CONF_WIDTH), lambda bb, i: (0, 0))
    return pl.pallas_call(
        functools.partial(_conf_kernel, tm=tm),
        grid=(b, t // tm),
        in_specs=_halo_specs(tm, t, CONF_WIDTH, COL_A // CONF_WIDTH)
        + _halo_specs(tm, t, CONF_WIDTH, COL_G // CONF_WIDTH)
        + [pl.BlockSpec((CONF_KERNEL, CONF_WIDTH), lambda bb, i: (0, 0)), vec, vec, vec],
        out_specs=pl.BlockSpec((1, tm, CONF_WIDTH), lambda bb, i: (bb, i, 0)),
        out_shape=jax.ShapeDtypeStruct((b, t, CONF_WIDTH), BF16),
        scratch_shapes=[
            pltpu.VMEM((tm + 2 * HALO, CONF_WIDTH), F32),
            pltpu.VMEM((tm, CONF_WIDTH), F32),
            pltpu.VMEM((SUBLANES - 1, tm + 2 * HALO - SUBLANES, CONF_WIDTH), F32),
        ],
        compiler_params=_cparams(("parallel", "parallel")),
        name="conformer",
    )(proj, proj, proj, proj, proj, proj, cw, cb, lw, lb)


def _ssdprep_kernel(x_ref, xp_ref, xn_ref, bc_ref, bcp_ref, bcn_ref, dt_ref, cwx_ref, cbx_ref,
                    cwb_ref, cbb_ref, dtb_ref, xo_ref, bco_ref, bmt_ref, dto_ref, xe_ref, be_ref, bv_ref, *, tm):
    i = pl.program_id(1)
    nt = pl.num_programs(1)
    xe_ref[HALO:HALO + tm] = x_ref[0]
    xe_ref[0:HALO] = jnp.where(i > 0, xp_ref[0], 0.0)
    xe_ref[HALO + tm:] = jnp.where(i < nt - 1, xn_ref[0], 0.0)
    be_ref[HALO:HALO + tm] = bc_ref[0]
    be_ref[0:HALO] = jnp.where(i > 0, bcp_ref[0], 0.0)
    be_ref[HALO + tm:] = jnp.where(i < nt - 1, bcn_ref[0], 0.0)
    pad = (SSM_CONV - 1) // 2

    def put_x(r0, nr, c0, acc):
        xo_ref[0, r0:r0 + nr, c0:c0 + LANES] = _silu(acc + cbx_ref[:, c0:c0 + LANES])

    def put_b(r0, nr, c0, acc):
        bv_ref[r0:r0 + nr, c0:c0 + LANES] = _silu(acc + cbb_ref[:, c0:c0 + LANES])

    _conv_rows(xe_ref, cwx_ref, SSM_CONV, HALO - pad, tm, SSM_WIDTH, put_x)
    _conv_rows(be_ref, cwb_ref, SSM_CONV, HALO - pad, tm, SSM_BC, put_b)
    bv = bv_ref[...]
    bco_ref[0] = bv.astype(BF16)
    bmt_ref[0] = bv[:, :SSM_GROUPS * SSM_STATE].T.astype(BF16)
    x = dt_ref[0] + dtb_ref[...]
    softplus = jnp.maximum(x, 0.0) + jnp.log(1.0 + jnp.exp(-jnp.abs(x)))
    lane = lax.broadcasted_iota(jnp.int32, x.shape, 1)
    dto_ref[0] = jnp.where(lane < 2 * SSM_HEADS, softplus, 0.0)


def _ssdprep(proj, cwx, cbx, cwb, cbb, dtb, tm):
    b, t, _ = proj.shape
    gs = SSM_GROUPS * SSM_STATE
    return pl.pallas_call(
        functools.partial(_ssdprep_kernel, tm=tm),
        grid=(b, t // tm),
        in_specs=_halo_specs(tm, t, SSM_WIDTH, COL_XS // SSM_WIDTH)
        + _halo_specs(tm, t, SSM_BC, COL_BC // SSM_BC)
        + [
            pl.BlockSpec((1, tm, LANES), lambda bb, i: (bb, i, COL_DT // LANES)),
            pl.BlockSpec((SSM_CONV, SSM_WIDTH), lambda bb, i: (0, 0)),
            pl.BlockSpec((1, SSM_WIDTH), lambda bb, i: (0, 0)),
            pl.BlockSpec((SSM_CONV, SSM_BC), lambda bb, i: (0, 0)),
            pl.BlockSpec((1, SSM_BC), lambda bb, i: (0, 0)),
            pl.BlockSpec((1, LANES), lambda bb, i: (0, 0)),
        ],
        out_specs=[
            pl.BlockSpec((1, tm, SSM_WIDTH), lambda bb, i: (bb, i, 0)),
            pl.BlockSpec((1, tm, SSM_BC), lambda bb, i: (bb, i, 0)),
            pl.BlockSpec((1, gs, tm), lambda bb, i: (bb, 0, i)),
            pl.BlockSpec((1, tm, LANES), lambda bb, i: (bb, i, 0)),
        ],
        out_shape=[
            jax.ShapeDtypeStruct((b, t, SSM_WIDTH), F32),
            jax.ShapeDtypeStruct((b, t, SSM_BC), BF16),
            jax.ShapeDtypeStruct((b, gs, t), BF16),
            jax.ShapeDtypeStruct((b, t, LANES), F32),
        ],
        scratch_shapes=[
            pltpu.VMEM((tm + 2 * HALO, SSM_WIDTH), F32),
            pltpu.VMEM((tm + 2 * HALO, SSM_BC), F32),
            pltpu.VMEM((tm, SSM_BC), F32),
        ],
        compiler_params=_cparams(("parallel", "parallel")),
        name="ssd_prep",
    )(proj, proj, proj, proj, proj, proj, proj, cwx, cbx, cwb, cbb, dtb)


def _split3(v, axis):
    hi = v.astype(BF16)
    r1 = v - hi.astype(F32)
    mid = r1.astype(BF16)
    lo = (r1 - mid.astype(F32)).astype(BF16)
    return jnp.concatenate([hi, mid, lo], axis=axis)


def _split_hi_lo(v):
    hi = v.astype(BF16)
    lo = (v - hi.astype(F32)).astype(BF16)
    return jnp.concatenate([hi, lo], axis=1)


def _ssd_kernel(*refs, direction, with_y, finish, nb):
    it = iter(refs)
    xs_ref, bc_ref, bmt_ref, dt_ref, a_ref, e_ref, h0_ref = (next(it) for _ in range(7))
    if finish:
        yf_ref, z_ref, dsum_ref, nw_ref = (next(it) for _ in range(4))
    if with_y:
        y_ref = next(it)
    hfin_ref = next(it)
    st_ref = next(it)
    if finish:
        out_ref = next(it)
    q = SSM_CHUNK
    c = pl.program_id(1)

    @pl.when(c == 0)
    def _():
        st_ref[...] = h0_ref[...]

    ii = lax.broadcasted_iota(jnp.int32, (q, q), 0)
    jj = lax.broadcasted_iota(jnp.int32, (q, q), 1)
    if direction == 0:
        valid, valid_t, end_row = jj <= ii, ii <= jj, q - 1
    else:
        valid, valid_t, end_row = jj >= ii, ii >= jj, 0
    tri = jnp.where(valid, 1.0, 0.0).astype(BF16)
    tri_t = jnp.where(valid_t, 1.0, 0.0).astype(BF16)
    tri3 = jnp.concatenate([tri, tri, tri], axis=1)
    tri3_t = jnp.concatenate([tri_t, tri_t, tri_t], axis=0)
    e = e_ref[...]

    def expand(v):
        return jnp.dot(_split_hi_lo(v), e, preferred_element_type=F32)

    gw = SSM_GROUP_WIDTH
    hpg = SSM_HEADS // SSM_GROUPS
    for bi in range(nb):
        dt = dt_ref[bi]
        da = dt * a_ref[...]
        acum_c = jnp.dot(tri3, _split3(da, 0), preferred_element_type=F32)
        acum_r = jnp.dot(_split3(da.T, 1), tri3_t, preferred_element_type=F32)
        tot = acum_c[end_row:end_row + 1, :]
        to_end = jnp.exp(tot - acum_c)
        ea = jnp.exp(acum_c)
        ea_e = expand(ea)
        xs = xs_ref[bi]
        xw = (xs * expand(dt * to_end)).astype(BF16)
        cdec_e = ea_e[end_row:end_row + 1, :]
        bc = bc_ref[bi]
        bmt = bmt_ref[bi]
        if with_y:
            xdt = (xs * expand(dt)).astype(BF16)
            lane = lax.broadcasted_iota(jnp.int32, xdt.shape, 1)
            left = (lane % LANES) < SSM_HEAD_DIM
            zero = jnp.zeros_like(xdt)
            xdt_l = jnp.where(left, xdt, zero)
            xdt_r = jnp.where(left, zero, xdt)
        for g in range(SSM_GROUPS):
            bm_t = bmt[g * SSM_STATE:(g + 1) * SSM_STATE, :]
            st = st_ref[bi, g]
            if with_y:
                cm = bc[:, (SSM_GROUPS + g) * SSM_STATE:(SSM_GROUPS + g + 1) * SSM_STATE]
                bm = bc[:, g * SSM_STATE:(g + 1) * SSM_STATE]
                cb = lax.dot_general(cm, bm, (((1,), (1,)), ((), ())), preferred_element_type=F32)
                y_inter = jnp.dot(cm, st.astype(BF16), preferred_element_type=F32) * ea_e[:, g * gw:(g + 1) * gw]
                parts = []
                for k in range(hpg // 2):
                    ms = []
                    for r in (2 * k, 2 * k + 1):
                        hl = SSM_HEADS * direction + g * hpg + r
                        seg = acum_c[:, hl:hl + 1] - acum_r[hl:hl + 1, :]
                        ms.append((cb * jnp.where(valid, jnp.exp(seg), 0.0)).astype(BF16))
                    c0 = g * gw + k * LANES
                    rhs = jnp.concatenate([xdt_l[:, c0:c0 + LANES], xdt_r[:, c0:c0 + LANES]], axis=0)
                    parts.append(jnp.dot(jnp.concatenate(ms, axis=1), rhs, preferred_element_type=F32))
                y_g = y_inter + jnp.concatenate(parts, axis=1)
                sl = slice(g * gw, (g + 1) * gw)
                if finish:
                    y_ref[bi, :, sl] = yf_ref[bi, :, sl] + y_g + dsum_ref[:, sl] * xs[:, sl]
                else:
                    y_ref[bi, :, sl] = y_g
            st_ref[bi, g] = st * cdec_e[:, g * gw:(g + 1) * gw] + jnp.dot(
                bm_t, xw[:, g * gw:(g + 1) * gw], preferred_element_type=F32)
        if finish:
            gz = y_ref[bi] * _silu(z_ref[bi])
            var = jnp.mean(gz * gz, axis=-1, keepdims=True)
            out_ref[bi] = (gz * lax.rsqrt(var + EPS) * nw_ref[...]).astype(BF16)

    @pl.when(c == pl.num_programs(1) - 1)
    def _():
        hfin_ref[...] = st_ref[...]


def _ssd_scan(xs, bc, bmt, dt, a_row, e_mat, h0, direction, with_y, fin=None):
    b, t, _ = xs.shape
    nb = 4 if b % 4 == 0 else (2 if b % 2 == 0 else 1)
    nc = t // SSM_CHUNK
    q = SSM_CHUNK
    gs = SSM_GROUPS * SSM_STATE
    finish = fin is not None
    if direction == 0:
        tmap = lambda bb, c: (bb, c, 0)
        tmap_t = lambda bb, c: (bb, 0, c)
        zmap = lambda bb, c: (bb, c, COL_Z // SSM_WIDTH)
    else:
        tmap = lambda bb, c: (bb, nc - 1 - c, 0)
        tmap_t = lambda bb, c: (bb, 0, nc - 1 - c)
        zmap = lambda bb, c: (bb, nc - 1 - c, COL_Z // SSM_WIDTH)
    st_spec = pl.BlockSpec((nb, SSM_GROUPS, SSM_STATE, SSM_GROUP_WIDTH), lambda bb, c: (bb, 0, 0, 0))
    in_specs = [
        pl.BlockSpec((nb, q, SSM_WIDTH), tmap),
        pl.BlockSpec((nb, q, SSM_BC), tmap),
        pl.BlockSpec((nb, gs, q), tmap_t),
        pl.BlockSpec((nb, q, LANES), tmap),
        pl.BlockSpec((1, LANES), lambda bb, c: (0, 0)),
        pl.BlockSpec((2 * LANES, SSM_WIDTH), lambda bb, c: (0, 0)),
        st_spec,
    ]
    args = [xs, bc, bmt, dt, a_row, e_mat, h0]
    out_specs, out_shape, scratch = [], [], []
    if finish:
        y_fwd, proj, dsum_e, nw = fin
        in_specs += [
            pl.BlockSpec((nb, q, SSM_WIDTH), tmap),
            pl.BlockSpec((nb, q, SSM_WIDTH), zmap),
            pl.BlockSpec((1, SSM_WIDTH), lambda bb, c: (0, 0)),
            pl.BlockSpec((1, SSM_WIDTH), lambda bb, c: (0, 0)),
        ]
        args += [y_fwd, proj, dsum_e, nw]
    elif with_y:
        out_specs.append(pl.BlockSpec((nb, q, SSM_WIDTH), tmap))
        out_shape.append(jax.ShapeDtypeStruct((b, t, SSM_WIDTH), F32))
    out_specs.append(st_spec)
    out_shape.append(jax.ShapeDtypeStruct((b, SSM_GROUPS, SSM_STATE, SSM_GROUP_WIDTH), F32))
    scratch.append(pltpu.VMEM((nb, SSM_GROUPS, SSM_STATE, SSM_GROUP_WIDTH), F32))
    if finish:
        out_specs.append(pl.BlockSpec((nb, q, SSM_WIDTH), tmap))
        out_shape.append(jax.ShapeDtypeStruct((b, t, SSM_WIDTH), BF16))
    return pl.pallas_call(
        functools.partial(_ssd_kernel_ordered, direction=direction, with_y=with_y, finish=finish, nb=nb),
        grid=(b // nb, nc),
        in_specs=in_specs,
        out_specs=out_specs,
        out_shape=out_shape,
        scratch_shapes=scratch + ([pltpu.VMEM((nb, q, SSM_WIDTH), F32)] if finish else []),
        compiler_params=_cparams(("parallel", "arbitrary")),
        name="ssd_scan",
    )(*args)


def _ssd_kernel_ordered(*refs, direction, with_y, finish, nb):
    n_in = 11 if finish else 7
    ins = list(refs[:n_in])
    rest = list(refs[n_in:])
    if finish:
        hfin, out, st, ybuf = rest
        ordered = ins + [ybuf, hfin, st, out]
    elif with_y:
        y, hfin, st = rest
        ordered = ins + [y, hfin, st]
    else:
        hfin, st = rest
        ordered = ins + [hfin, st]
    _ssd_kernel(*ordered, direction=direction, with_y=with_y, finish=finish, nb=nb)


def _merge_kernel(x_ref, nw_ref, sh_ref, sc_ref, gt_ref, a_ref, b_ref, c_ref, wg0_ref, wg1_ref, wg2_ref,
                  wa_ref, wb_ref, wc_ref, wo_ref, o_ref, h_ref):
    j = pl.program_id(2)

    def contribution(rows, h):
        s = None
        for act_ref, wp_ref, wg_ref in ((a_ref, wa_ref, wg0_ref), (b_ref, wb_ref, wg1_ref), (c_ref, wc_ref, wg2_ref)):
            y = jnp.dot(act_ref[0, rows], wp_ref[...], preferred_element_type=F32)
            gate = _sigmoid(jnp.dot(h, wg_ref[...], preferred_element_type=F32))
            s = gate * y if s is None else s + gate * y
        return jnp.dot(s.astype(BF16), wo_ref[...], preferred_element_type=F32)

    @pl.when(j == 0)
    def _():
        for r in range(0, h_ref.shape[0], ROW_CHUNK):
            rows = slice(r, r + ROW_CHUNK)
            hc = _rms_mod(x_ref[0, rows], nw_ref[...], sh_ref[0], sc_ref[0]).astype(BF16)
            h_ref[rows] = hc
            o_ref[0, rows] = contribution(rows, hc)

    @pl.when(j > 0)
    def _():
        o_ref[0] += contribution(slice(None), h_ref[...])

    @pl.when(j == pl.num_programs(2) - 1)
    def _():
        o_ref[0] = x_ref[0] + gt_ref[0] * o_ref[0]


def _merge(x, nw, shift, scale, gate, act_a, act_b, act_c, wg, wa, wb, wc, wo, l, tm):
    b, t, d = x.shape
    tn = 512
    nj = d // tn
    vec = pl.BlockSpec((1, 1, d), lambda bb, i, j: (bb, 0, 0))
    act = pl.BlockSpec((1, tm, ATTN_WIDTH), lambda bb, i, j: (bb, i, 0))
    wproj = pl.BlockSpec((None, ATTN_WIDTH, tn), lambda bb, i, j: (l, 0, j))
    return pl.pallas_call(
        _merge_kernel,
        grid=(b, t // tm, nj),
        in_specs=[
            pl.BlockSpec((1, tm, d), lambda bb, i, j: (bb, i, 0)),
            pl.BlockSpec((1, d), lambda bb, i, j: (0, 0)),
            vec, vec, vec, act, act, act,
            pl.BlockSpec((None, d, tn), lambda bb, i, j: (l, 0, j)),
            pl.BlockSpec((None, d, tn), lambda bb, i, j: (l, 0, nj + j)),
            pl.BlockSpec((None, d, tn), lambda bb, i, j: (l, 0, 2 * nj + j)),
            wproj, wproj, wproj,
            pl.BlockSpec((None, tn, d), lambda bb, i, j: (l, j, 0)),
        ],
        out_specs=pl.BlockSpec((1, tm, d), lambda bb, i, j: (bb, i, 0)),
        out_shape=jax.ShapeDtypeStruct((b, t, d), F32),
        scratch_shapes=[pltpu.VMEM((tm, d), BF16)],
        compiler_params=_cparams(("parallel", "parallel", "arbitrary")),
        name="branch_merge",
    )(x, nw, shift, scale, gate, act_a, act_b, act_c, wg, wg, wg, wa, wb, wc, wo)


def _ffn_kernel(x_ref, xp_ref, xn_ref, nw_ref, sh_ref, sc_ref, gt_ref, wug_ref, wuv_ref, cwg_ref, cwv_ref,
                cbg_ref, cbv_ref, wd_ref, fw_ref, o_ref, h_ref, pg_ref, pv_ref, *, tm, final):
    i = pl.program_id(1)
    j = pl.program_id(2)
    rows = tm + 2 * HALO

    def conv(p, cw_ref, cb_ref):
        prev = pltpu.roll(p, 1, 0)[HALO:HALO + tm]
        nxt = pltpu.roll(p, rows - 1, 0)[HALO:HALO + tm]
        return prev * cw_ref[0:1, :] + p[HALO:HALO + tm] * cw_ref[1:2, :] + nxt * cw_ref[2:3, :] + cb_ref[...]

    def down(pg, pv):
        s = (_silu(conv(pg, cwg_ref, cbg_ref)) * conv(pv, cwv_ref, cbv_ref)).astype(BF16)
        return jnp.dot(s, wd_ref[...], preferred_element_type=F32)

    @pl.when(j == 0)
    def _():
        nw, sh, sc = nw_ref[...], sh_ref[0], sc_ref[0]
        for r in range(0, tm, ROW_CHUNK):
            hc = _rms_mod(x_ref[0, r:r + ROW_CHUNK], nw, sh, sc).astype(BF16)
            lo, hi = HALO + r, HALO + r + ROW_CHUNK
            if r == 0:
                hp = jnp.where(i > 0, _rms_mod(xp_ref[0], nw, sh, sc), 0.0).astype(BF16)
                hc, lo = jnp.concatenate([hp, hc], axis=0), 0
            if r + ROW_CHUNK == tm:
                hn = jnp.where(i < pl.num_programs(1) - 1, _rms_mod(xn_ref[0], nw, sh, sc), 0.0).astype(BF16)
                hc, hi = jnp.concatenate([hc, hn], axis=0), rows
            h_ref[lo:hi] = hc
            pg_ref[lo:hi] = jnp.dot(hc, wug_ref[...], preferred_element_type=F32)
            pv_ref[lo:hi] = jnp.dot(hc, wuv_ref[...], preferred_element_type=F32)
        o_ref[0] = down(pg_ref[...], pv_ref[...])

    @pl.when(j > 0)
    def _():
        h = h_ref[...]
        o_ref[0] += down(jnp.dot(h, wug_ref[...], preferred_element_type=F32),
                         jnp.dot(h, wuv_ref[...], preferred_element_type=F32))

    @pl.when(j == pl.num_programs(2) - 1)
    def _():
        y = x_ref[0] + gt_ref[0] * o_ref[0]
        if final:
            y = y * lax.rsqrt(jnp.mean(y * y, axis=-1, keepdims=True) + EPS) * fw_ref[...]
        o_ref[0] = y


def _ffn(x, nw, shift, scale, gate, w_up, conv_w, conv_b, w_down, l, tm, final_w=None):
    b, t, d = x.shape
    final = final_w is not None
    if not final:
        final_w = nw
    dff = w_down.shape[1]
    tf = 512
    nj = dff // tf
    per = tm // HALO
    last = t // HALO - 1
    vec = pl.BlockSpec((1, 1, d), lambda bb, i, j: (bb, 0, 0))
    return pl.pallas_call(
        functools.partial(_ffn_kernel, tm=tm, final=final),
        grid=(b, t // tm, nj),
        in_specs=[
            pl.BlockSpec((1, tm, d), lambda bb, i, j: (bb, i, 0)),
            pl.BlockSpec((1, HALO, d), lambda bb, i, j: (bb, jnp.maximum(i * per - 1, 0), 0)),
            pl.BlockSpec((1, HALO, d), lambda bb, i, j: (bb, jnp.minimum((i + 1) * per, last), 0)),
            pl.BlockSpec((1, d), lambda bb, i, j: (0, 0)),
            vec, vec, vec,
            pl.BlockSpec((None, d, tf), lambda bb, i, j: (l, 0, j)),
            pl.BlockSpec((None, d, tf), lambda bb, i, j: (l, 0, nj + j)),
            pl.BlockSpec((FFN_CONV, tf), lambda bb, i, j: (0, j)),
            pl.BlockSpec((FFN_CONV, tf), lambda bb, i, j: (0, nj + j)),
            pl.BlockSpec((1, tf), lambda bb, i, j: (0, j)),
            pl.BlockSpec((1, tf), lambda bb, i, j: (0, nj + j)),
            pl.BlockSpec((None, tf, d), lambda bb, i, j: (l, j, 0)),
            pl.BlockSpec((1, d), lambda bb, i, j: (0, 0)),
        ],
        out_specs=pl.BlockSpec((1, tm, d), lambda bb, i, j: (bb, i, 0), pipeline_mode=pl.Buffered(1)),
        out_shape=jax.ShapeDtypeStruct((b, t, d), F32),
        scratch_shapes=[
            pltpu.VMEM((tm + 2 * HALO, d), BF16),
            pltpu.VMEM((tm + 2 * HALO, tf), F32),
            pltpu.VMEM((tm + 2 * HALO, tf), F32),
        ],
        compiler_params=_cparams(("parallel", "parallel", "arbitrary")),
        name="conv_ffn",
    )(x, x, x, nw, shift, scale, gate, w_up, w_up, conv_w, conv_w, conv_b, conv_b, w_down, final_w)


def _rope_tables(n_tokens):
    rows = n_tokens // GRID_W
    row = jnp.repeat(jnp.arange(rows), GRID_W).astype(F32)
    col = jnp.tile(jnp.arange(GRID_W), rows).astype(F32)
    inv = ROPE_THETA ** (-jnp.arange(0, ROPE_AXIS_DIM, 2, dtype=F32) / ROPE_AXIS_DIM)
    ar, ac = row[:, None] * inv, col[:, None] * inv
    cos = jnp.concatenate([jnp.cos(ar), jnp.cos(ar), jnp.cos(ac), jnp.cos(ac)], axis=1)
    sin = jnp.concatenate([-jnp.sin(ar), jnp.sin(ar), -jnp.sin(ac), jnp.sin(ac)], axis=1)
    return cos, sin


def _expansion_matrix(direction):
    head_of_col = jnp.arange(SSM_WIDTH) // SSM_HEAD_DIM
    row = jnp.arange(LANES)[:, None]
    e = (row == head_of_col[None, :] + SSM_HEADS * direction).astype(BF16)
    return jnp.concatenate([e, e], axis=0)


def _pad_lanes(v, offset):
    return jnp.zeros((1, LANES), F32).at[0, offset:offset + v.shape[0]].set(v.astype(F32))


def kernel(x, c, ctx, c_ctx, w_mod, b_mod, norm_mix_w, norm_ffn_w, w_in, q_norm_w, k_norm_w, w_attn_o, conf_conv_w, conf_conv_b, conf_ln_w, conf_ln_b, w_conf_o, ssm_conv_w, ssm_conv_b, ssm_a_log, ssm_dt_bias, ssm_d, ssm_norm_w, w_ssm_o, w_out, ffn_w_up, ffn_conv_w, ffn_conv_b, ffn_w_down, final_norm_w):
    bsz, n_lat, d = x.shape
    n_ctx = ctx.shape[1]
    depth = w_mod.shape[0]
    assert bsz <= 7 and n_lat % 512 == 0 and n_ctx % 256 == 0

    q_end = ATTN_WIDTH
    v_end = q_end + 2 * KV_WIDTH
    conf_end = v_end + 2 * CONF_WIDTH
    z_end = conf_end + SSM_WIDTH
    xbc_end = z_end + SSM_WIDTH + SSM_BC
    dt_end = xbc_end + 2 * SSM_HEADS

    cc = jnp.zeros((8, d), F32).at[:bsz].set(c).at[bsz].set(c_ctx)
    mods = _mod_vectors(cc, w_mod, b_mod).reshape(depth, 8, N_MOD, d)
    rope = _rope_tables(n_lat)
    e_mats = [_expansion_matrix(0), _expansion_matrix(1)]
    zero_state = jnp.zeros((bsz, SSM_GROUPS, SSM_STATE, SSM_GROUP_WIDTH), F32)

    w_mix, w_gate = _winprep(w_in, dict(q_end=q_end, v_end=v_end, xs_end=z_end + SSM_WIDTH, bc_end=xbc_end,
                                        dt_end=dt_end))
    wa, wb, wc, wo = (w.astype(BF16) for w in (w_attn_o, w_conf_o, w_ssm_o, w_out))
    w_up, w_down = ffn_w_up.astype(BF16), ffn_w_down.astype(BF16)

    for l in range(depth):
        need_ctx = l < depth - 1
        row = lambda v: v.reshape(1, -1)
        nmix, nffn = row(norm_mix_w[l]), row(norm_ffn_w[l])
        cwx, cwb = ssm_conv_w[l][:, :SSM_WIDTH], ssm_conv_w[l][:, SSM_WIDTH:]
        cbx, cbb = row(ssm_conv_b[l][:SSM_WIDTH]), row(ssm_conv_b[l][SSM_WIDTH:])
        a_coef = -jnp.exp(ssm_a_log[l].astype(F32))
        a_rows = [_pad_lanes(a_coef[0], 0), _pad_lanes(a_coef[1], SSM_HEADS)]
        dtb = _pad_lanes(ssm_dt_bias[l].reshape(-1), 0)
        dsum_e = row(jnp.repeat(ssm_d[l][0] + ssm_d[l][1], SSM_HEAD_DIM))
        m_lat = [mods[l, :bsz, k].reshape(bsz, 1, d) for k in range(N_MOD)]
        m_ctx = [jnp.broadcast_to(mods[l, bsz, k].reshape(1, 1, d), (bsz, 1, d)) for k in range(N_MOD)]

        proj_l = _inproj(x, nmix, m_lat[0], m_lat[1], w_mix, l, 1024 if n_lat % 1024 == 0 else 512)
        flat = lambda a: a.reshape(1, bsz * n_ctx, a.shape[-1])
        ctx_tm = 512 if (bsz * n_ctx) % 512 == 0 else 256
        proj_c = _inproj(flat(ctx), nmix, m_ctx[0][:1], m_ctx[1][:1], w_mix, l, ctx_tm).reshape(bsz, n_ctx, MIX_WIDTH)

        q_l, k_l, v_l = _qkprep(proj_l, row(q_norm_w[l]), row(k_norm_w[l]), rope, 512)
        q_c, k_c, v_c = _qkprep(proj_c, row(q_norm_w[l]), row(k_norm_w[l]), None, 256)
        act_a_l = _attention(q_l, [(k_c, v_c), (k_l, v_l)], 512, (256, 512))
        conf_args = (conf_conv_w[l], row(conf_conv_b[l]), row(conf_ln_w[l]), row(conf_ln_b[l]))
        act_b_l = _conformer(proj_l, *conf_args, 256)

        prep_l = _ssdprep(proj_l, cwx, cbx, cwb, cbb, dtb, 512)
        prep_c = _ssdprep(proj_c, cwx, cbx, cwb, cbb, dtb, 256)
        fin_args = (dsum_e, row(ssm_norm_w[l]))
        if need_ctx:
            y_cf, h_cf = _ssd_scan(*prep_c, a_rows[0], e_mats[0], zero_state, 0, True)
            h_cb, act_c_c = _ssd_scan(*prep_c, a_rows[1], e_mats[1], zero_state, 1, True, (y_cf, proj_c) + fin_args)
        else:
            (h_cf,) = _ssd_scan(*prep_c, a_rows[0], e_mats[0], zero_state, 0, False)
            (h_cb,) = _ssd_scan(*prep_c, a_rows[1], e_mats[1], zero_state, 1, False)
        y_lf, _ = _ssd_scan(*prep_l, a_rows[0], e_mats[0], h_cf, 0, True)
        _, act_c_l = _ssd_scan(*prep_l, a_rows[1], e_mats[1], h_cb, 1, True, (y_lf, proj_l) + fin_args)

        x = _merge(x, nmix, m_lat[0], m_lat[1], m_lat[2], act_a_l, act_b_l, act_c_l, w_gate, wa, wb, wc, wo, l, 512)
        ffn_args = (w_up, ffn_conv_w[l], row(ffn_conv_b[l]), w_down, l)
        x = _ffn(x, nffn, m_lat[3], m_lat[4], m_lat[5], *ffn_args, 1024 if n_lat % 1024 == 0 else 512,
                 final_w=None if need_ctx else final_norm_w.reshape(1, d))
        if need_ctx:
            act_a_c = _attention(q_c, [(k_c, v_c)], 256, (256,))
            act_b_c = _conformer(proj_c, *conf_args, 256)
            ctx = _merge(flat(ctx), nmix, m_ctx[0][:1], m_ctx[1][:1], m_ctx[2][:1], flat(act_a_c), flat(act_b_c),
                         flat(act_c_c), w_gate, wa, wb, wc, wo, l, ctx_tm).reshape(bsz, n_ctx, d)
            ctx = _ffn(ctx, nffn, m_ctx[3], m_ctx[4], m_ctx[5], *ffn_args, 256)
    return x
```
